```python
import jax, jax.numpy as jnp
from jax import lax
import numpy as np

D_MODEL = 2048
BATCH = 2
SEQ = 8192
DEPTH = 2

HEAD_DIM = 128
BLOCK = 128
A_HEADS = D_MODEL // (2 * HEAD_DIM)
DILATED_GROUPS = ((128, 1), (512, 4), (2048, 16))
B_HEADS = D_MODEL // (2 * HEAD_DIM)
MLA_Q_LORA = 512
MLA_KV_LORA = 256
MLA_NOPE = 128
MLA_ROPE = 64
MLA_V = 128
C_HEADS = D_MODEL // HEAD_DIM
PEER_HEADS = 8
PEER_N_KEYS = 128
PEER_N_EXPERTS = PEER_N_KEYS * PEER_N_KEYS
PEER_TOPK = 16
PEER_QUERY_DIM = 256
PEER_HALF = PEER_QUERY_DIM // 2
PEER_CHUNK = 128
ROPE_THETA = 10000.0
LN_EPS = 1e-5
RMS_EPS = 1e-6
NEG_INF = -1e30
DN_ALPHA = (2 * DEPTH) ** 0.25
DN_BETA = (8 * DEPTH) ** -0.25
N_EVEN = (DEPTH + 1) // 2
N_ODD = DEPTH // 2
A_QKV = 3 * A_HEADS * HEAD_DIM
EVEN_IN = A_QKV + MLA_Q_LORA + MLA_KV_LORA + MLA_ROPE
EVEN_MIX = A_HEADS * HEAD_DIM + B_HEADS * MLA_V
ODD_IN = 3 * C_HEADS * HEAD_DIM
ODD_MIX = C_HEADS * HEAD_DIM

kernel_name = "hybrid_dilated_mla_stickbreak_peer_deepnorm"


def layer_norm(x, g, b):
    xf = x.astype(jnp.float32)
    mu = jnp.mean(xf, -1, keepdims=True)
    var = jnp.mean(jnp.square(xf - mu), -1, keepdims=True)
    y = (xf - mu) * lax.rsqrt(var + LN_EPS)
    return (y * g.astype(jnp.float32) + b.astype(jnp.float32)).astype(x.dtype)


def rms_norm(x, g):
    xf = x.astype(jnp.float32)
    y = xf * lax.rsqrt(jnp.mean(jnp.square(xf), -1, keepdims=True) + RMS_EPS)
    return (y * g.astype(jnp.float32)).astype(x.dtype)


def rope(x, pos):
    half = x.shape[-1] // 2
    inv = ROPE_THETA ** (-jnp.arange(half, dtype=jnp.float32) / half)
    ang = pos.astype(jnp.float32)[:, None] * inv[None, :]
    cos = jnp.cos(ang)[None, :, None, :]
    sin = jnp.sin(ang)[None, :, None, :]
    xf = x.astype(jnp.float32)
    x1, x2 = xf[..., :half], xf[..., half:]
    return jnp.concatenate([x1 * cos - x2 * sin, x1 * sin + x2 * cos], -1).astype(x.dtype)


def dilated_window_attention(q, k, v, dilation, steps):
    B, S, H, Dh = q.shape
    L = S // dilation
    Lp = -(-L // BLOCK) * BLOCK
    nb = Lp // BLOCK

    def to_sub(t):
        t = t.reshape(B, L, dilation, H, Dh).transpose(0, 2, 3, 1, 4)
        t = jnp.pad(t, ((0, 0), (0, 0), (0, 0), (0, Lp - L), (0, 0)))
        return t.reshape(B, dilation, H, nb, BLOCK, Dh)

    def with_prev(t):
        prev = jnp.pad(t, ((0, 0), (0, 0), (0, 0), (1, 0), (0, 0), (0, 0)))[:, :, :, :-1]
        return jnp.concatenate([prev, t], axis=4)

    qs = to_sub(q)
    kk = with_prev(to_sub(k))
    vv = with_prev(to_sub(v))
    logits = jnp.einsum('brhnqd,brhnkd->brhnqk', qs, kk).astype(jnp.float32) * (Dh ** -0.5)
    a = jnp.arange(BLOCK)[:, None]
    c = jnp.arange(2 * BLOCK)[None, :]
    rel = a + BLOCK - c
    band = (rel >= 0) & (rel <= steps)
    valid = (jnp.arange(nb)[:, None, None] > 0) | (c[None] >= BLOCK)
    mask = band[None] & valid
    logits = jnp.where(mask, logits, NEG_INF)
    m = jnp.max(logits, -1, keepdims=True)
    p = jnp.exp(logits - m)
    denom = jnp.sum(p, -1, keepdims=True)
    out = jnp.einsum('brhnqk,brhnkd->brhnqd', (p / denom).astype(v.dtype), vv)
    lse = (m + jnp.log(denom))[..., 0]
    out = out.reshape(B, dilation, H, Lp, Dh)[:, :, :, :L].transpose(0, 3, 1, 2, 4).reshape(B, S, H, Dh)
    lse = lse.reshape(B, dilation, H, Lp)[..., :L].transpose(0, 3, 1, 2).reshape(B, S, H)
    return out, lse


def causal_block_attention(q, k, v, scale):
    B, S, H, dq = q.shape
    nb = S // BLOCK
    qb = q.reshape(B, nb, BLOCK, H, dq).transpose(1, 0, 2, 3, 4)
    starts = jnp.arange(nb) * BLOCK
    kpos = jnp.arange(S)

    def one(args):
        qc, start = args
        s = jnp.einsum('bqhd,bkhd->bhqk', qc, k).astype(jnp.float32) * scale
        mask = kpos[None, :] <= (start + jnp.arange(BLOCK))[:, None]
        p = jax.nn.softmax(jnp.where(mask, s, NEG_INF), axis=-1)
        return jnp.einsum('bhqk,bkhd->bqhd', p.astype(v.dtype), v)

    out = lax.map(one, (qb, starts))
    return out.transpose(1, 0, 2, 3, 4).reshape(B, S, H, v.shape[-1])


def stick_breaking_attention(q, k, v):
    B, S, H, Dh = q.shape
    nb = S // BLOCK
    qb = q.reshape(B, nb, BLOCK, H, Dh).transpose(1, 0, 2, 3, 4)
    starts = jnp.arange(nb) * BLOCK
    kpos = jnp.arange(S)
    scale = Dh ** -0.5

    def one(args):
        qc, start = args
        z = jnp.einsum('bqhd,bkhd->bhqk', qc, k).astype(jnp.float32) * scale
        mask = kpos[None, :] < (start + jnp.arange(BLOCK))[:, None]
        log_beta = jax.nn.log_sigmoid(z)
        log_rest = jnp.where(mask, jax.nn.log_sigmoid(-z), 0.0)
        suffix = lax.cumsum(log_rest, axis=3, reverse=True) - log_rest
        att = jnp.where(mask, jnp.exp(log_beta + suffix), 0.0)
        return jnp.einsum('bhqk,bkhd->bqhd', att.astype(v.dtype), v)

    out = lax.map(one, (qb, starts))
    return out.transpose(1, 0, 2, 3, 4).reshape(B, S, H, Dh)


def even_mixer(x, w_in, q_norm, w_q_b, kv_norm, w_kv_b, w_out):
    B, S, _ = x.shape
    pos = jnp.arange(S)
    h = x @ w_in
    o1 = A_QKV
    o2 = o1 + MLA_Q_LORA
    o3 = o2 + MLA_KV_LORA
    qkv = h[..., :o1].reshape(B, S, 3, A_HEADS, HEAD_DIM)
    qa = rope(qkv[:, :, 0], pos)
    ka = rope(qkv[:, :, 1], pos)
    va = qkv[:, :, 2]
    outs, lses = [], []
    for window, dil in DILATED_GROUPS:
        o, l = dilated_window_attention(qa, ka, va, dil, window // dil)
        outs.append(o)
        lses.append(l)
    wts = jax.nn.softmax(jnp.stack(lses, 0), axis=0)
    out_a = jnp.sum(wts[..., None] * jnp.stack(outs, 0).astype(jnp.float32), 0).astype(x.dtype)
    c_q = rms_norm(h[..., o1:o2], q_norm)
    q = (c_q @ w_q_b).reshape(B, S, B_HEADS, MLA_NOPE + MLA_ROPE)
    q = jnp.concatenate([q[..., :MLA_NOPE], rope(q[..., MLA_NOPE:], pos)], -1)
    c_kv = rms_norm(h[..., o2:o3], kv_norm)
    kv = (c_kv @ w_kv_b).reshape(B, S, B_HEADS, MLA_NOPE + MLA_V)
    k_rope = rope(h[..., o3:].reshape(B, S, 1, MLA_ROPE), pos)
    k = jnp.concatenate([kv[..., :MLA_NOPE], jnp.broadcast_to(k_rope, (B, S, B_HEADS, MLA_ROPE))], -1)
    out_b = causal_block_attention(q, k, kv[..., MLA_NOPE:], (MLA_NOPE + MLA_ROPE) ** -0.5)
    cat = jnp.concatenate([out_a.reshape(B, S, -1), out_b.reshape(B, S, -1)], -1)
    return cat @ w_out


def odd_mixer(x, w_in, w_out):
    B, S, _ = x.shape
    qkv = (x @ w_in).reshape(B, S, 3, C_HEADS, HEAD_DIM)
    o = stick_breaking_attention(qkv[:, :, 0], qkv[:, :, 1], qkv[:, :, 2])
    return o.reshape(B, S, -1) @ w_out


def peer(x, w_query, sub_keys, u_table, v_table):
    B, S, D = x.shape
    xt = x.reshape(-1, PEER_CHUNK, D)

    def chunk(xc):
        T = xc.shape[0]
        q = (xc @ w_query).reshape(T, PEER_HEADS, 2, PEER_HALF)
        s1 = jnp.einsum('thk,hnk->thn', q[:, :, 0], sub_keys[:, 0]).astype(jnp.float32)
        s2 = jnp.einsum('thk,hnk->thn', q[:, :, 1], sub_keys[:, 1]).astype(jnp.float32)
        v1, i1 = lax.top_k(s1, PEER_TOPK)
        v2, i2 = lax.top_k(s2, PEER_TOPK)
        cand = (v1[..., :, None] + v2[..., None, :]).reshape(T, PEER_HEADS, PEER_TOPK * PEER_TOPK)
        cand_idx = (i1[..., :, None] * PEER_N_KEYS + i2[..., None, :]).reshape(T, PEER_HEADS, -1)
        top_s, sel = lax.top_k(cand, PEER_TOPK)
        idx = jnp.take_along_axis(cand_idx, sel, axis=-1)
        g = jax.nn.softmax(top_s, axis=-1)
        u = u_table[idx]
        act = jax.nn.gelu(jnp.einsum('thkd,td->thk', u, xc).astype(jnp.float32), approximate=False)
        vsel = v_table[idx]
        return jnp.einsum('thk,thkd->td', (g * act).astype(x.dtype), vsel)

    return lax.map(chunk, xt).reshape(B, S, D)


def setup_inputs(seed: int = 0) -> dict:
    key = jax.random.key(seed)
    ks = jax.random.split(key, 16)
    f32 = jnp.float32

    def nrm(k, shape, scale):
        return jax.random.normal(k, shape, f32) * scale

    return {
        "x": nrm(ks[0], (BATCH, SEQ, D_MODEL), 1.0),
        "a_w_in": nrm(ks[1], (N_EVEN, D_MODEL, EVEN_IN), D_MODEL ** -0.5),
        "b_q_norm": 1.0 + nrm(ks[2], (N_EVEN, MLA_Q_LORA), 0.01),
        "b_w_q_b": nrm(ks[3], (N_EVEN, MLA_Q_LORA, B_HEADS * (MLA_NOPE + MLA_ROPE)), MLA_Q_LORA ** -0.5),
        "b_kv_norm": 1.0 + nrm(ks[4], (N_EVEN, MLA_KV_LORA), 0.01),
        "b_w_kv_b": nrm(ks[5], (N_EVEN, MLA_KV_LORA, B_HEADS * (MLA_NOPE + MLA_V)), MLA_KV_LORA ** -0.5),
        "ab_w_out": nrm(ks[6], (N_EVEN, EVEN_MIX, D_MODEL), DN_BETA * EVEN_MIX ** -0.5),
        "c_w_in": nrm(ks[7], (N_ODD, D_MODEL, ODD_IN), D_MODEL ** -0.5),
        "c_w_out": nrm(ks[8], (N_ODD, ODD_MIX, D_MODEL), DN_BETA * ODD_MIX ** -0.5),
        "peer_w_query": nrm(ks[9], (DEPTH, D_MODEL, PEER_HEADS * PEER_QUERY_DIM), D_MODEL ** -0.5),
        "peer_sub_keys": nrm(ks[10], (DEPTH, PEER_HEADS, 2, PEER_N_KEYS, PEER_HALF), PEER_HALF ** -0.5),
        "peer_u": nrm(ks[11], (DEPTH, PEER_N_EXPERTS, D_MODEL), D_MODEL ** -0.5),
        "peer_v": nrm(ks[12], (DEPTH, PEER_N_EXPERTS, D_MODEL), DN_BETA),
        "ln_gain": 1.0 + nrm(ks[13], (DEPTH, 2, D_MODEL), 0.01),
        "ln_bias": nrm(ks[14], (DEPTH, 2, D_MODEL), 0.01),
    }


def reference(x, a_w_in, b_q_norm, b_w_q_b, b_kv_norm, b_w_kv_b, ab_w_out, c_w_in, c_w_out,
              peer_w_query, peer_sub_keys, peer_u, peer_v, ln_gain, ln_bias):
    for layer in range(DEPTH):
        i = layer // 2
        if layer % 2 == 0:
            mix = even_mixer(x, a_w_in[i], b_q_norm[i], b_w_q_b[i], b_kv_norm[i], b_w_kv_b[i], ab_w_out[i])
        else:
            mix = odd_mixer(x, c_w_in[i], c_w_out[i])
        x = layer_norm(DN_ALPHA * x + mix, ln_gain[layer, 0], ln_bias[layer, 0])
        ffn = peer(x, peer_w_query[layer], peer_sub_keys[layer], peer_u[layer], peer_v[layer])
        x = layer_norm(DN_ALPHA * x + ffn, ln_gain[layer, 1], ln_bias[layer, 1])
    return x
```

```python
import functools

import jax
import jax.numpy as jnp
from jax import lax
from jax.experimental import pallas as pl
from jax.experimental.pallas import tpu as pltpu

F32 = jnp.float32
MXU_DTYPE = jnp.bfloat16

HEAD_DIM = 128
BLOCK = 128
DIL_STEPS = 128
DILATIONS = (1, 4, 16)
DIL_CHUNK = BLOCK * DILATIONS[-1]
A_HEADS = 8
B_HEADS = 8
MLA_Q_LORA = 512
MLA_KV_LORA = 256
MLA_NOPE = 128
MLA_ROPE = 64
MLA_V = 128
MLA_QK_PAD = 256
C_HEADS = 16
PEER_HEADS = 8
PEER_N_KEYS = 128
PEER_TOPK = 16
PEER_HALF = 128
ROPE_THETA = 10000.0
LN_EPS = 1e-5
RMS_EPS = 1e-6
NEG_INF = -1e30

V7X_VMEM_LIMIT_BYTES = 56 * 1024 * 1024
NT_DIMS = (((1,), (1,)), ((), ()))
TN_DIMS = (((0,), (0,)), ((), ()))


def _params(n_axes):
    return pltpu.CompilerParams(dimension_semantics=("arbitrary",) * n_axes,
                                vmem_limit_bytes=V7X_VMEM_LIMIT_BYTES)


def _mm_kernel(a_ref, b_ref, o_ref):
    a = a_ref[...].astype(MXU_DTYPE)
    o_ref[...] = jnp.dot(a, b_ref[...], preferred_element_type=F32).astype(o_ref.dtype)


def _matmul(a, b, out_dtype, bm=512, bn=1024):
    m, k = a.shape
    n = b.shape[1]
    bm, bn = min(bm, m), min(bn, n)
    assert m % bm == 0 and n % bn == 0
    return pl.pallas_call(
        _mm_kernel,
        grid=(m // bm, n // bn),
        in_specs=[pl.BlockSpec((bm, k), lambda i, j: (i, 0)),
                  pl.BlockSpec((k, bn), lambda i, j: (0, j))],
        out_specs=pl.BlockSpec((bm, bn), lambda i, j: (i, j)),
        out_shape=jax.ShapeDtypeStruct((m, n), out_dtype),
        compiler_params=_params(2),
        name="matmul",
    )(a, b)


def _layer_norm_rows(y, g, b):
    mu = jnp.mean(y, axis=-1, keepdims=True)
    d = y - mu
    var = jnp.mean(d * d, axis=-1, keepdims=True)
    return d * lax.rsqrt(var + LN_EPS) * g + b


def _mm_ln_kernel(*refs, n_pairs, alpha):
    a_refs = refs[:n_pairs]
    w_refs = refs[n_pairs:2 * n_pairs]
    res_ref, g_ref, b_ref, o_ref, ob_ref = refs[2 * n_pairs:]
    acc = jnp.dot(a_refs[0][...].astype(MXU_DTYPE), w_refs[0][...], preferred_element_type=F32)
    for a_ref, w_ref in zip(a_refs[1:], w_refs[1:]):
        acc = acc + jnp.dot(a_ref[...].astype(MXU_DTYPE), w_ref[...], preferred_element_type=F32)
    y = _layer_norm_rows(alpha * res_ref[...] + acc, g_ref[...], b_ref[...])
    o_ref[...] = y
    ob_ref[...] = y.astype(ob_ref.dtype)


def _matmul_residual_ln(a_list, w_list, res, gain, bias, alpha, bm=256):
    m, n = res.shape
    n_pairs = len(a_list)
    in_specs = [pl.BlockSpec((bm, a.shape[1]), lambda i: (i, 0)) for a in a_list]
    in_specs += [pl.BlockSpec(w.shape, lambda i: (0, 0)) for w in w_list]
    in_specs += [pl.BlockSpec((bm, n), lambda i: (i, 0)),
                 pl.BlockSpec((1, n), lambda i: (0, 0)),
                 pl.BlockSpec((1, n), lambda i: (0, 0))]
    return pl.pallas_call(
        functools.partial(_mm_ln_kernel, n_pairs=n_pairs, alpha=alpha),
        grid=(m // bm,),
        in_specs=in_specs,
        out_specs=[pl.BlockSpec((bm, n), lambda i: (i, 0)), pl.BlockSpec((bm, n), lambda i: (i, 0))],
        out_shape=[jax.ShapeDtypeStruct((m, n), F32), jax.ShapeDtypeStruct((m, n), MXU_DTYPE)],
        compiler_params=_params(1),
        name="matmul_residual_ln",
    )(*a_list, *w_list, res, gain.reshape(1, n), bias.reshape(1, n))


def _rope_tables(seq):
    pos = jnp.arange(seq, dtype=F32)[:, None]

    def cos_sin(half):
        inv = ROPE_THETA ** (-jnp.arange(half, dtype=F32) / half)
        ang = pos * inv[None, :]
        return jnp.cos(ang), jnp.sin(ang)

    c, s = cos_sin(HEAD_DIM // 2)
    cos_a = jnp.concatenate([c, c], axis=1)
    sin_a = jnp.concatenate([-s, s], axis=1)
    c, s = cos_sin(MLA_ROPE // 2)
    z32 = jnp.zeros_like(c)
    cos_b = jnp.concatenate([c, c, z32, z32], axis=1)
    sin_lo = jnp.concatenate([-s, z32, z32, z32], axis=1)
    sin_hi = jnp.concatenate([z32, s, z32, z32], axis=1)
    return cos_a, sin_a, cos_b, sin_lo, sin_hi


def _rope128(x, cos_a, sin_a):
    return x * cos_a + pltpu.roll(x, HEAD_DIM // 2, 1) * sin_a


def _rope64(x, cos_b, sin_lo, sin_hi):
    return x * cos_b + pltpu.roll(x, 96, 1) * sin_lo + pltpu.roll(x, 32, 1) * sin_hi


def _rms_norm_rows(x, g):
    return x * lax.rsqrt(jnp.mean(x * x, axis=-1, keepdims=True) + RMS_EPS) * g


def _even_prep_kernel(qk_ref, tail_ref, cos_a_ref, sin_a_ref, cos_b_ref, sin_lo_ref, sin_hi_ref,
                      qn_ref, kvn_ref, qa_ref, ka_ref, cq_ref, ckv_ref, kr_ref):
    cos_a, sin_a = cos_a_ref[...], sin_a_ref[...]
    n_qk = A_HEADS * HEAD_DIM
    for hd in range(A_HEADS):
        lo = hd * HEAD_DIM
        qa_ref[:, lo:lo + HEAD_DIM] = _rope128(qk_ref[:, lo:lo + HEAD_DIM], cos_a, sin_a)
        ka_ref[:, lo:lo + HEAD_DIM] = _rope128(qk_ref[:, n_qk + lo:n_qk + lo + HEAD_DIM], cos_a, sin_a)
    cq_ref[...] = _rms_norm_rows(tail_ref[:, :MLA_Q_LORA], qn_ref[...]).astype(cq_ref.dtype)
    o2 = MLA_Q_LORA + MLA_KV_LORA
    ckv_ref[...] = _rms_norm_rows(tail_ref[:, MLA_Q_LORA:o2], kvn_ref[...]).astype(ckv_ref.dtype)
    kr = _rope64(tail_ref[:, o2:o2 + 128], cos_b_ref[...], sin_lo_ref[...], sin_hi_ref[...])
    kr_ref[...] = kr.astype(kr_ref.dtype)


def _even_prep(h, tables, q_norm, kv_norm, seq, rows=256):
    t = h.shape[0]
    n_qk = A_HEADS * HEAD_DIM
    sb = seq // rows
    tab_spec = pl.BlockSpec((rows, 128), lambda i: (i % sb, 0))
    return pl.pallas_call(
        _even_prep_kernel,
        grid=(t // rows,),
        in_specs=[pl.BlockSpec((rows, 2 * n_qk), lambda i: (i, 0)),
                  pl.BlockSpec((rows, 1024), lambda i: (i, 3)),
                  tab_spec, tab_spec, tab_spec, tab_spec, tab_spec,
                  pl.BlockSpec((1, MLA_Q_LORA), lambda i: (0, 0)),
                  pl.BlockSpec((1, MLA_KV_LORA), lambda i: (0, 0))],
        out_specs=[pl.BlockSpec((rows, n_qk), lambda i: (i, 0)),
                   pl.BlockSpec((rows, n_qk), lambda i: (i, 0)),
                   pl.BlockSpec((rows, MLA_Q_LORA), lambda i: (i, 0)),
                   pl.BlockSpec((rows, MLA_KV_LORA), lambda i: (i, 0)),
                   pl.BlockSpec((rows, 128), lambda i: (i, 0))],
        out_shape=[jax.ShapeDtypeStruct((t, n_qk), F32),
                   jax.ShapeDtypeStruct((t, n_qk), F32),
                   jax.ShapeDtypeStruct((t, MLA_Q_LORA), MXU_DTYPE),
                   jax.ShapeDtypeStruct((t, MLA_KV_LORA), MXU_DTYPE),
                   jax.ShapeDtypeStruct((t, 128), MXU_DTYPE)],
        compiler_params=_params(1),
        name="even_prep",
    )(h, h, *tables, q_norm.reshape(1, -1), kv_norm.reshape(1, -1))


def _mla_prep_kernel(q_ref, kn_ref, kr_ref, cos_b_ref, sin_lo_ref, sin_hi_ref, qm_ref, km_ref):
    cos_b, sin_lo, sin_hi = cos_b_ref[...], sin_lo_ref[...], sin_hi_ref[...]
    kr = kr_ref[...]
    for hd in range(B_HEADS):
        lo = hd * MLA_QK_PAD
        qm_ref[:, lo:lo + MLA_NOPE] = q_ref[:, lo:lo + MLA_NOPE].astype(qm_ref.dtype)
        q_rope = _rope64(q_ref[:, lo + MLA_NOPE:lo + MLA_QK_PAD], cos_b, sin_lo, sin_hi)
        qm_ref[:, lo + MLA_NOPE:lo + MLA_QK_PAD] = q_rope.astype(qm_ref.dtype)
        km_ref[:, lo:lo + MLA_NOPE] = kn_ref[:, hd * MLA_NOPE:(hd + 1) * MLA_NOPE]
        km_ref[:, lo + MLA_NOPE:lo + MLA_QK_PAD] = kr


def _mla_prep(qf, kvf, kr, tables, seq, rows=256):
    t = qf.shape[0]
    w = B_HEADS * MLA_QK_PAD
    sb = seq // rows
    tab_spec = pl.BlockSpec((rows, 128), lambda i: (i % sb, 0))
    return pl.pallas_call(
        _mla_prep_kernel,
        grid=(t // rows,),
        in_specs=[pl.BlockSpec((rows, w), lambda i: (i, 0)),
                  pl.BlockSpec((rows, B_HEADS * MLA_NOPE), lambda i: (i, 0)),
                  pl.BlockSpec((rows, 128), lambda i: (i, 0)),
                  tab_spec, tab_spec, tab_spec],
        out_specs=[pl.BlockSpec((rows, w), lambda i: (i, 0)), pl.BlockSpec((rows, w), lambda i: (i, 0))],
        out_shape=[jax.ShapeDtypeStruct((t, w), MXU_DTYPE), jax.ShapeDtypeStruct((t, w), MXU_DTYPE)],
        compiler_params=_params(1),
        name="mla_prep",
    )(qf, kvf, kr, *tables)


def _dilated_kernel(q_ref, kc_ref, kp_ref, vc_ref, vp_ref, o_ref,
                    kbuf, vbuf, o0, o1, o2, l0, l1, l2):
    c = pl.program_id(2)
    kbuf[:DIL_CHUNK, :] = kp_ref[...]
    kbuf[DIL_CHUNK:, :] = kc_ref[...]
    vbuf[:DIL_CHUNK, :] = vp_ref[...]
    vbuf[DIL_CHUNK:, :] = vc_ref[...]
    scale = HEAD_DIM ** -0.5
    row = lax.broadcasted_iota(jnp.int32, (BLOCK, 2 * BLOCK), 0)
    col = lax.broadcasted_iota(jnp.int32, (BLOCK, 2 * BLOCK), 1)
    band = (col >= row) & (col <= row + DIL_STEPS)

    def block(q_start, k_start, dil, first, o_g, l_g):
        if dil == 1:
            q_idx, k_idx = pl.ds(q_start, BLOCK), pl.ds(k_start, 2 * BLOCK)
        else:
            q_idx = pl.ds(q_start, BLOCK, stride=dil)
            k_idx = pl.ds(k_start, 2 * BLOCK, stride=dil)
        q = q_ref[q_idx, :].astype(MXU_DTYPE)
        k = kbuf[k_idx, :].astype(MXU_DTYPE)
        v = vbuf[k_idx, :].astype(MXU_DTYPE)
        logits = lax.dot_general(q, k, NT_DIMS, preferred_element_type=F32) * scale
        valid_from = jnp.where(first, BLOCK, 0)
        logits = jnp.where(band & (col >= valid_from), logits, NEG_INF)
        m = jnp.max(logits, axis=-1, keepdims=True)
        p = jnp.exp(logits - m)
        denom = jnp.sum(p, axis=-1, keepdims=True)
        out = jnp.dot((p / denom).astype(MXU_DTYPE), v, preferred_element_type=F32)
        o_g[q_idx, :] = out
        l_g[q_idx, :] = jnp.broadcast_to(m + jnp.log(denom), (BLOCK, HEAD_DIM))

    n_blocks = DIL_CHUNK // BLOCK

    def body(i, carry):
        block(pl.multiple_of(i * BLOCK, BLOCK), pl.multiple_of(DIL_CHUNK + (i - 1) * BLOCK, BLOCK),
              DILATIONS[0], (c == 0) & (i == 0), o0, l0)
        d1 = DILATIONS[1]
        n, r = i // d1, i % d1
        span = BLOCK * d1
        block(n * span + r, DIL_CHUNK + (n - 1) * span + r, d1, (c == 0) & (n == 0), o1, l1)
        block(i, DIL_CHUNK - BLOCK * DILATIONS[2] + i, DILATIONS[2], c == 0, o2, l2)
        return carry

    lax.fori_loop(0, n_blocks, body, 0)

    la, lb, lc = l0[...], l1[...], l2[...]
    mx = jnp.maximum(jnp.maximum(la, lb), lc)
    ea, eb, ec = jnp.exp(la - mx), jnp.exp(lb - mx), jnp.exp(lc - mx)
    merged = (ea * o0[...] + eb * o1[...] + ec * o2[...]) / (ea + eb + ec)
    o_ref[...] = merged.astype(o_ref.dtype)


def _dilated_attention(qa, ka, h, batch, seq):
    t = qa.shape[0]
    nc = seq // DIL_CHUNK
    v_col0 = 2 * A_HEADS
    cur = lambda b, hd, c: (b * nc + c, hd)
    prev = lambda b, hd, c: (b * nc + jnp.maximum(c - 1, 0), hd)
    blk = (DIL_CHUNK, HEAD_DIM)
    return pl.pallas_call(
        _dilated_kernel,
        grid=(batch, A_HEADS, nc),
        in_specs=[pl.BlockSpec(blk, cur),
                  pl.BlockSpec(blk, cur), pl.BlockSpec(blk, prev),
                  pl.BlockSpec(blk, lambda b, hd, c: (b * nc + c, v_col0 + hd)),
                  pl.BlockSpec(blk, lambda b, hd, c: (b * nc + jnp.maximum(c - 1, 0), v_col0 + hd))],
        out_specs=pl.BlockSpec(blk, cur),
        out_shape=jax.ShapeDtypeStruct((t, A_HEADS * HEAD_DIM), MXU_DTYPE),
        scratch_shapes=[pltpu.VMEM((2 * DIL_CHUNK, HEAD_DIM), F32)] * 2
                       + [pltpu.VMEM(blk, F32)] * 6,
        compiler_params=_params(3),
        name="dilated_attention",
    )(qa, ka, ka, h, h)


def _mla_kernel(q_ref, k_ref, v_ref, o_ref, *, scale, bq):
    qi = pl.program_id(2)
    q = q_ref[...]
    row = lax.broadcasted_iota(jnp.int32, (bq, bq), 0)
    col = lax.broadcasted_iota(jnp.int32, (bq, bq), 1)

    def step(k_start, carry, diagonal):
        m, l, acc = carry
        k = k_ref[pl.ds(k_start, bq), :]
        v = v_ref[pl.ds(k_start, bq), :]
        s = lax.dot_general(q, k, NT_DIMS, preferred_element_type=F32) * scale
        if diagonal:
            s = jnp.where(col <= row, s, NEG_INF)
        m_new = jnp.maximum(m, jnp.max(s, axis=-1, keepdims=True))
        alpha = jnp.exp(m - m_new)
        p = jnp.exp(s - m_new)
        l = alpha * l + jnp.sum(p, axis=-1, keepdims=True)
        acc = alpha * acc + jnp.dot(p.astype(MXU_DTYPE), v, preferred_element_type=F32)
        return m_new, l, acc

    carry = (jnp.full((bq, 1), NEG_INF, F32), jnp.zeros((bq, 1), F32), jnp.zeros((bq, MLA_V), F32))
    carry = lax.fori_loop(0, qi, lambda j, cr: step(pl.multiple_of(j * bq, bq), cr, False), carry)
    _, l, acc = step(pl.multiple_of(qi * bq, bq), carry, True)
    o_ref[...] = (acc / l).astype(o_ref.dtype)


def _mla_attention(qm, km, kvf, batch, seq, bq=512):
    t = qm.shape[0]
    bq = min(bq, seq)
    nq = seq // bq
    scale = (MLA_NOPE + MLA_ROPE) ** -0.5
    return pl.pallas_call(
        functools.partial(_mla_kernel, scale=scale, bq=bq),
        grid=(batch, B_HEADS, nq),
        in_specs=[pl.BlockSpec((bq, MLA_QK_PAD), lambda b, hd, i: (b * nq + i, hd)),
                  pl.BlockSpec((seq, MLA_QK_PAD), lambda b, hd, i: (b, hd)),
                  pl.BlockSpec((seq, MLA_V), lambda b, hd, i: (b, B_HEADS + hd))],
        out_specs=pl.BlockSpec((bq, MLA_V), lambda b, hd, i: (b * nq + i, hd)),
        out_shape=jax.ShapeDtypeStruct((t, B_HEADS * MLA_V), MXU_DTYPE),
        compiler_params=_params(3),
        name="mla_attention",
    )(qm, km, kvf)


def _stick_kernel(q_ref, k_ref, v_ref, o_ref, *, scale, bq):
    qi = pl.program_id(2)
    q = q_ref[...]
    row = lax.broadcasted_iota(jnp.int32, (bq, bq), 0)
    col = lax.broadcasted_iota(jnp.int32, (bq, bq), 1)
    later = jnp.where(row > col, 1.0, 0.0).astype(MXU_DTYPE)

    def step(k_start, carry, diagonal):
        run, acc = carry
        k = k_ref[pl.ds(k_start, bq), :]
        v = v_ref[pl.ds(k_start, bq), :]
        z = lax.dot_general(q, k, NT_DIMS, preferred_element_type=F32) * scale
        softplus = jnp.maximum(z, 0.0) + jnp.log1p(jnp.exp(-jnp.abs(z)))
        log_rest = -softplus
        log_beta = z - softplus
        if diagonal:
            log_rest = jnp.where(col < row, log_rest, 0.0)
        hi = log_rest.astype(MXU_DTYPE)
        lo = (log_rest - hi.astype(F32)).astype(MXU_DTYPE)
        suffix = (jnp.dot(hi, later, preferred_element_type=F32)
                  + jnp.dot(lo, later, preferred_element_type=F32))
        att = jnp.exp(log_beta + suffix + run)
        if diagonal:
            att = jnp.where(col < row, att, 0.0)
        acc = acc + jnp.dot(att.astype(MXU_DTYPE), v, preferred_element_type=F32)
        run = run + jnp.sum(log_rest, axis=-1, keepdims=True)
        return run, acc

    carry = (jnp.zeros((bq, 1), F32), jnp.zeros((bq, HEAD_DIM), F32))
    carry = step(pl.multiple_of(qi * bq, bq), carry, True)
    carry = lax.fori_loop(
        0, qi, lambda j, cr: step(pl.multiple_of((qi - 1 - j) * bq, bq), cr, False), carry)
    o_ref[...] = carry[1].astype(o_ref.dtype)


def _stick_attention(qkv, batch, seq, bq=256):
    t = qkv.shape[0]
    bq = min(bq, seq)
    nq = seq // bq
    return pl.pallas_call(
        functools.partial(_stick_kernel, scale=HEAD_DIM ** -0.5, bq=bq),
        grid=(batch, C_HEADS, nq),
        in_specs=[pl.BlockSpec((bq, HEAD_DIM), lambda b, hd, i: (b * nq + i, hd)),
                  pl.BlockSpec((seq, HEAD_DIM), lambda b, hd, i: (b, C_HEADS + hd)),
                  pl.BlockSpec((seq, HEAD_DIM), lambda b, hd, i: (b, 2 * C_HEADS + hd))],
        out_specs=pl.BlockSpec((bq, HEAD_DIM), lambda b, hd, i: (b * nq + i, hd)),
        out_shape=jax.ShapeDtypeStruct((t, C_HEADS * HEAD_DIM), MXU_DTYPE),
        compiler_params=_params(3),
        name="stick_breaking_attention",
    )(qkv, qkv, qkv)


def _top_values(s, count):
    vals = []
    cur = s
    for _ in range(count):
        m = jnp.max(cur, axis=0, keepdims=True)
        vals.append(m)
        cur = jnp.where(cur == m, NEG_INF, cur)
    return vals


def _peer_route_kernel(q_ref, keys_ref, s2_ref, b_ref, th_ref, a_ref):
    def head(hd, carry):
        col = pl.multiple_of(hd * 2 * PEER_HALF, 2 * PEER_HALF)
        q1 = q_ref[:, pl.ds(col, PEER_HALF)].astype(MXU_DTYPE)
        q2 = q_ref[:, pl.ds(col + PEER_HALF, PEER_HALF)].astype(MXU_DTYPE)
        s1 = lax.dot_general(keys_ref[hd, 0], q1, NT_DIMS, preferred_element_type=F32)
        s2 = lax.dot_general(keys_ref[hd, 1], q2, NT_DIMS, preferred_element_type=F32)
        v1 = _top_values(s1, PEER_TOPK)
        v2 = _top_values(s2, PEER_TOPK)
        v2_all = jnp.concatenate(v2, axis=0)
        cand = jnp.concatenate([v + v2_all for v in v1], axis=0)
        tops = _top_values(cand, PEER_TOPK + 1)
        cut = 0.5 * (tops[PEER_TOPK - 1] + tops[PEER_TOPK])
        top = tops[0]
        z = jnp.sum(jnp.where(cand >= cut, jnp.exp(cand - top), 0.0), axis=0, keepdims=True)
        a = jnp.where(s1 >= v1[PEER_TOPK - 1], jnp.exp(s1 - v1[0]) / z, 0.0)
        b = jnp.where(s2 >= v2[PEER_TOPK - 1], jnp.exp(s2 - v2[0]), 0.0)
        s2_ref[hd] = s2
        b_ref[hd] = b
        th_ref[hd] = cut - s1
        a_ref[hd] = a
        return carry

    lax.fori_loop(0, PEER_HEADS, head, 0)


def _peer_route(qp, keys, tr=256):
    t = qp.shape[0]
    shape = jax.ShapeDtypeStruct((PEER_HEADS, PEER_N_KEYS, t), F32)
    spec = pl.BlockSpec((PEER_HEADS, PEER_N_KEYS, tr), lambda i: (0, 0, i))
    return pl.pallas_call(
        _peer_route_kernel,
        grid=(t // tr,),
        in_specs=[pl.BlockSpec((tr, qp.shape[1]), lambda i: (i, 0)),
                  pl.BlockSpec(keys.shape, lambda i: (0, 0, 0, 0))],
        out_specs=[spec] * 4,
        out_shape=[shape] * 4,
        compiler_params=_params(1),
        name="peer_route",
    )(qp, keys)


def _gelu_exact(x):
    return 0.5 * x * (1.0 + lax.erf(x * (2.0 ** -0.5)))


def _peer_kernel(x_ref, res_ref, u_ref, v_ref, s2_ref, b_ref, th_ref, a_ref, g_ref, bias_ref,
                 o_ref, st_ref, wt_ref, *, n_i, tb, alpha):
    e = pl.program_id(1)
    st_ref[...] = lax.dot_general(u_ref[...], x_ref[...], NT_DIMS, preferred_element_type=F32)
    for tc in range(tb // 128):
        lanes = slice(tc * 128, (tc + 1) * 128)
        for ii in range(n_i):
            rows = slice(ii * PEER_N_KEYS, (ii + 1) * PEER_N_KEYS)
            gate = jnp.zeros((PEER_N_KEYS, 128), F32)
            for hd in range(PEER_HEADS):
                th = th_ref[hd, 0, ii:ii + 1, lanes]
                a = a_ref[hd, 0, ii:ii + 1, lanes]
                gate = gate + jnp.where(s2_ref[hd, :, lanes] >= th, b_ref[hd, :, lanes], 0.0) * a
            wt_ref[rows, lanes] = (_gelu_exact(st_ref[rows, lanes]) * gate).astype(wt_ref.dtype)
    contrib = lax.dot_general(wt_ref[...], v_ref[...], TN_DIMS, preferred_element_type=F32)

    @pl.when(e == 0)
    def _():
        o_ref[...] = contrib

    @pl.when(e > 0)
    def _():
        o_ref[...] += contrib

    @pl.when(e == pl.num_programs(1) - 1)
    def _():
        o_ref[...] = _layer_norm_rows(alpha * res_ref[...] + o_ref[...], g_ref[...], bias_ref[...])


def _peer_experts(xb, res, u, v, s2, b, th, a, gain, bias, alpha, tb=512, n_i=4):
    t, d = res.shape
    n_exp = u.shape[0]
    eb = n_i * PEER_N_KEYS
    tb = min(tb, t)
    th4 = th.reshape(PEER_HEADS, PEER_N_KEYS // n_i, n_i, t)
    a4 = a.reshape(PEER_HEADS, PEER_N_KEYS // n_i, n_i, t)
    tok = lambda i, e: (i, 0)
    full = pl.BlockSpec((PEER_HEADS, PEER_N_KEYS, tb), lambda i, e: (0, 0, i))
    part = pl.BlockSpec((PEER_HEADS, 1, n_i, tb), lambda i, e: (0, e, 0, i))
    return pl.pallas_call(
        functools.partial(_peer_kernel, n_i=n_i, tb=tb, alpha=alpha),
        grid=(t // tb, n_exp // eb),
        in_specs=[pl.BlockSpec((tb, d), tok), pl.BlockSpec((tb, d), tok),
                  pl.BlockSpec((eb, d), lambda i, e: (e, 0)), pl.BlockSpec((eb, d), lambda i, e: (e, 0)),
                  full, full, part, part,
                  pl.BlockSpec((1, d), lambda i, e: (0, 0)), pl.BlockSpec((1, d), lambda i, e: (0, 0))],
        out_specs=pl.BlockSpec((tb, d), tok),
        out_shape=jax.ShapeDtypeStruct((t, d), F32),
        scratch_shapes=[pltpu.VMEM((eb, tb), F32), pltpu.VMEM((eb, tb), MXU_DTYPE)],
        compiler_params=_params(2),
        name="peer_experts",
    )(xb, res, u, v, s2, b, th4, a4, gain.reshape(1, d), bias.reshape(1, d))


def _peer_layer(x, xb, w_query, keys, u, v, gain, bias, alpha):
    qp = _matmul(xb, w_query, F32)
    s2, b, th, a = _peer_route(qp, keys)
    return _peer_experts(xb, x, u, v, s2, b, th, a, gain, bias, alpha)


def _even_mixer(x2d, batch, seq, w_in, q_norm, w_q_b, kv_norm, w_kv_b, tables):
    d = x2d.shape[1]
    o1 = 3 * A_HEADS * HEAD_DIM
    o3 = o1 + MLA_Q_LORA + MLA_KV_LORA
    w_in_p = jnp.concatenate(
        [w_in, jnp.zeros((d, 4096 - w_in.shape[1]), w_in.dtype)], axis=1).astype(MXU_DTYPE)
    h = _matmul(x2d, w_in_p, F32)
    cos_a, sin_a, cos_b, sin_lo, sin_hi = tables
    qa, ka, cqn, ckvn, kr = _even_prep(h, tables, q_norm, kv_norm, seq)
    out_a = _dilated_attention(qa, ka, h, batch, seq)
    wq = w_q_b.reshape(MLA_Q_LORA, B_HEADS, MLA_NOPE + MLA_ROPE)
    wq = jnp.pad(wq, ((0, 0), (0, 0), (0, MLA_QK_PAD - MLA_NOPE - MLA_ROPE)))
    wq = wq.reshape(MLA_Q_LORA, B_HEADS * MLA_QK_PAD).astype(MXU_DTYPE)
    wkv = w_kv_b.reshape(MLA_KV_LORA, B_HEADS, MLA_NOPE + MLA_V)
    wkv = jnp.concatenate([wkv[:, :, :MLA_NOPE].reshape(MLA_KV_LORA, -1),
                           wkv[:, :, MLA_NOPE:].reshape(MLA_KV_LORA, -1)], axis=1).astype(MXU_DTYPE)
    qf = _matmul(cqn, wq, F32)
    kvf = _matmul(ckvn, wkv, MXU_DTYPE)
    qm, km = _mla_prep(qf, kvf, kr, (cos_b, sin_lo, sin_hi), seq)
    out_b = _mla_attention(qm, km, kvf, batch, seq)
    del o3
    return out_a, out_b


def kernel(x, a_w_in, b_q_norm, b_w_q_b, b_kv_norm, b_w_kv_b, ab_w_out, c_w_in, c_w_out,
           peer_w_query, peer_sub_keys, peer_u, peer_v, ln_gain, ln_bias):
    batch, seq, d = x.shape
    depth = peer_u.shape[0]
    alpha = (2 * depth) ** 0.25
    tables = _rope_tables(seq)
    xf = x.reshape(batch * seq, d)
    xb = None
    for layer in range(depth):
        i = layer // 2
        if layer % 2 == 0:
            src = xf if xb is None else xb
            out_a, out_b = _even_mixer(src, batch, seq, a_w_in[i], b_q_norm[i], b_w_q_b[i],
                                       b_kv_norm[i], b_w_kv_b[i], tables)
            w_out = ab_w_out[i].astype(MXU_DTYPE)
            half = A_HEADS * HEAD_DIM
            xf, xb = _matmul_residual_ln([out_a, out_b], [w_out[:half], w_out[half:]], xf,
                                         ln_gain[layer, 0], ln_bias[layer, 0], alpha)
        else:
            src = xf if xb is None else xb
            qkv = _matmul(src, c_w_in[i].astype(MXU_DTYPE), MXU_DTYPE)
            o = _stick_attention(qkv, batch, seq)
            xf, xb = _matmul_residual_ln([o], [c_w_out[i].astype(MXU_DTYPE)], xf,
                                         ln_gain[layer, 0], ln_bias[layer, 0], alpha)
        xf = _peer_layer(xf, xb, peer_w_query[layer].astype(MXU_DTYPE),
                         peer_sub_keys[layer].astype(MXU_DTYPE),
                         peer_u[layer].astype(MXU_DTYPE), peer_v[layer].astype(MXU_DTYPE),
                         ln_gain[layer, 1], ln_bias[layer, 1], alpha)
        xb = None
    return xf.reshape(batch, seq, d)
```

```python
import functools

import jax
import jax.numpy as jnp
from jax import lax
from jax.experimental import pallas as pl
from jax.experimental.pallas import tpu as pltpu

F32 = jnp.float32
MXU_DTYPE = jnp.bfloat16

HEAD_DIM = 128
BLOCK = 128
DIL_STEPS = 128
DILATIONS = (1, 4, 16)
DIL_CHUNK = BLOCK * DILATIONS[-1]
A_HEADS = 8
B_HEADS = 8
MLA_Q_LORA = 512
MLA_KV_LORA = 256
MLA_NOPE = 128
MLA_ROPE = 64
MLA_V = 128
MLA_QK_PAD = 256
C_HEADS = 16
PEER_HEADS = 8
PEER_N_KEYS = 128
PEER_TOPK = 16
PEER_HALF = 128
ROPE_THETA = 10000.0
LN_EPS = 1e-5
RMS_EPS = 1e-6
NEG_INF = -1e30

V7X_VMEM_LIMIT_BYTES = 56 * 1024 * 1024
NT_DIMS = (((1,), (1,)), ((), ()))
TN_DIMS = (((0,), (0,)), ((), ()))


def _params(n_axes):
    return pltpu.CompilerParams(dimension_semantics=("arbitrary",) * n_axes,
                                vmem_limit_bytes=V7X_VMEM_LIMIT_BYTES)


def _mm_kernel(a_ref, b_ref, o_ref):
    a = a_ref[...].astype(MXU_DTYPE)
    o_ref[...] = jnp.dot(a, b_ref[...], preferred_element_type=F32).astype(o_ref.dtype)


def _matmul(a, b, out_dtype, bm=512, bn=1024):
    m, k = a.shape
    n = b.shape[1]
    bm, bn = min(bm, m), min(bn, n)
    assert m % bm == 0 and n % bn == 0
    return pl.pallas_call(
        _mm_kernel,
        grid=(m // bm, n // bn),
        in_specs=[pl.BlockSpec((bm, k), lambda i, j: (i, 0)),
                  pl.BlockSpec((k, bn), lambda i, j: (0, j))],
        out_specs=pl.BlockSpec((bm, bn), lambda i, j: (i, j)),
        out_shape=jax.ShapeDtypeStruct((m, n), out_dtype),
        compiler_params=_params(2),
        name="matmul",
    )(a, b)


def _layer_norm_rows(y, g, b):
    mu = jnp.mean(y, axis=-1, keepdims=True)
    d = y - mu
    var = jnp.mean(d * d, axis=-1, keepdims=True)
    return d * lax.rsqrt(var + LN_EPS) * g + b


def _mm_ln_kernel(*refs, n_pairs, alpha):
    a_refs = refs[:n_pairs]
    w_refs = refs[n_pairs:2 * n_pairs]
    res_ref, g_ref, b_ref, o_ref, ob_ref = refs[2 * n_pairs:]
    acc = jnp.dot(a_refs[0][...].astype(MXU_DTYPE), w_refs[0][...], preferred_element_type=F32)
    for a_ref, w_ref in zip(a_refs[1:], w_refs[1:]):
        acc = acc + jnp.dot(a_ref[...].astype(MXU_DTYPE), w_ref[...], preferred_element_type=F32)
    y = _layer_norm_rows(alpha * res_ref[...] + acc, g_ref[...], b_ref[...])
    o_ref[...] = y
    ob_ref[...] = y.astype(ob_ref.dtype)


def _matmul_residual_ln(a_list, w_list, res, gain, bias, alpha, bm=256):
    m, n = res.shape
    n_pairs = len(a_list)
    in_specs = [pl.BlockSpec((bm, a.shape[1]), lambda i: (i, 0)) for a in a_list]
    in_specs += [pl.BlockSpec(w.shape, lambda i: (0, 0)) for w in w_list]
    in_specs += [pl.BlockSpec((bm, n), lambda i: (i, 0)),
                 pl.BlockSpec((1, n), lambda i: (0, 0)),
                 pl.BlockSpec((1, n), lambda i: (0, 0))]
    return pl.pallas_call(
        functools.partial(_mm_ln_kernel, n_pairs=n_pairs, alpha=alpha),
        grid=(m // bm,),
        in_specs=in_specs,
        out_specs=[pl.BlockSpec((bm, n), lambda i: (i, 0)), pl.BlockSpec((bm, n), lambda i: (i, 0))],
        out_shape=[jax.ShapeDtypeStruct((m, n), F32), jax.ShapeDtypeStruct((m, n), MXU_DTYPE)],
        compiler_params=_params(1),
        name="matmul_residual_ln",
    )(*a_list, *w_list, res, gain.reshape(1, n), bias.reshape(1, n))


def _rope_tables(seq):
    pos = jnp.arange(seq, dtype=F32)[:, None]

    def cos_sin(half):
        inv = ROPE_THETA ** (-jnp.arange(half, dtype=F32) / half)
        ang = pos * inv[None, :]
        return jnp.cos(ang), jnp.sin(ang)

    c, s = cos_sin(HEAD_DIM // 2)
    cos_a = jnp.concatenate([c, c], axis=1)
    sin_a = jnp.concatenate([-s, s], axis=1)
    c, s = cos_sin(MLA_ROPE // 2)
    z32 = jnp.zeros_like(c)
    cos_b = jnp.concatenate([c, c, z32, z32], axis=1)
    sin_lo = jnp.concatenate([-s, z32, z32, z32], axis=1)
    sin_hi = jnp.concatenate([z32, s, z32, z32], axis=1)
    return cos_a, sin_a, cos_b, sin_lo, sin_hi


def _rope128(x, cos_a, sin_a):
    return x * cos_a + pltpu.roll(x, HEAD_DIM // 2, 1) * sin_a


def _rope64(x, cos_b, sin_lo, sin_hi):
    return x * cos_b + pltpu.roll(x, 96, 1) * sin_lo + pltpu.roll(x, 32, 1) * sin_hi


def _rms_norm_rows(x, g):
    return x * lax.rsqrt(jnp.mean(x * x, axis=-1, keepdims=True) + RMS_EPS) * g


def _even_prep_kernel(qk_ref, tail_ref, cos_a_ref, sin_a_ref, cos_b_ref, sin_lo_ref, sin_hi_ref,
                      qn_ref, kvn_ref, qa_ref, ka_ref, cq_ref, ckv_ref, kr_ref):
    cos_a, sin_a = cos_a_ref[...], sin_a_ref[...]
    n_qk = A_HEADS * HEAD_DIM
    for hd in range(A_HEADS):
        lo = hd * HEAD_DIM
        qa_ref[:, lo:lo + HEAD_DIM] = _rope128(qk_ref[:, lo:lo + HEAD_DIM], cos_a, sin_a)
        ka_ref[:, lo:lo + HEAD_DIM] = _rope128(qk_ref[:, n_qk + lo:n_qk + lo + HEAD_DIM], cos_a, sin_a)
    cq_ref[...] = _rms_norm_rows(tail_ref[:, :MLA_Q_LORA], qn_ref[...]).astype(cq_ref.dtype)
    o2 = MLA_Q_LORA + MLA_KV_LORA
    ckv_ref[...] = _rms_norm_rows(tail_ref[:, MLA_Q_LORA:o2], kvn_ref[...]).astype(ckv_ref.dtype)
    kr = _rope64(tail_ref[:, o2:o2 + 128], cos_b_ref[...], sin_lo_ref[...], sin_hi_ref[...])
    kr_ref[...] = kr.astype(kr_ref.dtype)


def _even_prep(h, tables, q_norm, kv_norm, seq, rows=256):
    t = h.shape[0]
    n_qk = A_HEADS * HEAD_DIM
    sb = seq // rows
    tab_spec = pl.BlockSpec((rows, 128), lambda i: (i % sb, 0))
    return pl.pallas_call(
        _even_prep_kernel,
        grid=(t // rows,),
        in_specs=[pl.BlockSpec((rows, 2 * n_qk), lambda i: (i, 0)),
                  pl.BlockSpec((rows, 1024), lambda i: (i, 3)),
                  tab_spec, tab_spec, tab_spec, tab_spec, tab_spec,
                  pl.BlockSpec((1, MLA_Q_LORA), lambda i: (0, 0)),
                  pl.BlockSpec((1, MLA_KV_LORA), lambda i: (0, 0))],
        out_specs=[pl.BlockSpec((rows, n_qk), lambda i: (i, 0)),
                   pl.BlockSpec((rows, n_qk), lambda i: (i, 0)),
                   pl.BlockSpec((rows, MLA_Q_LORA), lambda i: (i, 0)),
                   pl.BlockSpec((rows, MLA_KV_LORA), lambda i: (i, 0)),
                   pl.BlockSpec((rows, 128), lambda i: (i, 0))],
        out_shape=[jax.ShapeDtypeStruct((t, n_qk), F32),
                   jax.ShapeDtypeStruct((t, n_qk), F32),
                   jax.ShapeDtypeStruct((t, MLA_Q_LORA), MXU_DTYPE),
                   jax.ShapeDtypeStruct((t, MLA_KV_LORA), MXU_DTYPE),
                   jax.ShapeDtypeStruct((t, 128), MXU_DTYPE)],
        compiler_params=_params(1),
        name="even_prep",
    )(h, h, *tables, q_norm.reshape(1, -1), kv_norm.reshape(1, -1))


def _mla_prep_kernel(q_ref, kn_ref, kr_ref, cos_b_ref, sin_lo_ref, sin_hi_ref, qm_ref, km_ref):
    cos_b, sin_lo, sin_hi = cos_b_ref[...], sin_lo_ref[...], sin_hi_ref[...]
    kr = kr_ref[...]
    for hd in range(B_HEADS):
        lo = hd * MLA_QK_PAD
        qm_ref[:, lo:lo + MLA_NOPE] = q_ref[:, lo:lo + MLA_NOPE].astype(qm_ref.dtype)
        q_rope = _rope64(q_ref[:, lo + MLA_NOPE:lo + MLA_QK_PAD], cos_b, sin_lo, sin_hi)
        qm_ref[:, lo + MLA_NOPE:lo + MLA_QK_PAD] = q_rope.astype(qm_ref.dtype)
        km_ref[:, lo:lo + MLA_NOPE] = kn_ref[:, hd * MLA_NOPE:(hd + 1) * MLA_NOPE]
        km_ref[:, lo + MLA_NOPE:lo + MLA_QK_PAD] = kr


def _mla_prep(qf, kvf, kr, tables, seq, rows=256):
    t = qf.shape[0]
    w = B_HEADS * MLA_QK_PAD
    sb = seq // rows
    tab_spec = pl.BlockSpec((rows, 128), lambda i: (i % sb, 0))
    return pl.pallas_call(
        _mla_prep_kernel,
        grid=(t // rows,),
        in_specs=[pl.BlockSpec((rows, w), lambda i: (i, 0)),
                  pl.BlockSpec((rows, B_HEADS * MLA_NOPE), lambda i: (i, 0)),
                  pl.BlockSpec((rows, 128), lambda i: (i, 0)),
                  tab_spec, tab_spec, tab_spec],
        out_specs=[pl.BlockSpec((rows, w), lambda i: (i, 0)), pl.BlockSpec((rows, w), lambda i: (i, 0))],
        out_shape=[jax.ShapeDtypeStruct((t, w), MXU_DTYPE), jax.ShapeDtypeStruct((t, w), MXU_DTYPE)],
        compiler_params=_params(1),
        name="mla_prep",
    )(qf, kvf, kr, *tables)


def _dilated_kernel(q_ref, kc_ref, kp_ref, vc_ref, vp_ref, o_ref,
                    kbuf, vbuf, o0, o1, o2, l0, l1, l2):
    c = pl.program_id(2)
    kbuf[:DIL_CHUNK, :] = kp_ref[...]
    kbuf[DIL_CHUNK:, :] = kc_ref[...]
    vbuf[:DIL_CHUNK, :] = vp_ref[...]
    vbuf[DIL_CHUNK:, :] = vc_ref[...]
    scale = HEAD_DIM ** -0.5
    row = lax.broadcasted_iota(jnp.int32, (BLOCK, 2 * BLOCK), 0)
    col = lax.broadcasted_iota(jnp.int32, (BLOCK, 2 * BLOCK), 1)
    band = (col >= row) & (col <= row + DIL_STEPS)

    def block(q_start, k_start, dil, first, o_g, l_g):
        if dil == 1:
            q_idx, k_idx = pl.ds(q_start, BLOCK), pl.ds(k_start, 2 * BLOCK)
        else:
            q_idx = pl.ds(q_start, BLOCK, stride=dil)
            k_idx = pl.ds(k_start, 2 * BLOCK, stride=dil)
        q = q_ref[q_idx, :].astype(MXU_DTYPE)
        k = kbuf[k_idx, :].astype(MXU_DTYPE)
        v = vbuf[k_idx, :].astype(MXU_DTYPE)
        logits = lax.dot_general(q, k, NT_DIMS, preferred_element_type=F32) * scale
        valid_from = jnp.where(first, BLOCK, 0)
        logits = jnp.where(band & (col >= valid_from), logits, NEG_INF)
        m = jnp.max(logits, axis=-1, keepdims=True)
        p = jnp.exp(logits - m)
        denom = jnp.sum(p, axis=-1, keepdims=True)
        out = jnp.dot((p / denom).astype(MXU_DTYPE), v, preferred_element_type=F32)
        o_g[q_idx, :] = out
        l_g[q_idx, :] = jnp.broadcast_to(m + jnp.log(denom), (BLOCK, HEAD_DIM))

    n_blocks = DIL_CHUNK // BLOCK

    def body(i, carry):
        block(pl.multiple_of(i * BLOCK, BLOCK), pl.multiple_of(DIL_CHUNK + (i - 1) * BLOCK, BLOCK),
              DILATIONS[0], (c == 0) & (i == 0), o0, l0)
        d1 = DILATIONS[1]
        n, r = i // d1, i % d1
        span = BLOCK * d1
        block(n * span + r, DIL_CHUNK + (n - 1) * span + r, d1, (c == 0) & (n == 0), o1, l1)
        block(i, DIL_CHUNK - BLOCK * DILATIONS[2] + i, DILATIONS[2], c == 0, o2, l2)
        return carry

    lax.fori_loop(0, n_blocks, body, 0)

    la, lb, lc = l0[...], l1[...], l2[...]
    mx = jnp.maximum(jnp.maximum(la, lb), lc)
    ea, eb, ec = jnp.exp(la - mx), jnp.exp(lb - mx), jnp.exp(lc - mx)
    merged = (ea * o0[...] + eb * o1[...] + ec * o2[...]) / (ea + eb + ec)
    o_ref[...] = merged.astype(o_ref.dtype)


def _dilated_attention(qa, ka, h, batch, seq):
    t = qa.shape[0]
    nc = seq // DIL_CHUNK
    v_col0 = 2 * A_HEADS
    cur = lambda b, hd, c: (b * nc + c, hd)
    prev = lambda b, hd, c: (b * nc + jnp.maximum(c - 1, 0), hd)
    blk = (DIL_CHUNK, HEAD_DIM)
    return pl.pallas_call(
        _dilated_kernel,
        grid=(batch, A_HEADS, nc),
        in_specs=[pl.BlockSpec(blk, cur),
                  pl.BlockSpec(blk, cur), pl.BlockSpec(blk, prev),
                  pl.BlockSpec(blk, lambda b, hd, c: (b * nc + c, v_col0 + hd)),
                  pl.BlockSpec(blk, lambda b, hd, c: (b * nc + jnp.maximum(c - 1, 0), v_col0 + hd))],
        out_specs=pl.BlockSpec(blk, cur),
        out_shape=jax.ShapeDtypeStruct((t, A_HEADS * HEAD_DIM), MXU_DTYPE),
        scratch_shapes=[pltpu.VMEM((2 * DIL_CHUNK, HEAD_DIM), F32)] * 2
                       + [pltpu.VMEM(blk, F32)] * 6,
        compiler_params=_params(3),
        name="dilated_attention",
    )(qa, ka, ka, h, h)


def _mla_kernel(q_ref, k_ref, v_ref, o_ref, *, scale, bq):
    qi = pl.program_id(2)
    q = q_ref[...]
    row = lax.broadcasted_iota(jnp.int32, (bq, bq), 0)
    col = lax.broadcasted_iota(jnp.int32, (bq, bq), 1)

    def step(k_start, carry, diagonal):
        m, l, acc = carry
        k = k_ref[pl.ds(k_start, bq), :]
        v = v_ref[pl.ds(k_start, bq), :]
        s = lax.dot_general(q, k, NT_DIMS, preferred_element_type=F32) * scale
        if diagonal:
            s = jnp.where(col <= row, s, NEG_INF)
        m_new = jnp.maximum(m, jnp.max(s, axis=-1, keepdims=True))
        alpha = jnp.exp(m - m_new)
        p = jnp.exp(s - m_new)
        l = alpha * l + jnp.sum(p, axis=-1, keepdims=True)
        acc = alpha * acc + jnp.dot(p.astype(MXU_DTYPE), v, preferred_element_type=F32)
        return m_new, l, acc

    carry = (jnp.full((bq, 1), NEG_INF, F32), jnp.zeros((bq, 1), F32), jnp.zeros((bq, MLA_V), F32))
    carry = lax.fori_loop(0, qi, lambda j, cr: step(pl.multiple_of(j * bq, bq), cr, False), carry)
    _, l, acc = step(pl.multiple_of(qi * bq, bq), carry, True)
    o_ref[...] = (acc / l).astype(o_ref.dtype)


def _mla_attention(qm, km, kvf, batch, seq, bq=512):
    t = qm.shape[0]
    bq = min(bq, seq)
    nq = seq // bq
    scale = (MLA_NOPE + MLA_ROPE) ** -0.5
    return pl.pallas_call(
        functools.partial(_mla_kernel, scale=scale, bq=bq),
        grid=(batch, B_HEADS, nq),
        in_specs=[pl.BlockSpec((bq, MLA_QK_PAD), lambda b, hd, i: (b * nq + i, hd)),
                  pl.BlockSpec((seq, MLA_QK_PAD), lambda b, hd, i: (b, hd)),
                  pl.BlockSpec((seq, MLA_V), lambda b, hd, i: (b, B_HEADS + hd))],
        out_specs=pl.BlockSpec((bq, MLA_V), lambda b, hd, i: (b * nq + i, hd)),
        out_shape=jax.ShapeDtypeStruct((t, B_HEADS * MLA_V), MXU_DTYPE),
        compiler_params=_params(3),
        name="mla_attention",
    )(qm, km, kvf)


def _stick_kernel(q_ref, k_ref, v_ref, o_ref, acc_ref, run_ref, *, scale, bq, bk):
    qi = pl.program_id(2)
    n_sub = bq // bk
    tri_r = lax.broadcasted_iota(jnp.int32, (bk, bk), 0)
    tri_c = lax.broadcasted_iota(jnp.int32, (bk, bk), 1)
    later = jnp.where(tri_r > tri_c, 1.0, 0.0).astype(MXU_DTYPE)
    acc_ref[...] = jnp.zeros_like(acc_ref)
    run_ref[...] = jnp.zeros_like(run_ref)

    def tile(k_start, r0, diagonal):
        rows = bq - r0
        q = q_ref[r0:, :]
        k = k_ref[pl.ds(k_start, bk), :]
        v = v_ref[pl.ds(k_start, bk), :]
        z = lax.dot_general(q, k, NT_DIMS, preferred_element_type=F32) * scale
        neg_abs = lax.bitcast_convert_type(
            lax.bitcast_convert_type(z, jnp.uint32) | jnp.uint32(0x80000000), F32)
        softplus = jnp.maximum(z, 0.0) + jnp.log(1.0 + jnp.exp(neg_abs))
        log_beta = z - softplus
        if diagonal:
            row = lax.broadcasted_iota(jnp.int32, (rows, bk), 0)
            col = lax.broadcasted_iota(jnp.int32, (rows, bk), 1)
            softplus = jnp.where(col < row, softplus, 0.0)
        hi = softplus.astype(MXU_DTYPE)
        lo = (softplus - hi.astype(F32)).astype(MXU_DTYPE)
        suffix = (jnp.dot(hi, later, preferred_element_type=F32)
                  + jnp.dot(lo, later, preferred_element_type=F32))
        att = jnp.exp(log_beta - suffix - run_ref[r0:, :])
        if diagonal:
            att = jnp.where(col < row, att, 0.0)
        acc_ref[r0:, :] += jnp.dot(att.astype(MXU_DTYPE), v, preferred_element_type=F32)
        run_ref[r0:, :] += jnp.sum(softplus, axis=-1, keepdims=True)

    for jj in reversed(range(n_sub)):
        tile(pl.multiple_of(qi * bq + jj * bk, bk), jj * bk, True)

    def body(j, carry):
        tile(pl.multiple_of((qi * n_sub - 1 - j) * bk, bk), 0, False)
        return carry

    lax.fori_loop(0, qi * n_sub, body, 0)
    o_ref[...] = acc_ref[...].astype(o_ref.dtype)


def _stick_attention(qkv, batch, seq, bq=1024, bk=256):
    t = qkv.shape[0]
    bq, bk = min(bq, seq), min(bk, seq)
    nq = seq // bq
    return pl.pallas_call(
        functools.partial(_stick_kernel, scale=HEAD_DIM ** -0.5, bq=bq, bk=bk),
        grid=(batch, C_HEADS, nq),
        in_specs=[pl.BlockSpec((bq, HEAD_DIM), lambda b, hd, i: (b * nq + i, hd)),
                  pl.BlockSpec((seq, HEAD_DIM), lambda b, hd, i: (b, C_HEADS + hd)),
                  pl.BlockSpec((seq, HEAD_DIM), lambda b, hd, i: (b, 2 * C_HEADS + hd))],
        out_specs=pl.BlockSpec((bq, HEAD_DIM), lambda b, hd, i: (b * nq + i, hd)),
        out_shape=jax.ShapeDtypeStruct((t, C_HEADS * HEAD_DIM), MXU_DTYPE),
        scratch_shapes=[pltpu.VMEM((bq, HEAD_DIM), F32), pltpu.VMEM((bq, 1), F32)],
        compiler_params=_params(3),
        name="stick_breaking_attention",
    )(qkv, qkv, qkv)


def _top_values(s, count):
    vals = []
    cur = s
    for _ in range(count):
        m = jnp.max(cur, axis=0, keepdims=True)
        vals.append(m)
        cur = jnp.where(cur == m, NEG_INF, cur)
    return vals


def _peer_route_kernel(q_ref, keys_ref, s2_ref, b_ref, th_ref, a_ref):
    def head(hd, carry):
        col = pl.multiple_of(hd * 2 * PEER_HALF, 2 * PEER_HALF)
        q1 = q_ref[:, pl.ds(col, PEER_HALF)].astype(MXU_DTYPE)
        q2 = q_ref[:, pl.ds(col + PEER_HALF, PEER_HALF)].astype(MXU_DTYPE)
        s1 = lax.dot_general(keys_ref[hd, 0], q1, NT_DIMS, preferred_element_type=F32)
        s2 = lax.dot_general(keys_ref[hd, 1], q2, NT_DIMS, preferred_element_type=F32)
        v1 = _top_values(s1, PEER_TOPK)
        v2 = _top_values(s2, PEER_TOPK)
        v2_all = jnp.concatenate(v2, axis=0)
        cand = jnp.concatenate([v + v2_all for v in v1], axis=0)
        tops = _top_values(cand, PEER_TOPK + 1)
        cut = 0.5 * (tops[PEER_TOPK - 1] + tops[PEER_TOPK])
        top = tops[0]
        z = jnp.sum(jnp.where(cand >= cut, jnp.exp(cand - top), 0.0), axis=0, keepdims=True)
        a = jnp.where(s1 >= v1[PEER_TOPK - 1], jnp.exp(s1 - v1[0]) / z, 0.0)
        b = jnp.where(s2 >= v2[PEER_TOPK - 1], jnp.exp(s2 - v2[0]), 0.0)
        s2_ref[hd] = s2
        b_ref[hd] = b
        th_ref[hd] = cut - s1
        a_ref[hd] = a
        return carry

    lax.fori_loop(0, PEER_HEADS, head, 0)


def _peer_route(qp, keys, tr=256):
    t = qp.shape[0]
    shape = jax.ShapeDtypeStruct((PEER_HEADS, PEER_N_KEYS, t), F32)
    spec = pl.BlockSpec((PEER_HEADS, PEER_N_KEYS, tr), lambda i: (0, 0, i))
    return pl.pallas_call(
        _peer_route_kernel,
        grid=(t // tr,),
        in_specs=[pl.BlockSpec((tr, qp.shape[1]), lambda i: (i, 0)),
                  pl.BlockSpec(keys.shape, lambda i: (0, 0, 0, 0))],
        out_specs=[spec] * 4,
        out_shape=[shape] * 4,
        compiler_params=_params(1),
        name="peer_route",
    )(qp, keys)


def _gelu_exact(x):
    return 0.5 * x * (1.0 + lax.erf(x * (2.0 ** -0.5)))


def _peer_kernel(xt_ref, res_ref, u_ref, v_ref, s2_ref, b_ref, th_ref, a_ref, g_ref, bias_ref,
                 o_ref, st_a, st_b, wt_a, wt_b, *, n_i, tb, alpha):
    i = pl.program_id(0)
    s = pl.program_id(1)
    half = PEER_N_KEYS // 2

    @pl.when((i == 0) & (s == 0))
    def _():
        st_b[...] = jnp.zeros_like(st_b)
        wt_b[...] = jnp.zeros_like(wt_b)

    @pl.when(s == 0)
    def _():
        o_ref[...] = jnp.zeros_like(o_ref)

    def phase(st_write, st_read, wt_ref, wt_read):
        def gate_unit(l, jh):
            lanes = slice(l * 128, (l + 1) * 128)
            js = slice(jh * half, (jh + 1) * half)
            gates = [jnp.zeros((half, 128), F32) for _ in range(n_i)]
            for hd in range(PEER_HEADS):
                s2 = s2_ref[hd, js, lanes]
                b = b_ref[hd, js, lanes]
                for ii in range(n_i):
                    th = th_ref[hd, 0, ii:ii + 1, lanes]
                    a = a_ref[hd, 0, ii:ii + 1, lanes]
                    gates[ii] = gates[ii] + jnp.where(s2 >= th, b, 0.0) * a
            for ii in range(n_i):
                rows = slice(ii * PEER_N_KEYS + jh * half, ii * PEER_N_KEYS + (jh + 1) * half)
                wt_ref[rows, lanes] = (_gelu_exact(st_read[rows, lanes]) * gates[ii]).astype(wt_ref.dtype)

        def pre_piece(p, n):
            rows = slice(p * (n_i * PEER_N_KEYS // n), (p + 1) * (n_i * PEER_N_KEYS // n))
            st_write[rows, :] = jnp.dot(u_ref[rows, :], xt_ref[...], preferred_element_type=F32)

        def value_piece(p, n):
            tok = slice(p * (tb // n), (p + 1) * (tb // n))
            o_ref[tok, :] += lax.dot_general(wt_read[:, tok], v_ref[...], TN_DIMS,
                                             preferred_element_type=F32)

        units = [(l, jh) for l in range(tb // 128) for jh in range(PEER_N_KEYS // half)]
        n_pieces = len(units) // 2
        for k, (l, jh) in enumerate(units):
            if k % 2 == 0:
                value_piece(k // 2, n_pieces)
            else:
                pre_piece(k // 2, n_pieces)
            gate_unit(l, jh)

    @pl.when((s & 1) == 0)
    def _():
        phase(st_a, st_b, wt_a, wt_b)

    @pl.when((s & 1) == 1)
    def _():
        phase(st_b, st_a, wt_b, wt_a)

    @pl.when(s <= 1)
    def _():
        o_ref[...] = jnp.zeros_like(o_ref)

    @pl.when(s == pl.num_programs(1) - 1)
    def _():
        o_ref[...] = _layer_norm_rows(alpha * res_ref[...] + o_ref[...], g_ref[...], bias_ref[...])


def _peer_experts(xt, res, u, v, s2, b, th, a, gain, bias, alpha, tb=512, n_i=4):
    t, d = res.shape
    n_exp = u.shape[0]
    eb = n_i * PEER_N_KEYS
    n_e = n_exp // eb
    tb = min(tb, t)
    th4 = th.reshape(PEER_HEADS, PEER_N_KEYS // n_i, n_i, t)
    a4 = a.reshape(PEER_HEADS, PEER_N_KEYS // n_i, n_i, t)
    tok = lambda i, s: (i, 0)
    full = pl.BlockSpec((PEER_HEADS, PEER_N_KEYS, tb), lambda i, s: (0, 0, i))
    blk = lambda s, lag: jnp.clip(s - lag, 0, n_e - 1)
    part = pl.BlockSpec((PEER_HEADS, 1, n_i, tb), lambda i, s: (0, blk(s, 1), 0, i))
    return pl.pallas_call(
        functools.partial(_peer_kernel, n_i=n_i, tb=tb, alpha=alpha),
        grid=(t // tb, n_e + 2),
        in_specs=[pl.BlockSpec((d, tb), lambda i, s: (0, i)), pl.BlockSpec((tb, d), tok),
                  pl.BlockSpec((eb, d), lambda i, s: (blk(s, 0), 0)),
                  pl.BlockSpec((eb, d), lambda i, s: (blk(s, 2), 0)),
                  full, full, part, part,
                  pl.BlockSpec((1, d), lambda i, s: (0, 0)), pl.BlockSpec((1, d), lambda i, s: (0, 0))],
        out_specs=pl.BlockSpec((tb, d), tok),
        out_shape=jax.ShapeDtypeStruct((t, d), F32),
        scratch_shapes=[pltpu.VMEM((eb, tb), F32), pltpu.VMEM((eb, tb), F32),
                        pltpu.VMEM((eb, tb), MXU_DTYPE), pltpu.VMEM((eb, tb), MXU_DTYPE)],
        compiler_params=_params(2),
        name="peer_experts",
    )(xt, res, u, v, s2, b, th4, a4, gain.reshape(1, d), bias.reshape(1, d))


def _peer_layer(x, xb, w_query, keys, u, v, gain, bias, alpha):
    qp = _matmul(xb, w_query, F32)
    s2, b, th, a = _peer_route(qp, keys)
    return _peer_experts(xb.T, x, u, v, s2, b, th, a, gain, bias, alpha)


def _even_mixer(x2d, batch, seq, w_in, q_norm, w_q_b, kv_norm, w_kv_b, tables):
    d = x2d.shape[1]
    o1 = 3 * A_HEADS * HEAD_DIM
    o3 = o1 + MLA_Q_LORA + MLA_KV_LORA
    w_in_p = jnp.concatenate(
        [w_in, jnp.zeros((d, 4096 - w_in.shape[1]), w_in.dtype)], axis=1).astype(MXU_DTYPE)
    h = _matmul(x2d, w_in_p, F32)
    cos_a, sin_a, cos_b, sin_lo, sin_hi = tables
    qa, ka, cqn, ckvn, kr = _even_prep(h, tables, q_norm, kv_norm, seq)
    out_a = _dilated_attention(qa, ka, h, batch, seq)
    wq = w_q_b.reshape(MLA_Q_LORA, B_HEADS, MLA_NOPE + MLA_ROPE)
    wq = jnp.pad(wq, ((0, 0), (0, 0), (0, MLA_QK_PAD - MLA_NOPE - MLA_ROPE)))
    wq = wq.reshape(MLA_Q_LORA, B_HEADS * MLA_QK_PAD).astype(MXU_DTYPE)
    wkv = w_kv_b.reshape(MLA_KV_LORA, B_HEADS, MLA_NOPE + MLA_V)
    wkv = jnp.concatenate([wkv[:, :, :MLA_NOPE].reshape(MLA_KV_LORA, -1),
                           wkv[:, :, MLA_NOPE:].reshape(MLA_KV_LORA, -1)], axis=1).astype(MXU_DTYPE)
    qf = _matmul(cqn, wq, F32)
    kvf = _matmul(ckvn, wkv, MXU_DTYPE)
    qm, km = _mla_prep(qf, kvf, kr, (cos_b, sin_lo, sin_hi), seq)
    out_b = _mla_attention(qm, km, kvf, batch, seq)
    del o3
    return out_a, out_b


def kernel(x, a_w_in, b_q_norm, b_w_q_b, b_kv_norm, b_w_kv_b, ab_w_out, c_w_in, c_w_out,
           peer_w_query, peer_sub_keys, peer_u, peer_v, ln_gain, ln_bias):
    batch, seq, d = x.shape
    depth = peer_u.shape[0]
    alpha = (2 * depth) ** 0.25
    tables = _rope_tables(seq)
    xf = x.reshape(batch * seq, d)
    xb = None
    for layer in range(depth):
        i = layer // 2
        if layer % 2 == 0:
            src = xf if xb is None else xb
            out_a, out_b = _even_mixer(src, batch, seq, a_w_in[i], b_q_norm[i], b_w_q_b[i],
                                       b_kv_norm[i], b_w_kv_b[i], tables)
            w_out = ab_w_out[i].astype(MXU_DTYPE)
            half = A_HEADS * HEAD_DIM
            xf, xb = _matmul_residual_ln([out_a, out_b], [w_out[:half], w_out[half:]], xf,
                                         ln_gain[layer, 0], ln_bias[layer, 0], alpha)
        else:
            src = xf if xb is None else xb
            qkv = _matmul(src, c_w_in[i].astype(MXU_DTYPE), MXU_DTYPE)
            o = _stick_attention(qkv, batch, seq)
            xf, xb = _matmul_residual_ln([o], [c_w_out[i].astype(MXU_DTYPE)], xf,
                                         ln_gain[layer, 0], ln_bias[layer, 0], alpha)
        xf = _peer_layer(xf, xb, peer_w_query[layer].astype(MXU_DTYPE),
                         peer_sub_keys[layer].astype(MXU_DTYPE),
                         peer_u[layer].astype(MXU_DTYPE), peer_v[layer].astype(MXU_DTYPE),
                         ln_gain[layer, 1], ln_bias[layer, 1], alpha)
        xb = None
    return xf.reshape(batch, seq, d)
```

```python
import functools

import jax
import jax.numpy as jnp
from jax import lax
from jax.experimental import pallas as pl
from jax.experimental.pallas import tpu as pltpu

F32 = jnp.float32
MXU_DTYPE = jnp.bfloat16

HEAD_DIM = 128
BLOCK = 128
DIL_STEPS = 128
DILATIONS = (1, 4, 16)
DIL_CHUNK = BLOCK * DILATIONS[-1]
A_HEADS = 8
B_HEADS = 8
MLA_Q_LORA = 512
MLA_KV_LORA = 256
MLA_NOPE = 128
MLA_ROPE = 64
MLA_V = 128
MLA_QK_PAD = 256
C_HEADS = 16
PEER_HEADS = 8
PEER_N_KEYS = 128
PEER_TOPK = 16
PEER_HALF = 128
ROPE_THETA = 10000.0
LN_EPS = 1e-5
RMS_EPS = 1e-6
NEG_INF = -1e30

V7X_VMEM_LIMIT_BYTES = 56 * 1024 * 1024
NT_DIMS = (((1,), (1,)), ((), ()))
TN_DIMS = (((0,), (0,)), ((), ()))


def _params(n_axes):
    return pltpu.CompilerParams(dimension_semantics=("arbitrary",) * n_axes,
                                vmem_limit_bytes=V7X_VMEM_LIMIT_BYTES)


def _mm_kernel(a_ref, b_ref, o_ref):
    a = a_ref[...].astype(MXU_DTYPE)
    o_ref[...] = jnp.dot(a, b_ref[...], preferred_element_type=F32).astype(o_ref.dtype)


def _matmul(a, b, out_dtype, bm=1024, bn=1024):
    m, k = a.shape
    n = b.shape[1]
    bm, bn = min(bm, m), min(bn, n)
    assert m % bm == 0 and n % bn == 0
    return pl.pallas_call(
        _mm_kernel,
        grid=(m // bm, n // bn),
        in_specs=[pl.BlockSpec((bm, k), lambda i, j: (i, 0)),
                  pl.BlockSpec((k, bn), lambda i, j: (0, j))],
        out_specs=pl.BlockSpec((bm, bn), lambda i, j: (i, j)),
        out_shape=jax.ShapeDtypeStruct((m, n), out_dtype),
        compiler_params=_params(2),
        name="matmul",
    )(a, b)


def _layer_norm_rows(y, g, b):
    mu = jnp.mean(y, axis=-1, keepdims=True)
    d = y - mu
    var = jnp.mean(d * d, axis=-1, keepdims=True)
    return d * lax.rsqrt(var + LN_EPS) * g + b


def _mm_ln_kernel(*refs, n_pairs, alpha):
    a_refs = refs[:n_pairs]
    w_refs = refs[n_pairs:2 * n_pairs]
    res_ref, g_ref, b_ref, o_ref, ob_ref = refs[2 * n_pairs:]
    acc = jnp.dot(a_refs[0][...].astype(MXU_DTYPE), w_refs[0][...], preferred_element_type=F32)
    for a_ref, w_ref in zip(a_refs[1:], w_refs[1:]):
        acc = acc + jnp.dot(a_ref[...].astype(MXU_DTYPE), w_ref[...], preferred_element_type=F32)
    y = _layer_norm_rows(alpha * res_ref[...] + acc, g_ref[...], b_ref[...])
    o_ref[...] = y
    ob_ref[...] = y.astype(ob_ref.dtype)


def _matmul_residual_ln(a_list, w_list, res, gain, bias, alpha, bm=256):
    m, n = res.shape
    n_pairs = len(a_list)
    in_specs = [pl.BlockSpec((bm, a.shape[1]), lambda i: (i, 0)) for a in a_list]
    in_specs += [pl.BlockSpec(w.shape, lambda i: (0, 0)) for w in w_list]
    in_specs += [pl.BlockSpec((bm, n), lambda i: (i, 0)),
                 pl.BlockSpec((1, n), lambda i: (0, 0)),
                 pl.BlockSpec((1, n), lambda i: (0, 0))]
    return pl.pallas_call(
        functools.partial(_mm_ln_kernel, n_pairs=n_pairs, alpha=alpha),
        grid=(m // bm,),
        in_specs=in_specs,
        out_specs=[pl.BlockSpec((bm, n), lambda i: (i, 0)), pl.BlockSpec((bm, n), lambda i: (i, 0))],
        out_shape=[jax.ShapeDtypeStruct((m, n), F32), jax.ShapeDtypeStruct((m, n), MXU_DTYPE)],
        compiler_params=_params(1),
        name="matmul_residual_ln",
    )(*a_list, *w_list, res, gain.reshape(1, n), bias.reshape(1, n))


def _rope_tables(seq):
    pos = jnp.arange(seq, dtype=F32)[:, None]

    def cos_sin(half):
        inv = ROPE_THETA ** (-jnp.arange(half, dtype=F32) / half)
        ang = pos * inv[None, :]
        return jnp.cos(ang), jnp.sin(ang)

    c, s = cos_sin(HEAD_DIM // 2)
    cos_a = jnp.concatenate([c, c], axis=1)
    sin_a = jnp.concatenate([-s, s], axis=1)
    c, s = cos_sin(MLA_ROPE // 2)
    z32 = jnp.zeros_like(c)
    cos_b = jnp.concatenate([c, c, z32, z32], axis=1)
    sin_lo = jnp.concatenate([-s, z32, z32, z32], axis=1)
    sin_hi = jnp.concatenate([z32, s, z32, z32], axis=1)
    return cos_a, sin_a, cos_b, sin_lo, sin_hi


def _rope128(x, cos_a, sin_a):
    return x * cos_a + pltpu.roll(x, HEAD_DIM // 2, 1) * sin_a


def _rope64(x, cos_b, sin_lo, sin_hi):
    return x * cos_b + pltpu.roll(x, 96, 1) * sin_lo + pltpu.roll(x, 32, 1) * sin_hi


def _rms_norm_rows(x, g):
    return x * lax.rsqrt(jnp.mean(x * x, axis=-1, keepdims=True) + RMS_EPS) * g


def _even_prep_kernel(qk_ref, tail_ref, cos_a_ref, sin_a_ref, cos_b_ref, sin_lo_ref, sin_hi_ref,
                      qn_ref, kvn_ref, qa_ref, ka_ref, cq_ref, ckv_ref, kr_ref):
    cos_a, sin_a = cos_a_ref[...], sin_a_ref[...]
    n_qk = A_HEADS * HEAD_DIM
    for hd in range(A_HEADS):
        lo = hd * HEAD_DIM
        qa_ref[:, lo:lo + HEAD_DIM] = _rope128(qk_ref[:, lo:lo + HEAD_DIM], cos_a, sin_a)
        ka_ref[:, lo:lo + HEAD_DIM] = _rope128(qk_ref[:, n_qk + lo:n_qk + lo + HEAD_DIM], cos_a, sin_a)
    cq_ref[...] = _rms_norm_rows(tail_ref[:, :MLA_Q_LORA], qn_ref[...]).astype(cq_ref.dtype)
    o2 = MLA_Q_LORA + MLA_KV_LORA
    ckv_ref[...] = _rms_norm_rows(tail_ref[:, MLA_Q_LORA:o2], kvn_ref[...]).astype(ckv_ref.dtype)
    kr = _rope64(tail_ref[:, o2:o2 + 128], cos_b_ref[...], sin_lo_ref[...], sin_hi_ref[...])
    kr_ref[...] = kr.astype(kr_ref.dtype)


def _even_prep(h, tables, q_norm, kv_norm, seq, rows=256):
    t = h.shape[0]
    n_qk = A_HEADS * HEAD_DIM
    sb = seq // rows
    tab_spec = pl.BlockSpec((rows, 128), lambda i: (i % sb, 0))
    return pl.pallas_call(
        _even_prep_kernel,
        grid=(t // rows,),
        in_specs=[pl.BlockSpec((rows, 2 * n_qk), lambda i: (i, 0)),
                  pl.BlockSpec((rows, 1024), lambda i: (i, 3)),
                  tab_spec, tab_spec, tab_spec, tab_spec, tab_spec,
                  pl.BlockSpec((1, MLA_Q_LORA), lambda i: (0, 0)),
                  pl.BlockSpec((1, MLA_KV_LORA), lambda i: (0, 0))],
        out_specs=[pl.BlockSpec((rows, n_qk), lambda i: (i, 0)),
                   pl.BlockSpec((rows, n_qk), lambda i: (i, 0)),
                   pl.BlockSpec((rows, MLA_Q_LORA), lambda i: (i, 0)),
                   pl.BlockSpec((rows, MLA_KV_LORA), lambda i: (i, 0)),
                   pl.BlockSpec((rows, 128), lambda i: (i, 0))],
        out_shape=[jax.ShapeDtypeStruct((t, n_qk), F32),
                   jax.ShapeDtypeStruct((t, n_qk), F32),
                   jax.ShapeDtypeStruct((t, MLA_Q_LORA), MXU_DTYPE),
                   jax.ShapeDtypeStruct((t, MLA_KV_LORA), MXU_DTYPE),
                   jax.ShapeDtypeStruct((t, 128), MXU_DTYPE)],
        compiler_params=_params(1),
        name="even_prep",
    )(h, h, *tables, q_norm.reshape(1, -1), kv_norm.reshape(1, -1))


def _mla_prep_kernel(q_ref, kn_ref, kr_ref, cos_b_ref, sin_lo_ref, sin_hi_ref, qm_ref, km_ref):
    cos_b, sin_lo, sin_hi = cos_b_ref[...], sin_lo_ref[...], sin_hi_ref[...]
    kr = kr_ref[...]
    for hd in range(B_HEADS):
        lo = hd * MLA_QK_PAD
        qm_ref[:, lo:lo + MLA_NOPE] = q_ref[:, lo:lo + MLA_NOPE].astype(qm_ref.dtype)
        q_rope = _rope64(q_ref[:, lo + MLA_NOPE:lo + MLA_QK_PAD], cos_b, sin_lo, sin_hi)
        qm_ref[:, lo + MLA_NOPE:lo + MLA_QK_PAD] = q_rope.astype(qm_ref.dtype)
        km_ref[:, lo:lo + MLA_NOPE] = kn_ref[:, hd * MLA_NOPE:(hd + 1) * MLA_NOPE]
        km_ref[:, lo + MLA_NOPE:lo + MLA_QK_PAD] = kr


def _mla_prep(qf, kvf, kr, tables, seq, rows=256):
    t = qf.shape[0]
    w = B_HEADS * MLA_QK_PAD
    sb = seq // rows
    tab_spec = pl.BlockSpec((rows, 128), lambda i: (i % sb, 0))
    return pl.pallas_call(
        _mla_prep_kernel,
        grid=(t // rows,),
        in_specs=[pl.BlockSpec((rows, w), lambda i: (i, 0)),
                  pl.BlockSpec((rows, B_HEADS * MLA_NOPE), lambda i: (i, 0)),
                  pl.BlockSpec((rows, 128), lambda i: (i, 0)),
                  tab_spec, tab_spec, tab_spec],
        out_specs=[pl.BlockSpec((rows, w), lambda i: (i, 0)), pl.BlockSpec((rows, w), lambda i: (i, 0))],
        out_shape=[jax.ShapeDtypeStruct((t, w), MXU_DTYPE), jax.ShapeDtypeStruct((t, w), MXU_DTYPE)],
        compiler_params=_params(1),
        name="mla_prep",
    )(qf, kvf, kr, *tables)


def _dilated_kernel(q_ref, kc_ref, kp_ref, vc_ref, vp_ref, o_ref,
                    kbuf, vbuf, o0, o1, o2, l0, l1, l2):
    c = pl.program_id(2)
    kbuf[:DIL_CHUNK, :] = kp_ref[...]
    kbuf[DIL_CHUNK:, :] = kc_ref[...]
    vbuf[:DIL_CHUNK, :] = vp_ref[...]
    vbuf[DIL_CHUNK:, :] = vc_ref[...]
    scale = HEAD_DIM ** -0.5
    row = lax.broadcasted_iota(jnp.int32, (BLOCK, 2 * BLOCK), 0)
    col = lax.broadcasted_iota(jnp.int32, (BLOCK, 2 * BLOCK), 1)
    band = (col >= row) & (col <= row + DIL_STEPS)

    def block(q_start, k_start, dil, first, o_g, l_g):
        if dil == 1:
            q_idx, k_idx = pl.ds(q_start, BLOCK), pl.ds(k_start, 2 * BLOCK)
        else:
            q_idx = pl.ds(q_start, BLOCK, stride=dil)
            k_idx = pl.ds(k_start, 2 * BLOCK, stride=dil)
        q = q_ref[q_idx, :].astype(MXU_DTYPE)
        k = kbuf[k_idx, :].astype(MXU_DTYPE)
        v = vbuf[k_idx, :].astype(MXU_DTYPE)
        logits = lax.dot_general(q, k, NT_DIMS, preferred_element_type=F32) * scale
        valid_from = jnp.where(first, BLOCK, 0)
        logits = jnp.where(band & (col >= valid_from), logits, NEG_INF)
        m = jnp.max(logits, axis=-1, keepdims=True)
        p = jnp.exp(logits - m)
        denom = jnp.sum(p, axis=-1, keepdims=True)
        out = jnp.dot((p / denom).astype(MXU_DTYPE), v, preferred_element_type=F32)
        o_g[q_idx, :] = out
        l_g[q_idx, :] = jnp.broadcast_to(m + jnp.log(denom), (BLOCK, HEAD_DIM))

    n_blocks = DIL_CHUNK // BLOCK

    def body(i, carry):
        block(pl.multiple_of(i * BLOCK, BLOCK), pl.multiple_of(DIL_CHUNK + (i - 1) * BLOCK, BLOCK),
              DILATIONS[0], (c == 0) & (i == 0), o0, l0)
        d1 = DILATIONS[1]
        n, r = i // d1, i % d1
        span = BLOCK * d1
        block(n * span + r, DIL_CHUNK + (n - 1) * span + r, d1, (c == 0) & (n == 0), o1, l1)
        block(i, DIL_CHUNK - BLOCK * DILATIONS[2] + i, DILATIONS[2], c == 0, o2, l2)
        return carry

    lax.fori_loop(0, n_blocks, body, 0)

    la, lb, lc = l0[...], l1[...], l2[...]
    mx = jnp.maximum(jnp.maximum(la, lb), lc)
    ea, eb, ec = jnp.exp(la - mx), jnp.exp(lb - mx), jnp.exp(lc - mx)
    merged = (ea * o0[...] + eb * o1[...] + ec * o2[...]) / (ea + eb + ec)
    o_ref[...] = merged.astype(o_ref.dtype)


def _dilated_attention(qa, ka, h, batch, seq):
    t = qa.shape[0]
    nc = seq // DIL_CHUNK
    v_col0 = 2 * A_HEADS
    cur = lambda b, hd, c: (b * nc + c, hd)
    prev = lambda b, hd, c: (b * nc + jnp.maximum(c - 1, 0), hd)
    blk = (DIL_CHUNK, HEAD_DIM)
    return pl.pallas_call(
        _dilated_kernel,
        grid=(batch, A_HEADS, nc),
        in_specs=[pl.BlockSpec(blk, cur),
                  pl.BlockSpec(blk, cur), pl.BlockSpec(blk, prev),
                  pl.BlockSpec(blk, lambda b, hd, c: (b * nc + c, v_col0 + hd)),
                  pl.BlockSpec(blk, lambda b, hd, c: (b * nc + jnp.maximum(c - 1, 0), v_col0 + hd))],
        out_specs=pl.BlockSpec(blk, cur),
        out_shape=jax.ShapeDtypeStruct((t, A_HEADS * HEAD_DIM), MXU_DTYPE),
        scratch_shapes=[pltpu.VMEM((2 * DIL_CHUNK, HEAD_DIM), F32)] * 2
                       + [pltpu.VMEM(blk, F32)] * 6,
        compiler_params=_params(3),
        name="dilated_attention",
    )(qa, ka, ka, h, h)


def _mla_kernel(q_ref, k_ref, v_ref, o_ref, *, scale, bq):
    qi = pl.program_id(2)
    q = q_ref[...]
    row = lax.broadcasted_iota(jnp.int32, (bq, bq), 0)
    col = lax.broadcasted_iota(jnp.int32, (bq, bq), 1)

    def step(k_start, carry, diagonal):
        m, l, acc = carry
        k = k_ref[pl.ds(k_start, bq), :]
        v = v_ref[pl.ds(k_start, bq), :]
        s = lax.dot_general(q, k, NT_DIMS, preferred_element_type=F32) * scale
        if diagonal:
            s = jnp.where(col <= row, s, NEG_INF)
        m_new = jnp.maximum(m, jnp.max(s, axis=-1, keepdims=True))
        alpha = jnp.exp(m - m_new)
        p = jnp.exp(s - m_new)
        l = alpha * l + jnp.sum(p, axis=-1, keepdims=True)
        acc = alpha * acc + jnp.dot(p.astype(MXU_DTYPE), v, preferred_element_type=F32)
        return m_new, l, acc

    carry = (jnp.full((bq, 1), NEG_INF, F32), jnp.zeros((bq, 1), F32), jnp.zeros((bq, MLA_V), F32))
    carry = lax.fori_loop(0, qi, lambda j, cr: step(pl.multiple_of(j * bq, bq), cr, False), carry)
    _, l, acc = step(pl.multiple_of(qi * bq, bq), carry, True)
    o_ref[...] = (acc / l).astype(o_ref.dtype)


def _mla_attention(qm, km, kvf, batch, seq, bq=512):
    t = qm.shape[0]
    bq = min(bq, seq)
    nq = seq // bq
    scale = (MLA_NOPE + MLA_ROPE) ** -0.5
    return pl.pallas_call(
        functools.partial(_mla_kernel, scale=scale, bq=bq),
        grid=(batch, B_HEADS, nq),
        in_specs=[pl.BlockSpec((bq, MLA_QK_PAD), lambda b, hd, i: (b * nq + i, hd)),
                  pl.BlockSpec((seq, MLA_QK_PAD), lambda b, hd, i: (b, hd)),
                  pl.BlockSpec((seq, MLA_V), lambda b, hd, i: (b, B_HEADS + hd))],
        out_specs=pl.BlockSpec((bq, MLA_V), lambda b, hd, i: (b * nq + i, hd)),
        out_shape=jax.ShapeDtypeStruct((t, B_HEADS * MLA_V), MXU_DTYPE),
        compiler_params=_params(3),
        name="mla_attention",
    )(qm, km, kvf)


def _stick_kernel(q_ref, k_ref, v_ref, o_ref, acc_ref, run_ref, *, scale, bq, bk):
    qi = pl.program_id(2)
    n_sub = bq // bk
    tri_r = lax.broadcasted_iota(jnp.int32, (bk, bk), 0)
    tri_c = lax.broadcasted_iota(jnp.int32, (bk, bk), 1)
    later = jnp.where(tri_r > tri_c, 1.0, 0.0).astype(MXU_DTYPE)
    acc_ref[...] = jnp.zeros_like(acc_ref)
    run_ref[...] = jnp.zeros_like(run_ref)

    def tile(k_start, r0, diagonal):
        rows = bq - r0
        q = q_ref[r0:, :]
        k = k_ref[pl.ds(k_start, bk), :]
        v = v_ref[pl.ds(k_start, bk), :]
        z = lax.dot_general(q, k, NT_DIMS, preferred_element_type=F32) * scale
        neg_abs = lax.bitcast_convert_type(
            lax.bitcast_convert_type(z, jnp.uint32) | jnp.uint32(0x80000000), F32)
        softplus = jnp.maximum(z, 0.0) + jnp.log(1.0 + jnp.exp(neg_abs))
        log_beta = z - softplus
        if diagonal:
            row = lax.broadcasted_iota(jnp.int32, (rows, bk), 0)
            col = lax.broadcasted_iota(jnp.int32, (rows, bk), 1)
            softplus = jnp.where(col < row, softplus, 0.0)
        hi = softplus.astype(MXU_DTYPE)
        lo = (softplus - hi.astype(F32)).astype(MXU_DTYPE)
        suffix = (jnp.dot(hi, later, preferred_element_type=F32)
                  + jnp.dot(lo, later, preferred_element_type=F32))
        att = jnp.exp(log_beta - suffix - run_ref[r0:, :])
        if diagonal:
            att = jnp.where(col < row, att, 0.0)
        acc_ref[r0:, :] += jnp.dot(att.astype(MXU_DTYPE), v, preferred_element_type=F32)
        run_ref[r0:, :] += jnp.sum(softplus, axis=-1, keepdims=True)

    for jj in reversed(range(n_sub)):
        tile(pl.multiple_of(qi * bq + jj * bk, bk), jj * bk, True)

    def body(j, carry):
        tile(pl.multiple_of((qi * n_sub - 1 - j) * bk, bk), 0, False)
        return carry

    lax.fori_loop(0, qi * n_sub, body, 0)
    o_ref[...] = acc_ref[...].astype(o_ref.dtype)


def _stick_attention(qkv, batch, seq, bq=1024, bk=256):
    t = qkv.shape[0]
    bq, bk = min(bq, seq), min(bk, seq)
    nq = seq // bq
    return pl.pallas_call(
        functools.partial(_stick_kernel, scale=HEAD_DIM ** -0.5, bq=bq, bk=bk),
        grid=(batch, C_HEADS, nq),
        in_specs=[pl.BlockSpec((bq, HEAD_DIM), lambda b, hd, i: (b * nq + i, hd)),
                  pl.BlockSpec((seq, HEAD_DIM), lambda b, hd, i: (b, C_HEADS + hd)),
                  pl.BlockSpec((seq, HEAD_DIM), lambda b, hd, i: (b, 2 * C_HEADS + hd))],
        out_specs=pl.BlockSpec((bq, HEAD_DIM), lambda b, hd, i: (b * nq + i, hd)),
        out_shape=jax.ShapeDtypeStruct((t, C_HEADS * HEAD_DIM), MXU_DTYPE),
        scratch_shapes=[pltpu.VMEM((bq, HEAD_DIM), F32), pltpu.VMEM((bq, 1), F32)],
        compiler_params=_params(3),
        name="stick_breaking_attention",
    )(qkv, qkv, qkv)


def _top_values(s, count, with_rank=False):
    vals = []
    cur = s
    rank = jnp.full(s.shape, float(count), F32) if with_rank else None
    for k in range(count):
        m = jnp.max(cur, axis=0, keepdims=True)
        vals.append(m)
        hit = cur == m
        if with_rank:
            rank = jnp.where(hit, float(k), rank)
        cur = jnp.where(hit, NEG_INF, cur)
    return (vals, rank) if with_rank else vals


def _peer_route_kernel(q_ref, keys_ref, r2_ref, b_ref, c_ref, a_ref):
    def head(hd, carry):
        col = pl.multiple_of(hd * 2 * PEER_HALF, 2 * PEER_HALF)
        q1 = q_ref[:, pl.ds(col, PEER_HALF)].astype(MXU_DTYPE)
        q2 = q_ref[:, pl.ds(col + PEER_HALF, PEER_HALF)].astype(MXU_DTYPE)
        s1 = lax.dot_general(keys_ref[hd, 0], q1, NT_DIMS, preferred_element_type=F32)
        s2 = lax.dot_general(keys_ref[hd, 1], q2, NT_DIMS, preferred_element_type=F32)
        v1 = _top_values(s1, PEER_TOPK)
        v2, rank2 = _top_values(s2, PEER_TOPK, with_rank=True)
        v1_all = jnp.concatenate(v1, axis=0)
        v2_all = jnp.concatenate(v2, axis=0)
        cand = jnp.concatenate(
            [v1[0] + v2_all]
            + [v1[p] + v2_all[:8] for p in range(1, 8)]
            + [v2[0] + v1_all[8:]], axis=0)
        tops = _top_values(cand, PEER_TOPK + 1)
        cut = 0.5 * (tops[PEER_TOPK - 1] + tops[PEER_TOPK])
        top = tops[0]
        z = jnp.sum(jnp.where(cand >= cut, jnp.exp(cand - top), 0.0), axis=0, keepdims=True)
        need = cut - s1
        count = jnp.zeros_like(s1)
        for q in range(PEER_TOPK):
            count = count + jnp.where(v2[q] >= need, 1.0, 0.0)
        a = jnp.where(s1 >= v1[PEER_TOPK - 1], jnp.exp(s1 - v1[0]) / z, 0.0)
        b = jnp.where(s2 >= v2[PEER_TOPK - 1], jnp.exp(s2 - v2[0]), 0.0)
        word_rows = pl.ds(pl.multiple_of(hd * (PEER_N_KEYS // 2), PEER_N_KEYS // 2), PEER_N_KEYS // 2)
        r2_ref[word_rows, :] = pltpu.bitcast(rank2.astype(jnp.bfloat16), jnp.uint32)
        b_ref[word_rows, :] = pltpu.bitcast(b.astype(jnp.bfloat16), jnp.uint32)
        c_ref[hd] = count
        a_ref[hd] = a
        return carry

    lax.fori_loop(0, PEER_HEADS, head, 0)


def _peer_route(qp, keys, tr=256):
    t = qp.shape[0]
    shape = (PEER_HEADS, PEER_N_KEYS, t)
    spec = pl.BlockSpec((PEER_HEADS, PEER_N_KEYS, tr), lambda i: (0, 0, i))
    flat = (PEER_HEADS * PEER_N_KEYS // 2, t)
    flat_spec = pl.BlockSpec((PEER_HEADS * PEER_N_KEYS // 2, tr), lambda i: (0, i))
    return pl.pallas_call(
        _peer_route_kernel,
        grid=(t // tr,),
        in_specs=[pl.BlockSpec((tr, qp.shape[1]), lambda i: (i, 0)),
                  pl.BlockSpec(keys.shape, lambda i: (0, 0, 0, 0))],
        out_specs=[flat_spec, flat_spec, spec, spec],
        out_shape=[jax.ShapeDtypeStruct(flat, jnp.uint32), jax.ShapeDtypeStruct(flat, jnp.uint32),
                   jax.ShapeDtypeStruct(shape, F32), jax.ShapeDtypeStruct(shape, F32)],
        compiler_params=_params(1),
        name="peer_route",
    )(qp, keys)


def _gelu_exact(x):
    return 0.5 * x * (1.0 + lax.erf(x * (2.0 ** -0.5)))


def _pair_words(rows):
    bits = lax.bitcast_convert_type(rows.astype(jnp.bfloat16).astype(F32), jnp.uint32)
    return bits | (bits >> 16)


def _row_tile(words, rows):
    packed = pltpu.bitcast(jnp.broadcast_to(words, (8, 128)), jnp.bfloat16)
    return jnp.concatenate([packed] * (rows // 16), axis=0)


def _peer_kernel(xt_ref, res_ref, u_ref, v_ref, r2_ref, b_ref, c_ref, a_ref, g_ref, bias_ref,
                 o_ref, ob_ref, st_ref, wt_ref, *, n_i, tb, alpha):
    e = pl.program_id(1)

    @pl.when(e == 0)
    def _():
        o_ref[...] = jnp.zeros_like(o_ref)

    st_ref[...] = jnp.dot(u_ref[...], xt_ref[...], preferred_element_type=F32)
    zero = jnp.zeros((PEER_N_KEYS, 128), MXU_DTYPE)
    assert 8 % n_i == 0
    per_group = 8 // n_i
    group = pl.multiple_of((e // per_group) * 8, 8)
    within = e % per_group

    def key_rows(ref, hd, lanes):
        grp = ref[hd, pl.ds(group, 8), lanes]
        rows = grp[:n_i]
        for k in range(1, per_group):
            rows = jnp.where(within == k, grp[k * n_i:(k + 1) * n_i], rows)
        return rows

    for l in range(tb // 128):
        lanes = slice(l * 128, (l + 1) * 128)
        gates = [zero for _ in range(n_i)]
        for hd in range(PEER_HEADS):
            words = slice(hd * (PEER_N_KEYS // 2), (hd + 1) * (PEER_N_KEYS // 2))
            r2 = pltpu.bitcast(r2_ref[words, lanes], jnp.bfloat16)
            b = pltpu.bitcast(b_ref[words, lanes], jnp.bfloat16)
            counts = _pair_words(key_rows(c_ref, hd, lanes))
            a_rows = _pair_words(key_rows(a_ref, hd, lanes))
            for ii in range(n_i):
                count = _row_tile(counts[ii:ii + 1], PEER_N_KEYS)
                a = _row_tile(a_rows[ii:ii + 1], PEER_N_KEYS)
                gates[ii] = gates[ii] + jnp.where(r2 < count, b, zero) * a
        for ii in range(n_i):
            rows = slice(ii * PEER_N_KEYS, (ii + 1) * PEER_N_KEYS)
            act = _gelu_exact(st_ref[rows, lanes]).astype(MXU_DTYPE)
            wt_ref[rows, lanes] = act * gates[ii]
    o_ref[...] += lax.dot_general(wt_ref[...], v_ref[...], TN_DIMS, preferred_element_type=F32)

    @pl.when(e == pl.num_programs(1) - 1)
    def _():
        y = _layer_norm_rows(alpha * res_ref[...] + o_ref[...], g_ref[...], bias_ref[...])
        o_ref[...] = y
        ob_ref[...] = y.astype(ob_ref.dtype)


def _peer_experts(xt, res, u, v, r2, b, c, a, gain, bias, alpha, tb=512, n_i=4):
    t, d = res.shape
    n_exp = u.shape[0]
    eb = n_i * PEER_N_KEYS
    tb = min(tb, t)
    tok = lambda i, e: (i, 0)
    full = pl.BlockSpec((PEER_HEADS, PEER_N_KEYS, tb), lambda i, e: (0, 0, i))
    flat = pl.BlockSpec((PEER_HEADS * PEER_N_KEYS // 2, tb), lambda i, e: (0, i))
    return pl.pallas_call(
        functools.partial(_peer_kernel, n_i=n_i, tb=tb, alpha=alpha),
        grid=(t // tb, n_exp // eb),
        in_specs=[pl.BlockSpec((d, tb), lambda i, e: (0, i)), pl.BlockSpec((tb, d), tok),
                  pl.BlockSpec((eb, d), lambda i, e: (e, 0)),
                  pl.BlockSpec((eb, d), lambda i, e: (e, 0)),
                  flat, flat, full, full,
                  pl.BlockSpec((1, d), lambda i, e: (0, 0)), pl.BlockSpec((1, d), lambda i, e: (0, 0))],
        out_specs=[pl.BlockSpec((tb, d), tok), pl.BlockSpec((tb, d), tok)],
        out_shape=[jax.ShapeDtypeStruct((t, d), F32), jax.ShapeDtypeStruct((t, d), MXU_DTYPE)],
        scratch_shapes=[pltpu.VMEM((eb, tb), F32), pltpu.VMEM((eb, tb), MXU_DTYPE)],
        compiler_params=_params(2),
        name="peer_experts",
    )(xt, res, u, v, r2, b, c, a, gain.reshape(1, d), bias.reshape(1, d))


def _peer_layer(x, xb, w_query, keys, u, v, gain, bias, alpha):
    qp = _matmul(xb, w_query, F32)
    r2, b, c, a = _peer_route(qp, keys)
    return _peer_experts(xb.T, x, u, v, r2, b, c, a, gain, bias, alpha)


def _even_mixer(x2d, batch, seq, w_in, q_norm, w_q_b, kv_norm, w_kv_b, tables):
    d = x2d.shape[1]
    o1 = 3 * A_HEADS * HEAD_DIM
    o3 = o1 + MLA_Q_LORA + MLA_KV_LORA
    w_in_p = jnp.concatenate(
        [w_in, jnp.zeros((d, 4096 - w_in.shape[1]), w_in.dtype)], axis=1).astype(MXU_DTYPE)
    h = _matmul(x2d, w_in_p, F32)
    cos_a, sin_a, cos_b, sin_lo, sin_hi = tables
    qa, ka, cqn, ckvn, kr = _even_prep(h, tables, q_norm, kv_norm, seq)
    out_a = _dilated_attention(qa, ka, h, batch, seq)
    wq = w_q_b.reshape(MLA_Q_LORA, B_HEADS, MLA_NOPE + MLA_ROPE)
    wq = jnp.pad(wq, ((0, 0), (0, 0), (0, MLA_QK_PAD - MLA_NOPE - MLA_ROPE)))
    wq = wq.reshape(MLA_Q_LORA, B_HEADS * MLA_QK_PAD).astype(MXU_DTYPE)
    wkv = w_kv_b.reshape(MLA_KV_LORA, B_HEADS, MLA_NOPE + MLA_V)
    wkv = jnp.concatenate([wkv[:, :, :MLA_NOPE].reshape(MLA_KV_LORA, -1),
                           wkv[:, :, MLA_NOPE:].reshape(MLA_KV_LORA, -1)], axis=1).astype(MXU_DTYPE)
    qf = _matmul(cqn, wq, F32)
    kvf = _matmul(ckvn, wkv, MXU_DTYPE)
    qm, km = _mla_prep(qf, kvf, kr, (cos_b, sin_lo, sin_hi), seq)
    out_b = _mla_attention(qm, km, kvf, batch, seq)
    del o3
    return out_a, out_b


def kernel(x, a_w_in, b_q_norm, b_w_q_b, b_kv_norm, b_w_kv_b, ab_w_out, c_w_in, c_w_out,
           peer_w_query, peer_sub_keys, peer_u, peer_v, ln_gain, ln_bias):
    batch, seq, d = x.shape
    depth = peer_u.shape[0]
    alpha = (2 * depth) ** 0.25
    tables = _rope_tables(seq)
    xf = x.reshape(batch * seq, d)
    xb = None
    for layer in range(depth):
        i = layer // 2
        if layer % 2 == 0:
            src = xf if xb is None else xb
            out_a, out_b = _even_mixer(src, batch, seq, a_w_in[i], b_q_norm[i], b_w_q_b[i],
                                       b_kv_norm[i], b_w_kv_b[i], tables)
            w_out = ab_w_out[i].astype(MXU_DTYPE)
            half = A_HEADS * HEAD_DIM
            xf, xb = _matmul_residual_ln([out_a, out_b], [w_out[:half], w_out[half:]], xf,
                                         ln_gain[layer, 0], ln_bias[layer, 0], alpha)
        else:
            src = xf if xb is None else xb
            qkv = _matmul(src, c_w_in[i].astype(MXU_DTYPE), MXU_DTYPE)
            o = _stick_attention(qkv, batch, seq)
            xf, xb = _matmul_residual_ln([o], [c_w_out[i].astype(MXU_DTYPE)], xf,
                                         ln_gain[layer, 0], ln_bias[layer, 0], alpha)
        xf, xb = _peer_layer(xf, xb, peer_w_query[layer].astype(MXU_DTYPE),
                         peer_sub_keys[layer].astype(MXU_DTYPE),
                         peer_u[layer].astype(MXU_DTYPE), peer_v[layer].astype(MXU_DTYPE),
                         ln_gain[layer, 1], ln_bias[layer, 1], alpha)
    return xf.reshape(batch, seq, d)
```

```python
import functools

import jax
import jax.numpy as jnp
from jax import lax
from jax.experimental import pallas as pl
from jax.experimental.pallas import tpu as pltpu

F32 = jnp.float32
MXU_DTYPE = jnp.bfloat16

HEAD_DIM = 128
BLOCK = 128
DIL_STEPS = 128
DILATIONS = (1, 4, 16)
DIL_CHUNK = BLOCK * DILATIONS[-1]
A_HEADS = 8
B_HEADS = 8
MLA_Q_LORA = 512
MLA_KV_LORA = 256
MLA_NOPE = 128
MLA_ROPE = 64
MLA_V = 128
MLA_QK_PAD = 256
C_HEADS = 16
PEER_HEADS = 8
PEER_N_KEYS = 128
PEER_TOPK = 16
PEER_HALF = 128
ROPE_THETA = 10000.0
LN_EPS = 1e-5
RMS_EPS = 1e-6
NEG_INF = -1e30
LOG2_E = 1.4426950408889634
MLA_Q_SCALE = (MLA_NOPE + MLA_ROPE) ** -0.5 * LOG2_E

V7X_VMEM_LIMIT_BYTES = 56 * 1024 * 1024
NT_DIMS = (((1,), (1,)), ((), ()))
TN_DIMS = (((0,), (0,)), ((), ()))


def _params(n_axes):
    return pltpu.CompilerParams(dimension_semantics=("arbitrary",) * n_axes,
                                vmem_limit_bytes=V7X_VMEM_LIMIT_BYTES)


def _mm_kernel(a_ref, b_ref, o_ref):
    a = a_ref[...].astype(MXU_DTYPE)
    o_ref[...] = jnp.dot(a, b_ref[...], preferred_element_type=F32).astype(o_ref.dtype)


def _matmul(a, b, out_dtype, bm=1024, bn=1024):
    m, k = a.shape
    n = b.shape[1]
    bm, bn = min(bm, m), min(bn, n)
    assert m % bm == 0 and n % bn == 0
    return pl.pallas_call(
        _mm_kernel,
        grid=(m // bm, n // bn),
        in_specs=[pl.BlockSpec((bm, k), lambda i, j: (i, 0)),
                  pl.BlockSpec((k, bn), lambda i, j: (0, j))],
        out_specs=pl.BlockSpec((bm, bn), lambda i, j: (i, j)),
        out_shape=jax.ShapeDtypeStruct((m, n), out_dtype),
        compiler_params=_params(2),
        name="matmul",
    )(a, b)


def _layer_norm_rows(y, g, b):
    mu = jnp.mean(y, axis=-1, keepdims=True)
    d = y - mu
    var = jnp.mean(d * d, axis=-1, keepdims=True)
    return d * lax.rsqrt(var + LN_EPS) * g + b


def _mm_ln_kernel(*refs, n_pairs, alpha):
    a_refs = refs[:n_pairs]
    w_refs = refs[n_pairs:2 * n_pairs]
    res_ref, g_ref, b_ref, o_ref, ob_ref = refs[2 * n_pairs:]
    acc = jnp.dot(a_refs[0][...].astype(MXU_DTYPE), w_refs[0][...], preferred_element_type=F32)
    for a_ref, w_ref in zip(a_refs[1:], w_refs[1:]):
        acc = acc + jnp.dot(a_ref[...].astype(MXU_DTYPE), w_ref[...], preferred_element_type=F32)
    y = _layer_norm_rows(alpha * res_ref[...] + acc, g_ref[...], b_ref[...])
    o_ref[...] = y
    ob_ref[...] = y.astype(ob_ref.dtype)


def _matmul_residual_ln(a_list, w_list, res, gain, bias, alpha, bm=256):
    m, n = res.shape
    n_pairs = len(a_list)
    in_specs = [pl.BlockSpec((bm, a.shape[1]), lambda i: (i, 0)) for a in a_list]
    in_specs += [pl.BlockSpec(w.shape, lambda i: (0, 0)) for w in w_list]
    in_specs += [pl.BlockSpec((bm, n), lambda i: (i, 0)),
                 pl.BlockSpec((1, n), lambda i: (0, 0)),
                 pl.BlockSpec((1, n), lambda i: (0, 0))]
    return pl.pallas_call(
        functools.partial(_mm_ln_kernel, n_pairs=n_pairs, alpha=alpha),
        grid=(m // bm,),
        in_specs=in_specs,
        out_specs=[pl.BlockSpec((bm, n), lambda i: (i, 0)), pl.BlockSpec((bm, n), lambda i: (i, 0))],
        out_shape=[jax.ShapeDtypeStruct((m, n), F32), jax.ShapeDtypeStruct((m, n), MXU_DTYPE)],
        compiler_params=_params(1),
        name="matmul_residual_ln",
    )(*a_list, *w_list, res, gain.reshape(1, n), bias.reshape(1, n))


def _rope_tables(seq):
    pos = jnp.arange(seq, dtype=F32)[:, None]

    def cos_sin(half):
        inv = ROPE_THETA ** (-jnp.arange(half, dtype=F32) / half)
        ang = pos * inv[None, :]
        return jnp.cos(ang), jnp.sin(ang)

    c, s = cos_sin(HEAD_DIM // 2)
    cos_a = jnp.concatenate([c, c], axis=1)
    sin_a = jnp.concatenate([-s, s], axis=1)
    c, s = cos_sin(MLA_ROPE // 2)
    z32 = jnp.zeros_like(c)
    cos_b = jnp.concatenate([c, c, z32, z32], axis=1)
    sin_lo = jnp.concatenate([-s, z32, z32, z32], axis=1)
    sin_hi = jnp.concatenate([z32, s, z32, z32], axis=1)
    return cos_a, sin_a, cos_b, sin_lo, sin_hi


def _rope128(x, cos_a, sin_a):
    return x * cos_a + pltpu.roll(x, HEAD_DIM // 2, 1) * sin_a


def _rope64(x, cos_b, sin_lo, sin_hi):
    return x * cos_b + pltpu.roll(x, 96, 1) * sin_lo + pltpu.roll(x, 32, 1) * sin_hi


def _rms_norm_rows(x, g):
    return x * lax.rsqrt(jnp.mean(x * x, axis=-1, keepdims=True) + RMS_EPS) * g


def _even_prep_kernel(qk_ref, tail_ref, cos_a_ref, sin_a_ref, cos_b_ref, sin_lo_ref, sin_hi_ref,
                      qn_ref, kvn_ref, qa_ref, ka_ref, cq_ref, ckv_ref, kr_ref):
    cos_a, sin_a = cos_a_ref[...], sin_a_ref[...]
    n_qk = A_HEADS * HEAD_DIM
    for hd in range(A_HEADS):
        lo = hd * HEAD_DIM
        qa_ref[:, lo:lo + HEAD_DIM] = _rope128(qk_ref[:, lo:lo + HEAD_DIM], cos_a, sin_a)
        ka_ref[:, lo:lo + HEAD_DIM] = _rope128(qk_ref[:, n_qk + lo:n_qk + lo + HEAD_DIM], cos_a, sin_a)
    cq_ref[...] = _rms_norm_rows(tail_ref[:, :MLA_Q_LORA], qn_ref[...]).astype(cq_ref.dtype)
    o2 = MLA_Q_LORA + MLA_KV_LORA
    ckv_ref[...] = _rms_norm_rows(tail_ref[:, MLA_Q_LORA:o2], kvn_ref[...]).astype(ckv_ref.dtype)
    kr = _rope64(tail_ref[:, o2:o2 + 128], cos_b_ref[...], sin_lo_ref[...], sin_hi_ref[...])
    kr_ref[...] = kr.astype(kr_ref.dtype)


def _even_prep(h, tables, q_norm, kv_norm, seq, rows=256):
    t = h.shape[0]
    n_qk = A_HEADS * HEAD_DIM
    sb = seq // rows
    tab_spec = pl.BlockSpec((rows, 128), lambda i: (i % sb, 0))
    return pl.pallas_call(
        _even_prep_kernel,
        grid=(t // rows,),
        in_specs=[pl.BlockSpec((rows, 2 * n_qk), lambda i: (i, 0)),
                  pl.BlockSpec((rows, 1024), lambda i: (i, 3)),
                  tab_spec, tab_spec, tab_spec, tab_spec, tab_spec,
                  pl.BlockSpec((1, MLA_Q_LORA), lambda i: (0, 0)),
                  pl.BlockSpec((1, MLA_KV_LORA), lambda i: (0, 0))],
        out_specs=[pl.BlockSpec((rows, n_qk), lambda i: (i, 0)),
                   pl.BlockSpec((rows, n_qk), lambda i: (i, 0)),
                   pl.BlockSpec((rows, MLA_Q_LORA), lambda i: (i, 0)),
                   pl.BlockSpec((rows, MLA_KV_LORA), lambda i: (i, 0)),
                   pl.BlockSpec((rows, 128), lambda i: (i, 0))],
        out_shape=[jax.ShapeDtypeStruct((t, n_qk), F32),
                   jax.ShapeDtypeStruct((t, n_qk), F32),
                   jax.ShapeDtypeStruct((t, MLA_Q_LORA), MXU_DTYPE),
                   jax.ShapeDtypeStruct((t, MLA_KV_LORA), MXU_DTYPE),
                   jax.ShapeDtypeStruct((t, 128), MXU_DTYPE)],
        compiler_params=_params(1),
        name="even_prep",
    )(h, h, *tables, q_norm.reshape(1, -1), kv_norm.reshape(1, -1))


def _mla_prep_kernel(q_ref, kn_ref, kr_ref, cos_b_ref, sin_lo_ref, sin_hi_ref, qm_ref, km_ref):
    cos_b, sin_lo, sin_hi = cos_b_ref[...], sin_lo_ref[...], sin_hi_ref[...]
    kr = kr_ref[...]
    for hd in range(B_HEADS):
        lo = hd * MLA_QK_PAD
        qm_ref[:, lo:lo + MLA_NOPE] = (q_ref[:, lo:lo + MLA_NOPE] * MLA_Q_SCALE).astype(qm_ref.dtype)
        q_rope = _rope64(q_ref[:, lo + MLA_NOPE:lo + MLA_QK_PAD], cos_b, sin_lo, sin_hi)
        qm_ref[:, lo + MLA_NOPE:lo + MLA_QK_PAD] = (q_rope * MLA_Q_SCALE).astype(qm_ref.dtype)
        km_ref[:, lo:lo + MLA_NOPE] = kn_ref[:, hd * MLA_NOPE:(hd + 1) * MLA_NOPE]
        km_ref[:, lo + MLA_NOPE:lo + MLA_QK_PAD] = kr


def _mla_prep(qf, kvf, kr, tables, seq, rows=256):
    t = qf.shape[0]
    w = B_HEADS * MLA_QK_PAD
    sb = seq // rows
    tab_spec = pl.BlockSpec((rows, 128), lambda i: (i % sb, 0))
    return pl.pallas_call(
        _mla_prep_kernel,
        grid=(t // rows,),
        in_specs=[pl.BlockSpec((rows, w), lambda i: (i, 0)),
                  pl.BlockSpec((rows, B_HEADS * MLA_NOPE), lambda i: (i, 0)),
                  pl.BlockSpec((rows, 128), lambda i: (i, 0)),
                  tab_spec, tab_spec, tab_spec],
        out_specs=[pl.BlockSpec((rows, w), lambda i: (i, 0)), pl.BlockSpec((rows, w), lambda i: (i, 0))],
        out_shape=[jax.ShapeDtypeStruct((t, w), MXU_DTYPE), jax.ShapeDtypeStruct((t, w), MXU_DTYPE)],
        compiler_params=_params(1),
        name="mla_prep",
    )(qf, kvf, kr, *tables)


def _dilated_kernel(q_ref, kc_ref, kp_ref, vc_ref, vp_ref, o_ref,
                    kbuf, vbuf, o0, o1, o2, l0, l1, l2):
    c = pl.program_id(2)
    kbuf[:DIL_CHUNK, :] = kp_ref[...]
    kbuf[DIL_CHUNK:, :] = kc_ref[...]
    vbuf[:DIL_CHUNK, :] = vp_ref[...]
    vbuf[DIL_CHUNK:, :] = vc_ref[...]
    scale = HEAD_DIM ** -0.5
    row = lax.broadcasted_iota(jnp.int32, (BLOCK, 2 * BLOCK), 0)
    col = lax.broadcasted_iota(jnp.int32, (BLOCK, 2 * BLOCK), 1)
    band = (col >= row) & (col <= row + DIL_STEPS)

    def blocks(specs):
        staged = []
        for q_start, k_start, dil, first, o_g, l_g in specs:
            if dil == 1:
                q_idx, k_idx = pl.ds(q_start, BLOCK), pl.ds(k_start, 2 * BLOCK)
            else:
                q_idx = pl.ds(q_start, BLOCK, stride=dil)
                k_idx = pl.ds(k_start, 2 * BLOCK, stride=dil)
            q = q_ref[q_idx, :].astype(MXU_DTYPE)
            k = kbuf[k_idx, :].astype(MXU_DTYPE)
            v = vbuf[k_idx, :].astype(MXU_DTYPE)
            logits = lax.dot_general(q, k, NT_DIMS, preferred_element_type=F32) * scale
            staged.append((logits, v, q_idx, first, o_g, l_g))
        weighted = []
        for logits, v, q_idx, first, o_g, l_g in staged:
            valid_from = jnp.where(first, BLOCK, 0)
            logits = jnp.where(band & (col >= valid_from), logits, NEG_INF)
            m = jnp.max(logits, axis=-1, keepdims=True)
            p = jnp.exp(logits - m)
            denom = jnp.sum(p, axis=-1, keepdims=True)
            weighted.append(((p / denom).astype(MXU_DTYPE), m + jnp.log(denom)))
        for (logits, v, q_idx, first, o_g, l_g), (pn, lse) in zip(staged, weighted):
            o_g[q_idx, :] = jnp.dot(pn, v, preferred_element_type=F32)
            l_g[q_idx, :] = jnp.broadcast_to(lse, (BLOCK, HEAD_DIM))

    n_blocks = DIL_CHUNK // BLOCK
    per_trip = 2

    def body(j, carry):
        specs = []
        for u in range(per_trip):
            i = j * per_trip + u
            specs.append((pl.multiple_of(i * BLOCK, BLOCK),
                          pl.multiple_of(DIL_CHUNK + (i - 1) * BLOCK, BLOCK),
                          DILATIONS[0], (c == 0) & (i == 0), o0, l0))
            d1 = DILATIONS[1]
            n, r = i // d1, i % d1
            span = BLOCK * d1
            specs.append((n * span + r, DIL_CHUNK + (n - 1) * span + r, d1, (c == 0) & (n == 0), o1, l1))
            specs.append((i, DIL_CHUNK - BLOCK * DILATIONS[2] + i, DILATIONS[2], c == 0, o2, l2))
        blocks(specs)
        return carry

    lax.fori_loop(0, n_blocks // per_trip, body, 0)

    la, lb, lc = l0[...], l1[...], l2[...]
    mx = jnp.maximum(jnp.maximum(la, lb), lc)
    ea, eb, ec = jnp.exp(la - mx), jnp.exp(lb - mx), jnp.exp(lc - mx)
    merged = (ea * o0[...] + eb * o1[...] + ec * o2[...]) / (ea + eb + ec)
    o_ref[...] = merged.astype(o_ref.dtype)


def _dilated_attention(qa, ka, h, batch, seq):
    t = qa.shape[0]
    nc = seq // DIL_CHUNK
    v_col0 = 2 * A_HEADS
    cur = lambda b, hd, c: (b * nc + c, hd)
    prev = lambda b, hd, c: (b * nc + jnp.maximum(c - 1, 0), hd)
    blk = (DIL_CHUNK, HEAD_DIM)
    return pl.pallas_call(
        _dilated_kernel,
        grid=(batch, A_HEADS, nc),
        in_specs=[pl.BlockSpec(blk, cur),
                  pl.BlockSpec(blk, cur), pl.BlockSpec(blk, prev),
                  pl.BlockSpec(blk, lambda b, hd, c: (b * nc + c, v_col0 + hd)),
                  pl.BlockSpec(blk, lambda b, hd, c: (b * nc + jnp.maximum(c - 1, 0), v_col0 + hd))],
        out_specs=pl.BlockSpec(blk, cur),
        out_shape=jax.ShapeDtypeStruct((t, A_HEADS * HEAD_DIM), MXU_DTYPE),
        scratch_shapes=[pltpu.VMEM((2 * DIL_CHUNK, HEAD_DIM), F32)] * 2
                       + [pltpu.VMEM(blk, F32)] * 6,
        compiler_params=_params(3),
        name="dilated_attention",
    )(qa, ka, ka, h, h)


def _mla_kernel(q_ref, k_ref, v_ref, o_ref, m_ref, l_ref, acc_ref, *, bq, bk, unroll):
    qi = pl.program_id(2)
    n_sub = bq // bk
    m_ref[...] = jnp.full_like(m_ref, NEG_INF)
    l_ref[...] = jnp.zeros_like(l_ref)
    acc_ref[...] = jnp.zeros_like(acc_ref)

    def group(k_starts, r0, diagonal):
        cols = bq - r0
        q = q_ref[r0:, :]
        ss = [lax.dot_general(k_ref[pl.ds(ks, bk), :], q, NT_DIMS, preferred_element_type=F32)
              for ks in k_starts]
        if diagonal:
            key = lax.broadcasted_iota(jnp.int32, (bk, cols), 0)
            qry = lax.broadcasted_iota(jnp.int32, (bk, cols), 1)
            ss = [jnp.where(key <= qry, s, NEG_INF) for s in ss]
        m_prev = m_ref[:, r0:]
        m_new = m_prev
        for s in ss:
            m_new = jnp.maximum(m_new, jnp.max(s, axis=0, keepdims=True))
        alpha = jnp.exp2(m_prev - m_new)
        l = alpha * l_ref[:, r0:]
        acc = alpha * acc_ref[:, r0:]
        for ks, s in zip(k_starts, ss):
            p = jnp.exp2(s - m_new)
            l = l + jnp.sum(p, axis=0, keepdims=True)
            acc = acc + lax.dot_general(v_ref[pl.ds(ks, bk), :], p.astype(MXU_DTYPE), TN_DIMS,
                                        preferred_element_type=F32)
        m_ref[:, r0:] = m_new
        l_ref[:, r0:] = l
        acc_ref[:, r0:] = acc

    def body(j, carry):
        group([pl.multiple_of((j * unroll + u) * bk, bk) for u in range(unroll)], 0, False)
        return carry

    assert n_sub % unroll == 0
    lax.fori_loop(0, qi * (n_sub // unroll), body, 0)
    for jj in range(n_sub):
        group([pl.multiple_of(qi * bq + jj * bk, bk)], jj * bk, True)
    o_ref[...] = (acc_ref[...] / l_ref[...]).T.astype(o_ref.dtype)


def _mla_attention(qm, km, kvf, batch, seq, bq=1024, bk=512, unroll=2):
    t = qm.shape[0]
    bq, bk = min(bq, seq), min(bk, seq)
    nq = seq // bq
    return pl.pallas_call(
        functools.partial(_mla_kernel, bq=bq, bk=bk, unroll=min(unroll, bq // bk)),
        grid=(batch, B_HEADS, nq),
        in_specs=[pl.BlockSpec((bq, MLA_QK_PAD), lambda b, hd, i: (b * nq + i, hd)),
                  pl.BlockSpec((seq, MLA_QK_PAD), lambda b, hd, i: (b, hd)),
                  pl.BlockSpec((seq, MLA_V), lambda b, hd, i: (b, B_HEADS + hd))],
        out_specs=pl.BlockSpec((bq, MLA_V), lambda b, hd, i: (b * nq + i, hd)),
        out_shape=jax.ShapeDtypeStruct((t, B_HEADS * MLA_V), MXU_DTYPE),
        scratch_shapes=[pltpu.VMEM((1, bq), F32), pltpu.VMEM((1, bq), F32),
                        pltpu.VMEM((MLA_V, bq), F32)],
        compiler_params=_params(3),
        name="mla_attention",
    )(qm, km, kvf)


def _stick_kernel(q_ref, k_ref, v_ref, o_ref, acc_ref, run_ref, *, bq, bk, unroll):
    qi = pl.program_id(2)
    n_sub = bq // bk
    tri_r = lax.broadcasted_iota(jnp.int32, (bk, bk), 0)
    tri_c = lax.broadcasted_iota(jnp.int32, (bk, bk), 1)
    later = jnp.where(tri_r > tri_c, 1.0, 0.0).astype(MXU_DTYPE)
    acc_ref[...] = jnp.zeros_like(acc_ref)
    run_ref[...] = jnp.zeros_like(run_ref)

    def tile(k_start, r0, diagonal):
        rows = bq - r0
        q = q_ref[r0:, :]
        k = k_ref[pl.ds(k_start, bk), :]
        v = v_ref[pl.ds(k_start, bk), :]
        z = lax.dot_general(q, k, NT_DIMS, preferred_element_type=F32)
        neg_abs = lax.bitcast_convert_type(
            lax.bitcast_convert_type(z, jnp.uint32) | jnp.uint32(0x80000000), F32)
        softplus = jnp.maximum(z, 0.0) + jnp.log(1.0 + jnp.exp2(neg_abs)) * LOG2_E
        log_beta = z - softplus
        if diagonal:
            row = lax.broadcasted_iota(jnp.int32, (rows, bk), 0)
            col = lax.broadcasted_iota(jnp.int32, (rows, bk), 1)
            softplus = jnp.where(col < row, softplus, 0.0)
        hi = softplus.astype(MXU_DTYPE)
        lo = (softplus - hi.astype(F32)).astype(MXU_DTYPE)
        suffix = (jnp.dot(hi, later, preferred_element_type=F32)
                  + jnp.dot(lo, later, preferred_element_type=F32))
        att = jnp.exp2(log_beta - suffix - run_ref[r0:, :])
        if diagonal:
            att = jnp.where(col < row, att, 0.0)
        acc_ref[r0:, :] += jnp.dot(att.astype(MXU_DTYPE), v, preferred_element_type=F32)
        run_ref[r0:, :] += suffix[:, :1] + softplus[:, :1]

    for jj in reversed(range(n_sub)):
        tile(pl.multiple_of(qi * bq + jj * bk, bk), jj * bk, True)

    def group(k_starts):
        q = q_ref[...]
        zs = [lax.dot_general(q, k_ref[pl.ds(ks, bk), :], NT_DIMS, preferred_element_type=F32)
              for ks in k_starts]
        sps, lbs = [], []
        for z in zs:
            neg_abs = lax.bitcast_convert_type(
                lax.bitcast_convert_type(z, jnp.uint32) | jnp.uint32(0x80000000), F32)
            sp = jnp.maximum(z, 0.0) + jnp.log(1.0 + jnp.exp2(neg_abs)) * LOG2_E
            sps.append(sp)
            lbs.append(z - sp)
        sufs = []
        for sp in sps:
            hi = sp.astype(MXU_DTYPE)
            lo = (sp - hi.astype(F32)).astype(MXU_DTYPE)
            sufs.append(jnp.dot(hi, later, preferred_element_type=F32)
                        + jnp.dot(lo, later, preferred_element_type=F32))
        run = run_ref[...]
        acc = acc_ref[...]
        for ks, sp, lb, suf in zip(k_starts, sps, lbs, sufs):
            att = jnp.exp2(lb - suf - run)
            acc = acc + jnp.dot(att.astype(MXU_DTYPE), v_ref[pl.ds(ks, bk), :],
                                preferred_element_type=F32)
            run = run + suf[:, :1] + sp[:, :1]
        acc_ref[...] = acc
        run_ref[...] = run

    def body(j, carry):
        group([pl.multiple_of((qi * n_sub - 1 - j * unroll - u) * bk, bk) for u in range(unroll)])
        return carry

    assert n_sub % unroll == 0
    lax.fori_loop(0, qi * (n_sub // unroll), body, 0)
    o_ref[...] = acc_ref[...].astype(o_ref.dtype)


def _stick_attention(qkv, batch, seq, bq=1024, bk=256, unroll=4):
    t = qkv.shape[0]
    bq, bk = min(bq, seq), min(bk, seq)
    nq = seq // bq
    return pl.pallas_call(
        functools.partial(_stick_kernel, bq=bq, bk=bk, unroll=min(unroll, bq // bk)),
        grid=(batch, C_HEADS, nq),
        in_specs=[pl.BlockSpec((bq, HEAD_DIM), lambda b, hd, i: (b * nq + i, hd)),
                  pl.BlockSpec((seq, HEAD_DIM), lambda b, hd, i: (b, C_HEADS + hd)),
                  pl.BlockSpec((seq, HEAD_DIM), lambda b, hd, i: (b, 2 * C_HEADS + hd))],
        out_specs=pl.BlockSpec((bq, HEAD_DIM), lambda b, hd, i: (b * nq + i, hd)),
        out_shape=jax.ShapeDtypeStruct((t, C_HEADS * HEAD_DIM), MXU_DTYPE),
        scratch_shapes=[pltpu.VMEM((bq, HEAD_DIM), F32), pltpu.VMEM((bq, 1), F32)],
        compiler_params=_params(3),
        name="stick_breaking_attention",
    )(qkv, qkv, qkv)


def _top_values(s, count, with_rank=False):
    vals = []
    cur = s
    rank = jnp.full(s.shape, float(count), F32) if with_rank else None
    for k in range(count):
        m = jnp.max(cur, axis=0, keepdims=True)
        vals.append(m)
        hit = cur == m
        if with_rank:
            rank = jnp.where(hit, float(k), rank)
        cur = jnp.where(hit, NEG_INF, cur)
    return (vals, rank) if with_rank else vals


def _peer_route_kernel(q_ref, keys_ref, r2_ref, b_ref, c_ref, a_ref):
    def head(hd, carry):
        col = pl.multiple_of(hd * 2 * PEER_HALF, 2 * PEER_HALF)
        q1 = q_ref[:, pl.ds(col, PEER_HALF)].astype(MXU_DTYPE)
        q2 = q_ref[:, pl.ds(col + PEER_HALF, PEER_HALF)].astype(MXU_DTYPE)
        s1 = lax.dot_general(keys_ref[hd, 0], q1, NT_DIMS, preferred_element_type=F32)
        s2 = lax.dot_general(keys_ref[hd, 1], q2, NT_DIMS, preferred_element_type=F32)
        v1 = _top_values(s1, PEER_TOPK)
        v2, rank2 = _top_values(s2, PEER_TOPK, with_rank=True)
        v1_all = jnp.concatenate(v1, axis=0)
        v2_all = jnp.concatenate(v2, axis=0)
        cand = jnp.concatenate(
            [v1[0] + v2_all]
            + [v1[p] + v2_all[:8] for p in range(1, 8)]
            + [v2[0] + v1_all[8:]], axis=0)
        tops = _top_values(cand, PEER_TOPK + 1)
        cut = 0.5 * (tops[PEER_TOPK - 1] + tops[PEER_TOPK])
        top = tops[0]
        z = jnp.sum(jnp.where(cand >= cut, jnp.exp(cand - top), 0.0), axis=0, keepdims=True)
        need = cut - s1
        count = jnp.zeros_like(s1)
        for q in range(PEER_TOPK):
            count = count + jnp.where(v2[q] >= need, 1.0, 0.0)
        a = jnp.where(s1 >= v1[PEER_TOPK - 1], jnp.exp(s1 - v1[0]) / z, 0.0)
        b = jnp.where(s2 >= v2[PEER_TOPK - 1], jnp.exp(s2 - v2[0]), 0.0)
        word_rows = pl.ds(pl.multiple_of(hd * (PEER_N_KEYS // 2), PEER_N_KEYS // 2), PEER_N_KEYS // 2)
        r2_ref[word_rows, :] = pltpu.bitcast(rank2.astype(jnp.bfloat16), jnp.uint32)
        b_ref[word_rows, :] = pltpu.bitcast(b.astype(jnp.bfloat16), jnp.uint32)
        c_ref[hd] = count
        a_ref[hd] = a
        return carry

    lax.fori_loop(0, PEER_HEADS, head, 0)


def _peer_route(qp, keys, tr=256):
    t = qp.shape[0]
    shape = (PEER_HEADS, PEER_N_KEYS, t)
    spec = pl.BlockSpec((PEER_HEADS, PEER_N_KEYS, tr), lambda i: (0, 0, i))
    flat = (PEER_HEADS * PEER_N_KEYS // 2, t)
    flat_spec = pl.BlockSpec((PEER_HEADS * PEER_N_KEYS // 2, tr), lambda i: (0, i))
    return pl.pallas_call(
        _peer_route_kernel,
        grid=(t // tr,),
        in_specs=[pl.BlockSpec((tr, qp.shape[1]), lambda i: (i, 0)),
                  pl.BlockSpec(keys.shape, lambda i: (0, 0, 0, 0))],
        out_specs=[flat_spec, flat_spec, spec, spec],
        out_shape=[jax.ShapeDtypeStruct(flat, jnp.uint32), jax.ShapeDtypeStruct(flat, jnp.uint32),
                   jax.ShapeDtypeStruct(shape, F32), jax.ShapeDtypeStruct(shape, F32)],
        compiler_params=_params(1),
        name="peer_route",
    )(qp, keys)


def _gelu_exact(x):
    return 0.5 * x * (1.0 + lax.erf(x * (2.0 ** -0.5)))


def _pair_words(rows):
    bits = lax.bitcast_convert_type(rows.astype(jnp.bfloat16).astype(F32), jnp.uint32)
    return bits | (bits >> 16)


def _row_tile(words, rows):
    packed = pltpu.bitcast(jnp.broadcast_to(words, (8, 128)), jnp.bfloat16)
    return jnp.concatenate([packed] * (rows // 16), axis=0)


def _peer_kernel(xt_ref, res_ref, u_ref, v_ref, r2_ref, b_ref, c_ref, a_ref, g_ref, bias_ref,
                 o_ref, ob_ref, st_ref, wt_ref, *, n_i, tb, alpha):
    e = pl.program_id(1)

    @pl.when(e == 0)
    def _():
        o_ref[...] = jnp.zeros_like(o_ref)

    st_ref[...] = jnp.dot(u_ref[...], xt_ref[...], preferred_element_type=F32)
    zero = jnp.zeros((PEER_N_KEYS, 128), MXU_DTYPE)
    assert 8 % n_i == 0
    per_group = 8 // n_i
    group = pl.multiple_of((e // per_group) * 8, 8)
    within = e % per_group

    def key_rows(ref, hd, lanes):
        grp = ref[hd, pl.ds(group, 8), lanes]
        rows = grp[:n_i]
        for k in range(1, per_group):
            rows = jnp.where(within == k, grp[k * n_i:(k + 1) * n_i], rows)
        return rows

    for l in range(tb // 128):
        lanes = slice(l * 128, (l + 1) * 128)
        gates = [zero for _ in range(n_i)]
        for hd in range(PEER_HEADS):
            words = slice(hd * (PEER_N_KEYS // 2), (hd + 1) * (PEER_N_KEYS // 2))
            r2 = pltpu.bitcast(r2_ref[words, lanes], jnp.bfloat16)
            b = pltpu.bitcast(b_ref[words, lanes], jnp.bfloat16)
            counts = _pair_words(key_rows(c_ref, hd, lanes))
            a_rows = _pair_words(key_rows(a_ref, hd, lanes))
            for ii in range(n_i):
                count = _row_tile(counts[ii:ii + 1], PEER_N_KEYS)
                a = _row_tile(a_rows[ii:ii + 1], PEER_N_KEYS)
                gates[ii] = gates[ii] + jnp.where(r2 < count, b, zero) * a
        for ii in range(n_i):
            rows = slice(ii * PEER_N_KEYS, (ii + 1) * PEER_N_KEYS)
            act = _gelu_exact(st_ref[rows, lanes]).astype(MXU_DTYPE)
            wt_ref[rows, lanes] = act * gates[ii]
    o_ref[...] += lax.dot_general(wt_ref[...], v_ref[...], TN_DIMS, preferred_element_type=F32)

    @pl.when(e == pl.num_programs(1) - 1)
    def _():
        y = _layer_norm_rows(alpha * res_ref[...] + o_ref[...], g_ref[...], bias_ref[...])
        o_ref[...] = y
        ob_ref[...] = y.astype(ob_ref.dtype)


def _peer_experts(xt, res, u, v, r2, b, c, a, gain, bias, alpha, tb=512, n_i=4):
    t, d = res.shape
    n_exp = u.shape[0]
    eb = n_i * PEER_N_KEYS
    tb = min(tb, t)
    tok = lambda i, e: (i, 0)
    full = pl.BlockSpec((PEER_HEADS, PEER_N_KEYS, tb), lambda i, e: (0, 0, i))
    flat = pl.BlockSpec((PEER_HEADS * PEER_N_KEYS // 2, tb), lambda i, e: (0, i))
    return pl.pallas_call(
        functools.partial(_peer_kernel, n_i=n_i, tb=tb, alpha=alpha),
        grid=(t // tb, n_exp // eb),
        in_specs=[pl.BlockSpec((d, tb), lambda i, e: (0, i)), pl.BlockSpec((tb, d), tok),
                  pl.BlockSpec((eb, d), lambda i, e: (e, 0)),
                  pl.BlockSpec((eb, d), lambda i, e: (e, 0)),
                  flat, flat, full, full,
                  pl.BlockSpec((1, d), lambda i, e: (0, 0)), pl.BlockSpec((1, d), lambda i, e: (0, 0))],
        out_specs=[pl.BlockSpec((tb, d), tok), pl.BlockSpec((tb, d), tok)],
        out_shape=[jax.ShapeDtypeStruct((t, d), F32), jax.ShapeDtypeStruct((t, d), MXU_DTYPE)],
        scratch_shapes=[pltpu.VMEM((eb, tb), F32), pltpu.VMEM((eb, tb), MXU_DTYPE)],
        compiler_params=_params(2),
        name="peer_experts",
    )(xt, res, u, v, r2, b, c, a, gain.reshape(1, d), bias.reshape(1, d))


def _peer_layer(x, xb, w_query, keys, u, v, gain, bias, alpha):
    qp = _matmul(xb, w_query, F32)
    r2, b, c, a = _peer_route(qp, keys)
    return _peer_experts(xb.T, x, u, v, r2, b, c, a, gain, bias, alpha)


def _even_mixer(x2d, batch, seq, w_in, q_norm, w_q_b, kv_norm, w_kv_b, tables):
    d = x2d.shape[1]
    o1 = 3 * A_HEADS * HEAD_DIM
    o3 = o1 + MLA_Q_LORA + MLA_KV_LORA
    w_in_p = jnp.concatenate(
        [w_in, jnp.zeros((d, 4096 - w_in.shape[1]), w_in.dtype)], axis=1).astype(MXU_DTYPE)
    h = _matmul(x2d, w_in_p, F32)
    cos_a, sin_a, cos_b, sin_lo, sin_hi = tables
    qa, ka, cqn, ckvn, kr = _even_prep(h, tables, q_norm, kv_norm, seq)
    out_a = _dilated_attention(qa, ka, h, batch, seq)
    wq = w_q_b.reshape(MLA_Q_LORA, B_HEADS, MLA_NOPE + MLA_ROPE)
    wq = jnp.pad(wq, ((0, 0), (0, 0), (0, MLA_QK_PAD - MLA_NOPE - MLA_ROPE)))
    wq = wq.reshape(MLA_Q_LORA, B_HEADS * MLA_QK_PAD).astype(MXU_DTYPE)
    wkv = w_kv_b.reshape(MLA_KV_LORA, B_HEADS, MLA_NOPE + MLA_V)
    wkv = jnp.concatenate([wkv[:, :, :MLA_NOPE].reshape(MLA_KV_LORA, -1),
                           wkv[:, :, MLA_NOPE:].reshape(MLA_KV_LORA, -1)], axis=1).astype(MXU_DTYPE)
    qf = _matmul(cqn, wq, F32)
    kvf = _matmul(ckvn, wkv, MXU_DTYPE)
    qm, km = _mla_prep(qf, kvf, kr, (cos_b, sin_lo, sin_hi), seq)
    out_b = _mla_attention(qm, km, kvf, batch, seq)
    del o3
    return out_a, out_b


def kernel(x, a_w_in, b_q_norm, b_w_q_b, b_kv_norm, b_w_kv_b, ab_w_out, c_w_in, c_w_out,
           peer_w_query, peer_sub_keys, peer_u, peer_v, ln_gain, ln_bias):
    batch, seq, d = x.shape
    depth = peer_u.shape[0]
    alpha = (2 * depth) ** 0.25
    tables = _rope_tables(seq)
    xf = x.reshape(batch * seq, d)
    xb = None
    for layer in range(depth):
        i = layer // 2
        if layer % 2 == 0:
            src = xf if xb is None else xb
            out_a, out_b = _even_mixer(src, batch, seq, a_w_in[i], b_q_norm[i], b_w_q_b[i],
                                       b_kv_norm[i], b_w_kv_b[i], tables)
            w_out = ab_w_out[i].astype(MXU_DTYPE)
            half = A_HEADS * HEAD_DIM
            xf, xb = _matmul_residual_ln([out_a, out_b], [w_out[:half], w_out[half:]], xf,
                                         ln_gain[layer, 0], ln_bias[layer, 0], alpha)
        else:
            src = xf if xb is None else xb
            n_q = C_HEADS * HEAD_DIM
            w_qkv = jnp.concatenate([c_w_in[i][:, :n_q] * (HEAD_DIM ** -0.5 * LOG2_E),
                                     c_w_in[i][:, n_q:]], axis=1).astype(MXU_DTYPE)
            qkv = _matmul(src, w_qkv, MXU_DTYPE)
            o = _stick_attention(qkv, batch, seq)
            xf, xb = _matmul_residual_ln([o], [c_w_out[i].astype(MXU_DTYPE)], xf,
                                         ln_gain[layer, 0], ln_bias[layer, 0], alpha)
        xf, xb = _peer_layer(xf, xb, peer_w_query[layer].astype(MXU_DTYPE),
                         peer_sub_keys[layer].astype(MXU_DTYPE),
                         peer_u[layer].astype(MXU_DTYPE), peer_v[layer].astype(MXU_DTYPE),
                         ln_gain[layer, 1], ln_bias[layer, 1], alpha)
    return xf.reshape(batch, seq, d)
```

```python
import functools

import jax
import jax.numpy as jnp
from jax import lax
from jax.experimental import pallas as pl
from jax.experimental.pallas import tpu as pltpu

F32 = jnp.float32
MXU_DTYPE = jnp.bfloat16

HEAD_DIM = 128
BLOCK = 128
DIL_STEPS = 128
DILATIONS = (1, 4, 16)
DIL_CHUNK = BLOCK * DILATIONS[-1]
A_HEADS = 8
B_HEADS = 8
MLA_Q_LORA = 512
MLA_KV_LORA = 256
MLA_NOPE = 128
MLA_ROPE = 64
MLA_V = 128
MLA_QK_PAD = 256
C_HEADS = 16
PEER_HEADS = 8
PEER_N_KEYS = 128
PEER_TOPK = 16
PEER_HALF = 128
ROPE_THETA = 10000.0
LN_EPS = 1e-5
RMS_EPS = 1e-6
NEG_INF = -1e30
LOG2_E = 1.4426950408889634
MLA_Q_SCALE = (MLA_NOPE + MLA_ROPE) ** -0.5 * LOG2_E

V7X_VMEM_LIMIT_BYTES = 56 * 1024 * 1024
NT_DIMS = (((1,), (1,)), ((), ()))
TN_DIMS = (((0,), (0,)), ((), ()))


def _params(n_axes):
    return pltpu.CompilerParams(dimension_semantics=("arbitrary",) * n_axes,
                                vmem_limit_bytes=V7X_VMEM_LIMIT_BYTES)


def _mm_kernel(a_ref, b_ref, o_ref):
    a = a_ref[...].astype(MXU_DTYPE)
    o_ref[...] = jnp.dot(a, b_ref[...], preferred_element_type=F32).astype(o_ref.dtype)


def _matmul(a, b, out_dtype, bm=1024, bn=1024):
    m, k = a.shape
    n = b.shape[1]
    bm, bn = min(bm, m), min(bn, n)
    assert m % bm == 0 and n % bn == 0
    return pl.pallas_call(
        _mm_kernel,
        grid=(m // bm, n // bn),
        in_specs=[pl.BlockSpec((bm, k), lambda i, j: (i, 0)),
                  pl.BlockSpec((k, bn), lambda i, j: (0, j))],
        out_specs=pl.BlockSpec((bm, bn), lambda i, j: (i, j)),
        out_shape=jax.ShapeDtypeStruct((m, n), out_dtype),
        compiler_params=_params(2),
        name="matmul",
    )(a, b)


def _layer_norm_rows(y, g, b):
    mu = jnp.mean(y, axis=-1, keepdims=True)
    d = y - mu
    var = jnp.mean(d * d, axis=-1, keepdims=True)
    return d * lax.rsqrt(var + LN_EPS) * g + b


def _mm_ln_kernel(*refs, n_pairs, alpha):
    a_refs = refs[:n_pairs]
    w_refs = refs[n_pairs:2 * n_pairs]
    res_ref, g_ref, b_ref, o_ref, ob_ref = refs[2 * n_pairs:]
    acc = jnp.dot(a_refs[0][...].astype(MXU_DTYPE), w_refs[0][...], preferred_element_type=F32)
    for a_ref, w_ref in zip(a_refs[1:], w_refs[1:]):
        acc = acc + jnp.dot(a_ref[...].astype(MXU_DTYPE), w_ref[...], preferred_element_type=F32)
    y = _layer_norm_rows(alpha * res_ref[...] + acc, g_ref[...], b_ref[...])
    o_ref[...] = y
    ob_ref[...] = y.astype(ob_ref.dtype)


def _matmul_residual_ln(a_list, w_list, res, gain, bias, alpha, bm=256):
    m, n = res.shape
    n_pairs = len(a_list)
    in_specs = [pl.BlockSpec((bm, a.shape[1]), lambda i: (i, 0)) for a in a_list]
    in_specs += [pl.BlockSpec(w.shape, lambda i: (0, 0)) for w in w_list]
    in_specs += [pl.BlockSpec((bm, n), lambda i: (i, 0)),
                 pl.BlockSpec((1, n), lambda i: (0, 0)),
                 pl.BlockSpec((1, n), lambda i: (0, 0))]
    return pl.pallas_call(
        functools.partial(_mm_ln_kernel, n_pairs=n_pairs, alpha=alpha),
        grid=(m // bm,),
        in_specs=in_specs,
        out_specs=[pl.BlockSpec((bm, n), lambda i: (i, 0)), pl.BlockSpec((bm, n), lambda i: (i, 0))],
        out_shape=[jax.ShapeDtypeStruct((m, n), F32), jax.ShapeDtypeStruct((m, n), MXU_DTYPE)],
        compiler_params=_params(1),
        name="matmul_residual_ln",
    )(*a_list, *w_list, res, gain.reshape(1, n), bias.reshape(1, n))


def _rope_tables(seq):
    pos = jnp.arange(seq, dtype=F32)[:, None]

    def cos_sin(half):
        inv = ROPE_THETA ** (-jnp.arange(half, dtype=F32) / half)
        ang = pos * inv[None, :]
        return jnp.cos(ang), jnp.sin(ang)

    c, s = cos_sin(HEAD_DIM // 2)
    cos_a = jnp.concatenate([c, c], axis=1)
    sin_a = jnp.concatenate([-s, s], axis=1)
    c, s = cos_sin(MLA_ROPE // 2)
    z32 = jnp.zeros_like(c)
    cos_b = jnp.concatenate([c, c, z32, z32], axis=1)
    sin_lo = jnp.concatenate([-s, z32, z32, z32], axis=1)
    sin_hi = jnp.concatenate([z32, s, z32, z32], axis=1)
    return cos_a, sin_a, cos_b, sin_lo, sin_hi


def _rope128(x, cos_a, sin_a):
    return x * cos_a + pltpu.roll(x, HEAD_DIM // 2, 1) * sin_a


def _rope64(x, cos_b, sin_lo, sin_hi):
    return x * cos_b + pltpu.roll(x, 96, 1) * sin_lo + pltpu.roll(x, 32, 1) * sin_hi


def _rms_norm_rows(x, g):
    return x * lax.rsqrt(jnp.mean(x * x, axis=-1, keepdims=True) + RMS_EPS) * g


def _even_prep_kernel(qk_ref, tail_ref, cos_a_ref, sin_a_ref, cos_b_ref, sin_lo_ref, sin_hi_ref,
                      qn_ref, kvn_ref, qa_ref, ka_ref, cq_ref, ckv_ref, kr_ref):
    cos_a, sin_a = cos_a_ref[...], sin_a_ref[...]
    n_qk = A_HEADS * HEAD_DIM
    for hd in range(A_HEADS):
        lo = hd * HEAD_DIM
        qa_ref[:, lo:lo + HEAD_DIM] = _rope128(qk_ref[:, lo:lo + HEAD_DIM], cos_a, sin_a)
        ka_ref[:, lo:lo + HEAD_DIM] = _rope128(qk_ref[:, n_qk + lo:n_qk + lo + HEAD_DIM], cos_a, sin_a)
    cq_ref[...] = _rms_norm_rows(tail_ref[:, :MLA_Q_LORA], qn_ref[...]).astype(cq_ref.dtype)
    o2 = MLA_Q_LORA + MLA_KV_LORA
    ckv_ref[...] = _rms_norm_rows(tail_ref[:, MLA_Q_LORA:o2], kvn_ref[...]).astype(ckv_ref.dtype)
    kr = _rope64(tail_ref[:, o2:o2 + 128], cos_b_ref[...], sin_lo_ref[...], sin_hi_ref[...])
    kr_ref[...] = kr.astype(kr_ref.dtype)


def _even_prep(h, tables, q_norm, kv_norm, seq, rows=256):
    t = h.shape[0]
    n_qk = A_HEADS * HEAD_DIM
    sb = seq // rows
    tab_spec = pl.BlockSpec((rows, 128), lambda i: (i % sb, 0))
    return pl.pallas_call(
        _even_prep_kernel,
        grid=(t // rows,),
        in_specs=[pl.BlockSpec((rows, 2 * n_qk), lambda i: (i, 0)),
                  pl.BlockSpec((rows, 1024), lambda i: (i, 3)),
                  tab_spec, tab_spec, tab_spec, tab_spec, tab_spec,
                  pl.BlockSpec((1, MLA_Q_LORA), lambda i: (0, 0)),
                  pl.BlockSpec((1, MLA_KV_LORA), lambda i: (0, 0))],
        out_specs=[pl.BlockSpec((rows, n_qk), lambda i: (i, 0)),
                   pl.BlockSpec((rows, n_qk), lambda i: (i, 0)),
                   pl.BlockSpec((rows, MLA_Q_LORA), lambda i: (i, 0)),
                   pl.BlockSpec((rows, MLA_KV_LORA), lambda i: (i, 0)),
                   pl.BlockSpec((rows, 128), lambda i: (i, 0))],
        out_shape=[jax.ShapeDtypeStruct((t, n_qk), F32),
                   jax.ShapeDtypeStruct((t, n_qk), F32),
                   jax.ShapeDtypeStruct((t, MLA_Q_LORA), MXU_DTYPE),
                   jax.ShapeDtypeStruct((t, MLA_KV_LORA), MXU_DTYPE),
                   jax.ShapeDtypeStruct((t, 128), MXU_DTYPE)],
        compiler_params=_params(1),
        name="even_prep",
    )(h, h, *tables, q_norm.reshape(1, -1), kv_norm.reshape(1, -1))


def _mla_prep_kernel(q_ref, kn_ref, kr_ref, cos_b_ref, sin_lo_ref, sin_hi_ref, qm_ref, km_ref):
    cos_b, sin_lo, sin_hi = cos_b_ref[...], sin_lo_ref[...], sin_hi_ref[...]
    kr = kr_ref[...]
    for hd in range(B_HEADS):
        lo = hd * MLA_QK_PAD
        qm_ref[:, lo:lo + MLA_NOPE] = (q_ref[:, lo:lo + MLA_NOPE] * MLA_Q_SCALE).astype(qm_ref.dtype)
        q_rope = _rope64(q_ref[:, lo + MLA_NOPE:lo + MLA_QK_PAD], cos_b, sin_lo, sin_hi)
        qm_ref[:, lo + MLA_NOPE:lo + MLA_QK_PAD] = (q_rope * MLA_Q_SCALE).astype(qm_ref.dtype)
        km_ref[:, lo:lo + MLA_NOPE] = kn_ref[:, hd * MLA_NOPE:(hd + 1) * MLA_NOPE]
        km_ref[:, lo + MLA_NOPE:lo + MLA_QK_PAD] = kr


def _mla_prep(qf, kvf, kr, tables, seq, rows=256):
    t = qf.shape[0]
    w = B_HEADS * MLA_QK_PAD
    sb = seq // rows
    tab_spec = pl.BlockSpec((rows, 128), lambda i: (i % sb, 0))
    return pl.pallas_call(
        _mla_prep_kernel,
        grid=(t // rows,),
        in_specs=[pl.BlockSpec((rows, w), lambda i: (i, 0)),
                  pl.BlockSpec((rows, B_HEADS * MLA_NOPE), lambda i: (i, 0)),
                  pl.BlockSpec((rows, 128), lambda i: (i, 0)),
                  tab_spec, tab_spec, tab_spec],
        out_specs=[pl.BlockSpec((rows, w), lambda i: (i, 0)), pl.BlockSpec((rows, w), lambda i: (i, 0))],
        out_shape=[jax.ShapeDtypeStruct((t, w), MXU_DTYPE), jax.ShapeDtypeStruct((t, w), MXU_DTYPE)],
        compiler_params=_params(1),
        name="mla_prep",
    )(qf, kvf, kr, *tables)


def _dilated_kernel(q_ref, kc_ref, kp_ref, vc_ref, vp_ref, o_ref,
                    kbuf, vbuf, o0, o1, o2, l0, l1, l2):
    c = pl.program_id(2)
    kbuf[:DIL_CHUNK, :] = kp_ref[...]
    kbuf[DIL_CHUNK:, :] = kc_ref[...]
    vbuf[:DIL_CHUNK, :] = vp_ref[...]
    vbuf[DIL_CHUNK:, :] = vc_ref[...]
    scale = HEAD_DIM ** -0.5
    row = lax.broadcasted_iota(jnp.int32, (BLOCK, 2 * BLOCK), 0)
    col = lax.broadcasted_iota(jnp.int32, (BLOCK, 2 * BLOCK), 1)
    band = (col >= row) & (col <= row + DIL_STEPS)

    def blocks(specs):
        staged = []
        for q_start, k_start, dil, first, o_g, l_g in specs:
            if dil == 1:
                q_idx, k_idx = pl.ds(q_start, BLOCK), pl.ds(k_start, 2 * BLOCK)
            else:
                q_idx = pl.ds(q_start, BLOCK, stride=dil)
                k_idx = pl.ds(k_start, 2 * BLOCK, stride=dil)
            q = q_ref[q_idx, :].astype(MXU_DTYPE)
            k = kbuf[k_idx, :].astype(MXU_DTYPE)
            v = vbuf[k_idx, :].astype(MXU_DTYPE)
            logits = lax.dot_general(q, k, NT_DIMS, preferred_element_type=F32) * scale
            staged.append((logits, v, q_idx, first, o_g, l_g))
        weighted = []
        for logits, v, q_idx, first, o_g, l_g in staged:
            valid_from = jnp.where(first, BLOCK, 0)
            logits = jnp.where(band & (col >= valid_from), logits, NEG_INF)
            m = jnp.max(logits, axis=-1, keepdims=True)
            p = jnp.exp(logits - m)
            denom = jnp.sum(p, axis=-1, keepdims=True)
            weighted.append(((p / denom).astype(MXU_DTYPE), m + jnp.log(denom)))
        for (logits, v, q_idx, first, o_g, l_g), (pn, lse) in zip(staged, weighted):
            o_g[q_idx, :] = jnp.dot(pn, v, preferred_element_type=F32)
            l_g[q_idx, :] = jnp.broadcast_to(lse, (BLOCK, HEAD_DIM))

    n_blocks = DIL_CHUNK // BLOCK
    per_trip = 2

    def body(j, carry):
        specs = []
        for u in range(per_trip):
            i = j * per_trip + u
            specs.append((pl.multiple_of(i * BLOCK, BLOCK),
                          pl.multiple_of(DIL_CHUNK + (i - 1) * BLOCK, BLOCK),
                          DILATIONS[0], (c == 0) & (i == 0), o0, l0))
            d1 = DILATIONS[1]
            n, r = i // d1, i % d1
            span = BLOCK * d1
            specs.append((n * span + r, DIL_CHUNK + (n - 1) * span + r, d1, (c == 0) & (n == 0), o1, l1))
            specs.append((i, DIL_CHUNK - BLOCK * DILATIONS[2] + i, DILATIONS[2], c == 0, o2, l2))
        blocks(specs)
        return carry

    lax.fori_loop(0, n_blocks // per_trip, body, 0)

    la, lb, lc = l0[...], l1[...], l2[...]
    mx = jnp.maximum(jnp.maximum(la, lb), lc)
    ea, eb, ec = jnp.exp(la - mx), jnp.exp(lb - mx), jnp.exp(lc - mx)
    merged = (ea * o0[...] + eb * o1[...] + ec * o2[...]) / (ea + eb + ec)
    o_ref[...] = merged.astype(o_ref.dtype)


def _dilated_attention(qa, ka, h, batch, seq):
    t = qa.shape[0]
    nc = seq // DIL_CHUNK
    v_col0 = 2 * A_HEADS
    cur = lambda b, hd, c: (b * nc + c, hd)
    prev = lambda b, hd, c: (b * nc + jnp.maximum(c - 1, 0), hd)
    blk = (DIL_CHUNK, HEAD_DIM)
    return pl.pallas_call(
        _dilated_kernel,
        grid=(batch, A_HEADS, nc),
        in_specs=[pl.BlockSpec(blk, cur),
                  pl.BlockSpec(blk, cur), pl.BlockSpec(blk, prev),
                  pl.BlockSpec(blk, lambda b, hd, c: (b * nc + c, v_col0 + hd)),
                  pl.BlockSpec(blk, lambda b, hd, c: (b * nc + jnp.maximum(c - 1, 0), v_col0 + hd))],
        out_specs=pl.BlockSpec(blk, cur),
        out_shape=jax.ShapeDtypeStruct((t, A_HEADS * HEAD_DIM), MXU_DTYPE),
        scratch_shapes=[pltpu.VMEM((2 * DIL_CHUNK, HEAD_DIM), F32)] * 2
                       + [pltpu.VMEM(blk, F32)] * 6,
        compiler_params=_params(3),
        name="dilated_attention",
    )(qa, ka, ka, h, h)


def _mla_kernel(q_ref, k_ref, v_ref, o_ref, m_ref, l_ref, acc_ref, *, bq, bk, unroll):
    qi = pl.program_id(2)
    n_sub = bq // bk
    m_ref[...] = jnp.full_like(m_ref, NEG_INF)
    l_ref[...] = jnp.zeros_like(l_ref)
    acc_ref[...] = jnp.zeros_like(acc_ref)

    def group(k_starts, r0, diagonal):
        cols = bq - r0
        q = q_ref[r0:, :]
        ss = [lax.dot_general(k_ref[pl.ds(ks, bk), :], q, NT_DIMS, preferred_element_type=F32)
              for ks in k_starts]
        if diagonal:
            key = lax.broadcasted_iota(jnp.int32, (bk, cols), 0)
            qry = lax.broadcasted_iota(jnp.int32, (bk, cols), 1)
            ss = [jnp.where(key <= qry, s, NEG_INF) for s in ss]
        m_prev = m_ref[:, r0:]
        m_new = m_prev
        for s in ss:
            m_new = jnp.maximum(m_new, jnp.max(s, axis=0, keepdims=True))
        alpha = jnp.exp2(m_prev - m_new)
        l = alpha * l_ref[:, r0:]
        acc = alpha * acc_ref[:, r0:]
        for ks, s in zip(k_starts, ss):
            p = jnp.exp2(s - m_new)
            l = l + jnp.sum(p, axis=0, keepdims=True)
            acc = acc + lax.dot_general(v_ref[pl.ds(ks, bk), :], p.astype(MXU_DTYPE), TN_DIMS,
                                        preferred_element_type=F32)
        m_ref[:, r0:] = m_new
        l_ref[:, r0:] = l
        acc_ref[:, r0:] = acc

    def body(j, carry):
        group([pl.multiple_of((j * unroll + u) * bk, bk) for u in range(unroll)], 0, False)
        return carry

    assert n_sub % unroll == 0
    lax.fori_loop(0, qi * (n_sub // unroll), body, 0)
    for jj in range(n_sub):
        group([pl.multiple_of(qi * bq + jj * bk, bk)], jj * bk, True)
    o_ref[...] = (acc_ref[...] / l_ref[...]).T.astype(o_ref.dtype)


def _mla_attention(qm, km, kvf, batch, seq, bq=1024, bk=512, unroll=2):
    t = qm.shape[0]
    bq, bk = min(bq, seq), min(bk, seq)
    nq = seq // bq
    return pl.pallas_call(
        functools.partial(_mla_kernel, bq=bq, bk=bk, unroll=min(unroll, bq // bk)),
        grid=(batch, B_HEADS, nq),
        in_specs=[pl.BlockSpec((bq, MLA_QK_PAD), lambda b, hd, i: (b * nq + i, hd)),
                  pl.BlockSpec((seq, MLA_QK_PAD), lambda b, hd, i: (b, hd)),
                  pl.BlockSpec((seq, MLA_V), lambda b, hd, i: (b, B_HEADS + hd))],
        out_specs=pl.BlockSpec((bq, MLA_V), lambda b, hd, i: (b * nq + i, hd)),
        out_shape=jax.ShapeDtypeStruct((t, B_HEADS * MLA_V), MXU_DTYPE),
        scratch_shapes=[pltpu.VMEM((1, bq), F32), pltpu.VMEM((1, bq), F32),
                        pltpu.VMEM((MLA_V, bq), F32)],
        compiler_params=_params(3),
        name="mla_attention",
    )(qm, km, kvf)


def _stick_kernel(q_ref, k_ref, v_ref, o_ref, acc_ref, run_ref, *, bq, bk, unroll):
    qi = pl.program_id(2)
    n_sub = bq // bk
    tri_r = lax.broadcasted_iota(jnp.int32, (bk, bk), 0)
    tri_c = lax.broadcasted_iota(jnp.int32, (bk, bk), 1)
    at_or_after = jnp.where(tri_r >= tri_c, 1.0, 0.0).astype(MXU_DTYPE)
    acc_ref[...] = jnp.zeros_like(acc_ref)
    run_ref[...] = jnp.zeros_like(run_ref)

    def group(tiles):
        staged = []
        for ks, r0, diagonal in tiles:
            rows = bq - r0
            z = lax.dot_general(q_ref[r0:, :], k_ref[pl.ds(ks, bk), :], NT_DIMS,
                                preferred_element_type=F32)
            neg_abs = lax.bitcast_convert_type(
                lax.bitcast_convert_type(z, jnp.uint32) | jnp.uint32(0x80000000), F32)
            sp = jnp.maximum(z, 0.0) + jnp.log(1.0 + jnp.exp2(neg_abs)) * LOG2_E
            mask = None
            if diagonal:
                row = lax.broadcasted_iota(jnp.int32, (rows, bk), 0)
                col = lax.broadcasted_iota(jnp.int32, (rows, bk), 1)
                mask = col < row
                sp = jnp.where(mask, sp, 0.0)
            staged.append((z, sp, mask))
        sufs = []
        for _, sp, _ in staged:
            hi = sp.astype(MXU_DTYPE)
            lo = (sp - hi.astype(F32)).astype(MXU_DTYPE)
            sufs.append(jnp.dot(hi, at_or_after, preferred_element_type=F32)
                        + jnp.dot(lo, at_or_after, preferred_element_type=F32))
        run = run_ref[...]
        acc = acc_ref[...]
        for (ks, r0, _), (z, _, mask), suf in zip(tiles, staged, sufs):
            att = jnp.exp2(z - suf - run[r0:])
            if mask is not None:
                att = jnp.where(mask, att, 0.0)
            part = jnp.dot(att.astype(MXU_DTYPE), v_ref[pl.ds(ks, bk), :], preferred_element_type=F32)
            total = suf[:, :1]
            if r0 == 0:
                acc, run = acc + part, run + total
            else:
                acc = jnp.concatenate([acc[:r0], acc[r0:] + part], axis=0)
                run = jnp.concatenate([run[:r0], run[r0:] + total], axis=0)
        acc_ref[...] = acc
        run_ref[...] = run

    group([(pl.multiple_of(qi * bq + jj * bk, bk), jj * bk, True) for jj in reversed(range(n_sub))])

    def body(j, carry):
        group([(pl.multiple_of((qi * n_sub - 1 - j * unroll - u) * bk, bk), 0, False)
               for u in range(unroll)])
        return carry

    assert n_sub % unroll == 0
    lax.fori_loop(0, qi * (n_sub // unroll), body, 0)
    o_ref[...] = acc_ref[...].astype(o_ref.dtype)


def _stick_attention(qkv, batch, seq, bq=1024, bk=256, unroll=4):
    t = qkv.shape[0]
    bq, bk = min(bq, seq), min(bk, seq)
    nq = seq // bq
    return pl.pallas_call(
        functools.partial(_stick_kernel, bq=bq, bk=bk, unroll=min(unroll, bq // bk)),
        grid=(batch, C_HEADS, nq),
        in_specs=[pl.BlockSpec((bq, HEAD_DIM), lambda b, hd, i: (b * nq + i, hd)),
                  pl.BlockSpec((seq, HEAD_DIM), lambda b, hd, i: (b, C_HEADS + hd)),
                  pl.BlockSpec((seq, HEAD_DIM), lambda b, hd, i: (b, 2 * C_HEADS + hd))],
        out_specs=pl.BlockSpec((bq, HEAD_DIM), lambda b, hd, i: (b * nq + i, hd)),
        out_shape=jax.ShapeDtypeStruct((t, C_HEADS * HEAD_DIM), MXU_DTYPE),
        scratch_shapes=[pltpu.VMEM((bq, HEAD_DIM), F32), pltpu.VMEM((bq, 1), F32)],
        compiler_params=_params(3),
        name="stick_breaking_attention",
    )(qkv, qkv, qkv)


def _top_values(s, count, with_rank=False):
    vals = []
    cur = s
    rank = jnp.full(s.shape, float(count), F32) if with_rank else None
    for k in range(count):
        m = jnp.max(cur, axis=0, keepdims=True)
        vals.append(m)
        hit = cur == m
        if with_rank:
            rank = jnp.where(hit, float(k), rank)
        cur = jnp.where(hit, NEG_INF, cur)
    return (vals, rank) if with_rank else vals


def _peer_route_kernel(q_ref, keys_ref, r2_ref, b_ref, c_ref, a_ref):
    def head(hd, carry):
        col = pl.multiple_of(hd * 2 * PEER_HALF, 2 * PEER_HALF)
        q1 = q_ref[:, pl.ds(col, PEER_HALF)].astype(MXU_DTYPE)
        q2 = q_ref[:, pl.ds(col + PEER_HALF, PEER_HALF)].astype(MXU_DTYPE)
        s1 = lax.dot_general(keys_ref[hd, 0], q1, NT_DIMS, preferred_element_type=F32)
        s2 = lax.dot_general(keys_ref[hd, 1], q2, NT_DIMS, preferred_element_type=F32)
        v1 = _top_values(s1, PEER_TOPK)
        v2, rank2 = _top_values(s2, PEER_TOPK, with_rank=True)
        v1_all = jnp.concatenate(v1, axis=0)
        v2_all = jnp.concatenate(v2, axis=0)
        cand = jnp.concatenate(
            [v1[0] + v2_all]
            + [v1[p] + v2_all[:8] for p in range(1, 8)]
            + [v2[0] + v1_all[8:]], axis=0)
        tops = _top_values(cand, PEER_TOPK + 1)
        cut = 0.5 * (tops[PEER_TOPK - 1] + tops[PEER_TOPK])
        top = tops[0]
        z = jnp.sum(jnp.where(cand >= cut, jnp.exp(cand - top), 0.0), axis=0, keepdims=True)
        need = cut - s1
        count = jnp.zeros_like(s1)
        for q in range(PEER_TOPK):
            count = count + jnp.where(v2[q] >= need, 1.0, 0.0)
        a = jnp.where(s1 >= v1[PEER_TOPK - 1], jnp.exp(s1 - v1[0]) / (2.0 * z), 0.0)
        b = jnp.where(s2 >= v2[PEER_TOPK - 1], jnp.exp(s2 - v2[0]), 0.0)
        word_rows = pl.ds(pl.multiple_of(hd * (PEER_N_KEYS // 2), PEER_N_KEYS // 2), PEER_N_KEYS // 2)
        r2_ref[word_rows, :] = pltpu.bitcast(rank2.astype(jnp.bfloat16), jnp.uint32)
        b_ref[word_rows, :] = pltpu.bitcast(b.astype(jnp.bfloat16), jnp.uint32)
        c_ref[hd] = _pair_words(count)
        a_ref[hd] = _pair_words(a)
        return carry

    lax.fori_loop(0, PEER_HEADS, head, 0)


def _peer_route(qp, keys, tr=256):
    t = qp.shape[0]
    shape = (PEER_HEADS, PEER_N_KEYS, t)
    spec = pl.BlockSpec((PEER_HEADS, PEER_N_KEYS, tr), lambda i: (0, 0, i))
    flat = (PEER_HEADS * PEER_N_KEYS // 2, t)
    flat_spec = pl.BlockSpec((PEER_HEADS * PEER_N_KEYS // 2, tr), lambda i: (0, i))
    return pl.pallas_call(
        _peer_route_kernel,
        grid=(t // tr,),
        in_specs=[pl.BlockSpec((tr, qp.shape[1]), lambda i: (i, 0)),
                  pl.BlockSpec(keys.shape, lambda i: (0, 0, 0, 0))],
        out_specs=[flat_spec, flat_spec, spec, spec],
        out_shape=[jax.ShapeDtypeStruct(flat, jnp.uint32), jax.ShapeDtypeStruct(flat, jnp.uint32),
                   jax.ShapeDtypeStruct(shape, jnp.uint32), jax.ShapeDtypeStruct(shape, jnp.uint32)],
        compiler_params=_params(1),
        name="peer_route",
    )(qp, keys)


def _twice_gelu(x):
    return x * (1.0 + lax.erf(x * (2.0 ** -0.5)))


def _pair_words(rows):
    bits = lax.bitcast_convert_type(rows.astype(jnp.bfloat16).astype(F32), jnp.uint32)
    return bits | (bits >> 16)


def _row_tile(words, rows):
    packed = pltpu.bitcast(jnp.broadcast_to(words, (8, 128)), jnp.bfloat16)
    return jnp.concatenate([packed] * (rows // 16), axis=0)


def _peer_kernel(xt_ref, res_ref, u_ref, v_ref, r2_ref, b_ref, c_ref, a_ref, g_ref, bias_ref,
                 o_ref, ob_ref, st_ref, wt_ref, *, n_i, tb, alpha):
    e = pl.program_id(1)

    @pl.when(e == 0)
    def _():
        o_ref[...] = jnp.zeros_like(o_ref)

    st_ref[...] = jnp.dot(u_ref[...], xt_ref[...], preferred_element_type=F32)
    zero = jnp.zeros((PEER_N_KEYS, 128), MXU_DTYPE)
    assert 8 % n_i == 0
    per_group = 8 // n_i
    group = pl.multiple_of((e // per_group) * 8, 8)
    within = e % per_group

    def key_rows(ref, hd, lanes):
        grp = ref[hd, pl.ds(group, 8), lanes]
        rows = grp[:n_i]
        for k in range(1, per_group):
            rows = jnp.where(within == k, grp[k * n_i:(k + 1) * n_i], rows)
        return rows

    for l in range(tb // 128):
        lanes = slice(l * 128, (l + 1) * 128)
        gates = [zero for _ in range(n_i)]
        for hd in range(PEER_HEADS):
            words = slice(hd * (PEER_N_KEYS // 2), (hd + 1) * (PEER_N_KEYS // 2))
            r2 = pltpu.bitcast(r2_ref[words, lanes], jnp.bfloat16)
            b = pltpu.bitcast(b_ref[words, lanes], jnp.bfloat16)
            counts = key_rows(c_ref, hd, lanes)
            a_rows = key_rows(a_ref, hd, lanes)
            for ii in range(n_i):
                count = _row_tile(counts[ii:ii + 1], PEER_N_KEYS)
                a = _row_tile(a_rows[ii:ii + 1], PEER_N_KEYS)
                gates[ii] = gates[ii] + jnp.where(r2 < count, b, zero) * a
        for ii in range(n_i):
            rows = slice(ii * PEER_N_KEYS, (ii + 1) * PEER_N_KEYS)
            act = _twice_gelu(st_ref[rows, lanes]).astype(MXU_DTYPE)
            wt_ref[rows, lanes] = act * gates[ii]
    o_ref[...] += lax.dot_general(wt_ref[...], v_ref[...], TN_DIMS, preferred_element_type=F32)

    @pl.when(e == pl.num_programs(1) - 1)
    def _():
        y = _layer_norm_rows(alpha * res_ref[...] + o_ref[...], g_ref[...], bias_ref[...])
        o_ref[...] = y
        ob_ref[...] = y.astype(ob_ref.dtype)


def _peer_experts(xt, res, u, v, r2, b, c, a, gain, bias, alpha, tb=512, n_i=4):
    t, d = res.shape
    n_exp = u.shape[0]
    eb = n_i * PEER_N_KEYS
    tb = min(tb, t)
    tok = lambda i, e: (i, 0)
    full = pl.BlockSpec((PEER_HEADS, PEER_N_KEYS, tb), lambda i, e: (0, 0, i))
    flat = pl.BlockSpec((PEER_HEADS * PEER_N_KEYS // 2, tb), lambda i, e: (0, i))
    return pl.pallas_call(
        functools.partial(_peer_kernel, n_i=n_i, tb=tb, alpha=alpha),
        grid=(t // tb, n_exp // eb),
        in_specs=[pl.BlockSpec((d, tb), lambda i, e: (0, i)), pl.BlockSpec((tb, d), tok),
                  pl.BlockSpec((eb, d), lambda i, e: (e, 0)),
                  pl.BlockSpec((eb, d), lambda i, e: (e, 0)),
                  flat, flat, full, full,
                  pl.BlockSpec((1, d), lambda i, e: (0, 0)), pl.BlockSpec((1, d), lambda i, e: (0, 0))],
        out_specs=[pl.BlockSpec((tb, d), tok), pl.BlockSpec((tb, d), tok)],
        out_shape=[jax.ShapeDtypeStruct((t, d), F32), jax.ShapeDtypeStruct((t, d), MXU_DTYPE)],
        scratch_shapes=[pltpu.VMEM((eb, tb), F32), pltpu.VMEM((eb, tb), MXU_DTYPE)],
        compiler_params=_params(2),
        name="peer_experts",
    )(xt, res, u, v, r2, b, c, a, gain.reshape(1, d), bias.reshape(1, d))


def _peer_layer(x, xb, w_query, keys, u, v, gain, bias, alpha):
    qp = _matmul(xb, w_query, F32)
    r2, b, c, a = _peer_route(qp, keys)
    return _peer_experts(xb.T, x, u, v, r2, b, c, a, gain, bias, alpha)


def _even_mixer(x2d, batch, seq, w_in, q_norm, w_q_b, kv_norm, w_kv_b, tables):
    d = x2d.shape[1]
    o1 = 3 * A_HEADS * HEAD_DIM
    o3 = o1 + MLA_Q_LORA + MLA_KV_LORA
    w_in_p = jnp.concatenate(
        [w_in, jnp.zeros((d, 4096 - w_in.shape[1]), w_in.dtype)], axis=1).astype(MXU_DTYPE)
    h = _matmul(x2d, w_in_p, F32)
    cos_a, sin_a, cos_b, sin_lo, sin_hi = tables
    qa, ka, cqn, ckvn, kr = _even_prep(h, tables, q_norm, kv_norm, seq)
    out_a = _dilated_attention(qa, ka, h, batch, seq)
    wq = w_q_b.reshape(MLA_Q_LORA, B_HEADS, MLA_NOPE + MLA_ROPE)
    wq = jnp.pad(wq, ((0, 0), (0, 0), (0, MLA_QK_PAD - MLA_NOPE - MLA_ROPE)))
    wq = wq.reshape(MLA_Q_LORA, B_HEADS * MLA_QK_PAD).astype(MXU_DTYPE)
    wkv = w_kv_b.reshape(MLA_KV_LORA, B_HEADS, MLA_NOPE + MLA_V)
    wkv = jnp.concatenate([wkv[:, :, :MLA_NOPE].reshape(MLA_KV_LORA, -1),
                           wkv[:, :, MLA_NOPE:].reshape(MLA_KV_LORA, -1)], axis=1).astype(MXU_DTYPE)
    qf = _matmul(cqn, wq, F32)
    kvf = _matmul(ckvn, wkv, MXU_DTYPE)
    qm, km = _mla_prep(qf, kvf, kr, (cos_b, sin_lo, sin_hi), seq)
    out_b = _mla_attention(qm, km, kvf, batch, seq)
    del o3
    return out_a, out_b


def kernel(x, a_w_in, b_q_norm, b_w_q_b, b_kv_norm, b_w_kv_b, ab_w_out, c_w_in, c_w_out,
           peer_w_query, peer_sub_keys, peer_u, peer_v, ln_gain, ln_bias):
    batch, seq, d = x.shape
    depth = peer_u.shape[0]
    alpha = (2 * depth) ** 0.25
    tables = _rope_tables(seq)
    xf = x.reshape(batch * seq, d)
    xb = None
    for layer in range(depth):
        i = layer // 2
        if layer % 2 == 0:
            src = xf if xb is None else xb
            out_a, out_b = _even_mixer(src, batch, seq, a_w_in[i], b_q_norm[i], b_w_q_b[i],
                                       b_kv_norm[i], b_w_kv_b[i], tables)
            w_out = ab_w_out[i].astype(MXU_DTYPE)
            half = A_HEADS * HEAD_DIM
            xf, xb = _matmul_residual_ln([out_a, out_b], [w_out[:half], w_out[half:]], xf,
                                         ln_gain[layer, 0], ln_bias[layer, 0], alpha)
        else:
            src = xf if xb is None else xb
            n_q = C_HEADS * HEAD_DIM
            w_qkv = jnp.concatenate([c_w_in[i][:, :n_q] * (HEAD_DIM ** -0.5 * LOG2_E),
                                     c_w_in[i][:, n_q:]], axis=1).astype(MXU_DTYPE)
            qkv = _matmul(src, w_qkv, MXU_DTYPE)
            o = _stick_attention(qkv, batch, seq)
            xf, xb = _matmul_residual_ln([o], [c_w_out[i].astype(MXU_DTYPE)], xf,
                                         ln_gain[layer, 0], ln_bias[layer, 0], alpha)
        xf, xb = _peer_layer(xf, xb, peer_w_query[layer].astype(MXU_DTYPE),
                         peer_sub_keys[layer].astype(MXU_DTYPE),
                         peer_u[layer].astype(MXU_DTYPE), peer_v[layer].astype(MXU_DTYPE),
                         ln_gain[layer, 1], ln_bias[layer, 1], alpha)
    return xf.reshape(batch, seq, d)
```

```python
import functools

import jax
import jax.numpy as jnp
from jax import lax
from jax.experimental import pallas as pl
from jax.experimental.pallas import tpu as pltpu

F32 = jnp.float32
MXU_DTYPE = jnp.bfloat16

HEAD_DIM = 128
BLOCK = 128
DIL_STEPS = 128
DILATIONS = (1, 4, 16)
DIL_CHUNK = BLOCK * DILATIONS[-1]
A_HEADS = 8
B_HEADS = 8
MLA_Q_LORA = 512
MLA_KV_LORA = 256
MLA_NOPE = 128
MLA_ROPE = 64
MLA_V = 128
MLA_QK_PAD = 256
C_HEADS = 16
PEER_HEADS = 8
PEER_N_KEYS = 128
PEER_TOPK = 16
PEER_HALF = 128
ROPE_THETA = 10000.0
LN_EPS = 1e-5
RMS_EPS = 1e-6
NEG_INF = -1e30
LOG2_E = 1.4426950408889634
MLA_Q_SCALE = (MLA_NOPE + MLA_ROPE) ** -0.5 * LOG2_E

V7X_VMEM_LIMIT_BYTES = 56 * 1024 * 1024
NT_DIMS = (((1,), (1,)), ((), ()))
TN_DIMS = (((0,), (0,)), ((), ()))


def _params(n_axes):
    return pltpu.CompilerParams(dimension_semantics=("arbitrary",) * n_axes,
                                vmem_limit_bytes=V7X_VMEM_LIMIT_BYTES)


def _mm_kernel(a_ref, b_ref, o_ref):
    a = a_ref[...].astype(MXU_DTYPE)
    o_ref[...] = jnp.dot(a, b_ref[...], preferred_element_type=F32).astype(o_ref.dtype)


def _matmul(a, b, out_dtype, bm=1024, bn=1024):
    m, k = a.shape
    n = b.shape[1]
    bm, bn = min(bm, m), min(bn, n)
    assert m % bm == 0 and n % bn == 0
    return pl.pallas_call(
        _mm_kernel,
        grid=(m // bm, n // bn),
        in_specs=[pl.BlockSpec((bm, k), lambda i, j: (i, 0)),
                  pl.BlockSpec((k, bn), lambda i, j: (0, j))],
        out_specs=pl.BlockSpec((bm, bn), lambda i, j: (i, j)),
        out_shape=jax.ShapeDtypeStruct((m, n), out_dtype),
        compiler_params=_params(2),
        name="matmul",
    )(a, b)


def _layer_norm_rows(y, g, b):
    mu = jnp.mean(y, axis=-1, keepdims=True)
    d = y - mu
    var = jnp.mean(d * d, axis=-1, keepdims=True)
    return d * lax.rsqrt(var + LN_EPS) * g + b


def _mm_ln_kernel(*refs, n_pairs, alpha):
    a_refs = refs[:n_pairs]
    w_refs = refs[n_pairs:2 * n_pairs]
    res_ref, g_ref, b_ref, o_ref, ob_ref = refs[2 * n_pairs:]
    acc = jnp.dot(a_refs[0][...].astype(MXU_DTYPE), w_refs[0][...], preferred_element_type=F32)
    for a_ref, w_ref in zip(a_refs[1:], w_refs[1:]):
        acc = acc + jnp.dot(a_ref[...].astype(MXU_DTYPE), w_ref[...], preferred_element_type=F32)
    y = _layer_norm_rows(alpha * res_ref[...] + acc, g_ref[...], b_ref[...])
    o_ref[...] = y
    ob_ref[...] = y.astype(ob_ref.dtype)


def _matmul_residual_ln(a_list, w_list, res, gain, bias, alpha, bm=256):
    m, n = res.shape
    n_pairs = len(a_list)
    in_specs = [pl.BlockSpec((bm, a.shape[1]), lambda i: (i, 0)) for a in a_list]
    in_specs += [pl.BlockSpec(w.shape, lambda i: (0, 0)) for w in w_list]
    in_specs += [pl.BlockSpec((bm, n), lambda i: (i, 0)),
                 pl.BlockSpec((1, n), lambda i: (0, 0)),
                 pl.BlockSpec((1, n), lambda i: (0, 0))]
    return pl.pallas_call(
        functools.partial(_mm_ln_kernel, n_pairs=n_pairs, alpha=alpha),
        grid=(m // bm,),
        in_specs=in_specs,
        out_specs=[pl.BlockSpec((bm, n), lambda i: (i, 0)), pl.BlockSpec((bm, n), lambda i: (i, 0))],
        out_shape=[jax.ShapeDtypeStruct((m, n), F32), jax.ShapeDtypeStruct((m, n), MXU_DTYPE)],
        compiler_params=_params(1),
        name="matmul_residual_ln",
    )(*a_list, *w_list, res, gain.reshape(1, n), bias.reshape(1, n))


def _rope_tables(seq):
    pos = jnp.arange(seq, dtype=F32)[:, None]

    def cos_sin(half):
        inv = ROPE_THETA ** (-jnp.arange(half, dtype=F32) / half)
        ang = pos * inv[None, :]
        return jnp.cos(ang), jnp.sin(ang)

    c, s = cos_sin(HEAD_DIM // 2)
    cos_a = jnp.concatenate([c, c], axis=1)
    sin_a = jnp.concatenate([-s, s], axis=1)
    c, s = cos_sin(MLA_ROPE // 2)
    z32 = jnp.zeros_like(c)
    cos_b = jnp.concatenate([c, c, z32, z32], axis=1)
    sin_lo = jnp.concatenate([-s, z32, z32, z32], axis=1)
    sin_hi = jnp.concatenate([z32, s, z32, z32], axis=1)
    return cos_a, sin_a, cos_b, sin_lo, sin_hi


def _rope128(x, cos_a, sin_a):
    return x * cos_a + pltpu.roll(x, HEAD_DIM // 2, 1) * sin_a


def _rope64(x, cos_b, sin_lo, sin_hi):
    return x * cos_b + pltpu.roll(x, 96, 1) * sin_lo + pltpu.roll(x, 32, 1) * sin_hi


def _rms_norm_rows(x, g):
    return x * lax.rsqrt(jnp.mean(x * x, axis=-1, keepdims=True) + RMS_EPS) * g


def _even_prep_kernel(qk_ref, tail_ref, cos_a_ref, sin_a_ref, cos_b_ref, sin_lo_ref, sin_hi_ref,
                      qn_ref, kvn_ref, qa_ref, ka_ref, cq_ref, ckv_ref, kr_ref):
    cos_a, sin_a = cos_a_ref[...], sin_a_ref[...]
    n_qk = A_HEADS * HEAD_DIM
    for hd in range(A_HEADS):
        lo = hd * HEAD_DIM
        qa_ref[:, lo:lo + HEAD_DIM] = _rope128(qk_ref[:, lo:lo + HEAD_DIM], cos_a, sin_a)
        ka_ref[:, lo:lo + HEAD_DIM] = _rope128(qk_ref[:, n_qk + lo:n_qk + lo + HEAD_DIM], cos_a, sin_a)
    cq_ref[...] = _rms_norm_rows(tail_ref[:, :MLA_Q_LORA], qn_ref[...]).astype(cq_ref.dtype)
    o2 = MLA_Q_LORA + MLA_KV_LORA
    ckv_ref[...] = _rms_norm_rows(tail_ref[:, MLA_Q_LORA:o2], kvn_ref[...]).astype(ckv_ref.dtype)
    kr = _rope64(tail_ref[:, o2:o2 + 128], cos_b_ref[...], sin_lo_ref[...], sin_hi_ref[...])
    kr_ref[...] = kr.astype(kr_ref.dtype)


def _even_prep(h, tables, q_norm, kv_norm, seq, rows=256):
    t = h.shape[0]
    n_qk = A_HEADS * HEAD_DIM
    sb = seq // rows
    tab_spec = pl.BlockSpec((rows, 128), lambda i: (i % sb, 0))
    return pl.pallas_call(
        _even_prep_kernel,
        grid=(t // rows,),
        in_specs=[pl.BlockSpec((rows, 2 * n_qk), lambda i: (i, 0)),
                  pl.BlockSpec((rows, 1024), lambda i: (i, 3)),
                  tab_spec, tab_spec, tab_spec, tab_spec, tab_spec,
                  pl.BlockSpec((1, MLA_Q_LORA), lambda i: (0, 0)),
                  pl.BlockSpec((1, MLA_KV_LORA), lambda i: (0, 0))],
        out_specs=[pl.BlockSpec((rows, n_qk), lambda i: (i, 0)),
                   pl.BlockSpec((rows, n_qk), lambda i: (i, 0)),
                   pl.BlockSpec((rows, MLA_Q_LORA), lambda i: (i, 0)),
                   pl.BlockSpec((rows, MLA_KV_LORA), lambda i: (i, 0)),
                   pl.BlockSpec((rows, 128), lambda i: (i, 0))],
        out_shape=[jax.ShapeDtypeStruct((t, n_qk), F32),
                   jax.ShapeDtypeStruct((t, n_qk), F32),
                   jax.ShapeDtypeStruct((t, MLA_Q_LORA), MXU_DTYPE),
                   jax.ShapeDtypeStruct((t, MLA_KV_LORA), MXU_DTYPE),
                   jax.ShapeDtypeStruct((t, 128), MXU_DTYPE)],
        compiler_params=_params(1),
        name="even_prep",
    )(h, h, *tables, q_norm.reshape(1, -1), kv_norm.reshape(1, -1))


def _mla_prep_kernel(q_ref, kn_ref, kr_ref, cos_b_ref, sin_lo_ref, sin_hi_ref, qm_ref, km_ref):
    cos_b, sin_lo, sin_hi = cos_b_ref[...], sin_lo_ref[...], sin_hi_ref[...]
    kr = kr_ref[...]
    for hd in range(B_HEADS):
        lo = hd * MLA_QK_PAD
        qm_ref[:, lo:lo + MLA_NOPE] = (q_ref[:, lo:lo + MLA_NOPE] * MLA_Q_SCALE).astype(qm_ref.dtype)
        q_rope = _rope64(q_ref[:, lo + MLA_NOPE:lo + MLA_QK_PAD], cos_b, sin_lo, sin_hi)
        qm_ref[:, lo + MLA_NOPE:lo + MLA_QK_PAD] = (q_rope * MLA_Q_SCALE).astype(qm_ref.dtype)
        km_ref[:, lo:lo + MLA_NOPE] = kn_ref[:, hd * MLA_NOPE:(hd + 1) * MLA_NOPE]
        km_ref[:, lo + MLA_NOPE:lo + MLA_QK_PAD] = kr


def _mla_prep(qf, kvf, kr, tables, seq, rows=256):
    t = qf.shape[0]
    w = B_HEADS * MLA_QK_PAD
    sb = seq // rows
    tab_spec = pl.BlockSpec((rows, 128), lambda i: (i % sb, 0))
    return pl.pallas_call(
        _mla_prep_kernel,
        grid=(t // rows,),
        in_specs=[pl.BlockSpec((rows, w), lambda i: (i, 0)),
                  pl.BlockSpec((rows, B_HEADS * MLA_NOPE), lambda i: (i, 0)),
                  pl.BlockSpec((rows, 128), lambda i: (i, 0)),
                  tab_spec, tab_spec, tab_spec],
        out_specs=[pl.BlockSpec((rows, w), lambda i: (i, 0)), pl.BlockSpec((rows, w), lambda i: (i, 0))],
        out_shape=[jax.ShapeDtypeStruct((t, w), MXU_DTYPE), jax.ShapeDtypeStruct((t, w), MXU_DTYPE)],
        compiler_params=_params(1),
        name="mla_prep",
    )(qf, kvf, kr, *tables)


def _dilated_kernel(q_ref, kc_ref, kp_ref, vc_ref, vp_ref, o_ref,
                    kbuf, vbuf, o0, o1, o2, l0, l1, l2):
    c = pl.program_id(2)
    kbuf[:DIL_CHUNK, :] = kp_ref[...]
    kbuf[DIL_CHUNK:, :] = kc_ref[...]
    vbuf[:DIL_CHUNK, :] = vp_ref[...]
    vbuf[DIL_CHUNK:, :] = vc_ref[...]
    scale = HEAD_DIM ** -0.5
    row = lax.broadcasted_iota(jnp.int32, (BLOCK, 2 * BLOCK), 0)
    col = lax.broadcasted_iota(jnp.int32, (BLOCK, 2 * BLOCK), 1)
    band = (col >= row) & (col <= row + DIL_STEPS)

    def blocks(specs):
        staged = []
        for q_start, k_start, dil, first, o_g, l_g in specs:
            if dil == 1:
                q_idx, k_idx = pl.ds(q_start, BLOCK), pl.ds(k_start, 2 * BLOCK)
            else:
                q_idx = pl.ds(q_start, BLOCK, stride=dil)
                k_idx = pl.ds(k_start, 2 * BLOCK, stride=dil)
            q = q_ref[q_idx, :].astype(MXU_DTYPE)
            k = kbuf[k_idx, :].astype(MXU_DTYPE)
            v = vbuf[k_idx, :].astype(MXU_DTYPE)
            logits = lax.dot_general(q, k, NT_DIMS, preferred_element_type=F32) * scale
            staged.append((logits, v, q_idx, first, o_g, l_g))
        weighted = []
        for logits, v, q_idx, first, o_g, l_g in staged:
            valid_from = jnp.where(first, BLOCK, 0)
            logits = jnp.where(band & (col >= valid_from), logits, NEG_INF)
            m = jnp.max(logits, axis=-1, keepdims=True)
            p = jnp.exp(logits - m)
            denom = jnp.sum(p, axis=-1, keepdims=True)
            weighted.append(((p / denom).astype(MXU_DTYPE), m + jnp.log(denom)))
        for (logits, v, q_idx, first, o_g, l_g), (pn, lse) in zip(staged, weighted):
            o_g[q_idx, :] = jnp.dot(pn, v, preferred_element_type=F32)
            l_g[q_idx, :] = jnp.broadcast_to(lse, (BLOCK, HEAD_DIM))

    n_blocks = DIL_CHUNK // BLOCK
    per_trip = 2

    def body(j, carry):
        specs = []
        for u in range(per_trip):
            i = j * per_trip + u
            specs.append((pl.multiple_of(i * BLOCK, BLOCK),
                          pl.multiple_of(DIL_CHUNK + (i - 1) * BLOCK, BLOCK),
                          DILATIONS[0], (c == 0) & (i == 0), o0, l0))
            d1 = DILATIONS[1]
            n, r = i // d1, i % d1
            span = BLOCK * d1
            specs.append((n * span + r, DIL_CHUNK + (n - 1) * span + r, d1, (c == 0) & (n == 0), o1, l1))
            specs.append((i, DIL_CHUNK - BLOCK * DILATIONS[2] + i, DILATIONS[2], c == 0, o2, l2))
        blocks(specs)
        return carry

    lax.fori_loop(0, n_blocks // per_trip, body, 0)

    la, lb, lc = l0[...], l1[...], l2[...]
    mx = jnp.maximum(jnp.maximum(la, lb), lc)
    ea, eb, ec = jnp.exp(la - mx), jnp.exp(lb - mx), jnp.exp(lc - mx)
    merged = (ea * o0[...] + eb * o1[...] + ec * o2[...]) / (ea + eb + ec)
    o_ref[...] = merged.astype(o_ref.dtype)


def _dilated_attention(qa, ka, h, batch, seq):
    t = qa.shape[0]
    nc = seq // DIL_CHUNK
    v_col0 = 2 * A_HEADS
    cur = lambda b, hd, c: (b * nc + c, hd)
    prev = lambda b, hd, c: (b * nc + jnp.maximum(c - 1, 0), hd)
    blk = (DIL_CHUNK, HEAD_DIM)
    return pl.pallas_call(
        _dilated_kernel,
        grid=(batch, A_HEADS, nc),
        in_specs=[pl.BlockSpec(blk, cur),
                  pl.BlockSpec(blk, cur), pl.BlockSpec(blk, prev),
                  pl.BlockSpec(blk, lambda b, hd, c: (b * nc + c, v_col0 + hd)),
                  pl.BlockSpec(blk, lambda b, hd, c: (b * nc + jnp.maximum(c - 1, 0), v_col0 + hd))],
        out_specs=pl.BlockSpec(blk, cur),
        out_shape=jax.ShapeDtypeStruct((t, A_HEADS * HEAD_DIM), MXU_DTYPE),
        scratch_shapes=[pltpu.VMEM((2 * DIL_CHUNK, HEAD_DIM), F32)] * 2
                       + [pltpu.VMEM(blk, F32)] * 6,
        compiler_params=_params(3),
        name="dilated_attention",
    )(qa, ka, ka, h, h)


def _mla_kernel(q_ref, k_ref, v_ref, o_ref, m_ref, l_ref, acc_ref, *, bq, bk, unroll):
    qi = pl.program_id(2)
    n_sub = bq // bk
    m_ref[...] = jnp.full_like(m_ref, NEG_INF)
    l_ref[...] = jnp.zeros_like(l_ref)
    acc_ref[...] = jnp.zeros_like(acc_ref)

    def group(k_starts, r0, diagonal):
        cols = bq - r0
        q = q_ref[r0:, :]
        ss = [lax.dot_general(k_ref[pl.ds(ks, bk), :], q, NT_DIMS, preferred_element_type=F32)
              for ks in k_starts]
        if diagonal:
            key = lax.broadcasted_iota(jnp.int32, (bk, cols), 0)
            qry = lax.broadcasted_iota(jnp.int32, (bk, cols), 1)
            ss = [jnp.where(key <= qry, s, NEG_INF) for s in ss]
        m_prev = m_ref[:, r0:]
        m_new = m_prev
        for s in ss:
            m_new = jnp.maximum(m_new, jnp.max(s, axis=0, keepdims=True))
        alpha = jnp.exp2(m_prev - m_new)
        l = alpha * l_ref[:, r0:]
        acc = alpha * acc_ref[:, r0:]
        for ks, s in zip(k_starts, ss):
            p = jnp.exp2(s - m_new)
            l = l + jnp.sum(p, axis=0, keepdims=True)
            acc = acc + lax.dot_general(v_ref[pl.ds(ks, bk), :], p.astype(MXU_DTYPE), TN_DIMS,
                                        preferred_element_type=F32)
        m_ref[:, r0:] = m_new
        l_ref[:, r0:] = l
        acc_ref[:, r0:] = acc

    def body(j, carry):
        group([pl.multiple_of((j * unroll + u) * bk, bk) for u in range(unroll)], 0, False)
        return carry

    assert n_sub % unroll == 0
    lax.fori_loop(0, qi * (n_sub // unroll), body, 0)
    for jj in range(n_sub):
        group([pl.multiple_of(qi * bq + jj * bk, bk)], jj * bk, True)
    o_ref[...] = (acc_ref[...] / l_ref[...]).T.astype(o_ref.dtype)


def _mla_attention(qm, km, kvf, batch, seq, bq=1024, bk=512, unroll=2):
    t = qm.shape[0]
    bq, bk = min(bq, seq), min(bk, seq)
    nq = seq // bq
    return pl.pallas_call(
        functools.partial(_mla_kernel, bq=bq, bk=bk, unroll=min(unroll, bq // bk)),
        grid=(batch, B_HEADS, nq),
        in_specs=[pl.BlockSpec((bq, MLA_QK_PAD), lambda b, hd, i: (b * nq + i, hd)),
                  pl.BlockSpec((seq, MLA_QK_PAD), lambda b, hd, i: (b, hd)),
                  pl.BlockSpec((seq, MLA_V), lambda b, hd, i: (b, B_HEADS + hd))],
        out_specs=pl.BlockSpec((bq, MLA_V), lambda b, hd, i: (b * nq + i, hd)),
        out_shape=jax.ShapeDtypeStruct((t, B_HEADS * MLA_V), MXU_DTYPE),
        scratch_shapes=[pltpu.VMEM((1, bq), F32), pltpu.VMEM((1, bq), F32),
                        pltpu.VMEM((MLA_V, bq), F32)],
        compiler_params=_params(3),
        name="mla_attention",
    )(qm, km, kvf)


def _stick_kernel(q_ref, k_ref, v_ref, o_ref, acc_ref, run_ref, *, bq, bk, unroll):
    qi = pl.program_id(2)
    n_sub = bq // bk
    tri_r = lax.broadcasted_iota(jnp.int32, (bk, bk), 0)
    tri_c = lax.broadcasted_iota(jnp.int32, (bk, bk), 1)
    at_or_after = jnp.where(tri_r >= tri_c, 1.0, 0.0).astype(MXU_DTYPE)
    acc_ref[...] = jnp.zeros_like(acc_ref)
    run_ref[...] = jnp.zeros_like(run_ref)

    def group(tiles):
        staged = []
        for ks, r0, diagonal in tiles:
            rows = bq - r0
            z = lax.dot_general(q_ref[r0:, :], k_ref[pl.ds(ks, bk), :], NT_DIMS,
                                preferred_element_type=F32)
            neg_abs = lax.bitcast_convert_type(
                lax.bitcast_convert_type(z, jnp.uint32) | jnp.uint32(0x80000000), F32)
            sp = jnp.maximum(z, 0.0) + jnp.log(1.0 + jnp.exp2(neg_abs)) * LOG2_E
            mask = None
            if diagonal:
                row = lax.broadcasted_iota(jnp.int32, (rows, bk), 0)
                col = lax.broadcasted_iota(jnp.int32, (rows, bk), 1)
                mask = col < row
                sp = jnp.where(mask, sp, 0.0)
            staged.append((z, sp, mask))
        sufs = []
        for _, sp, _ in staged:
            hi = sp.astype(MXU_DTYPE)
            lo = (sp - hi.astype(F32)).astype(MXU_DTYPE)
            sufs.append(jnp.dot(hi, at_or_after, preferred_element_type=F32)
                        + jnp.dot(lo, at_or_after, preferred_element_type=F32))
        run = run_ref[...]
        acc = acc_ref[...]
        for (ks, r0, _), (z, _, mask), suf in zip(tiles, staged, sufs):
            att = jnp.exp2(z - suf - run[r0:])
            if mask is not None:
                att = jnp.where(mask, att, 0.0)
            part = jnp.dot(att.astype(MXU_DTYPE), v_ref[pl.ds(ks, bk), :], preferred_element_type=F32)
            total = suf[:, :1]
            if r0 == 0:
                acc, run = acc + part, run + total
            else:
                acc = jnp.concatenate([acc[:r0], acc[r0:] + part], axis=0)
                run = jnp.concatenate([run[:r0], run[r0:] + total], axis=0)
        acc_ref[...] = acc
        run_ref[...] = run

    group([(pl.multiple_of(qi * bq + jj * bk, bk), jj * bk, True) for jj in reversed(range(n_sub))])

    def body(j, carry):
        group([(pl.multiple_of((qi * n_sub - 1 - j * unroll - u) * bk, bk), 0, False)
               for u in range(unroll)])
        return carry

    assert n_sub % unroll == 0
    lax.fori_loop(0, qi * (n_sub // unroll), body, 0)
    o_ref[...] = acc_ref[...].astype(o_ref.dtype)


def _stick_attention(qkv, batch, seq, bq=1024, bk=256, unroll=4):
    t = qkv.shape[0]
    bq, bk = min(bq, seq), min(bk, seq)
    nq = seq // bq
    return pl.pallas_call(
        functools.partial(_stick_kernel, bq=bq, bk=bk, unroll=min(unroll, bq // bk)),
        grid=(batch, C_HEADS, nq),
        in_specs=[pl.BlockSpec((bq, HEAD_DIM), lambda b, hd, i: (b * nq + i, hd)),
                  pl.BlockSpec((seq, HEAD_DIM), lambda b, hd, i: (b, C_HEADS + hd)),
                  pl.BlockSpec((seq, HEAD_DIM), lambda b, hd, i: (b, 2 * C_HEADS + hd))],
        out_specs=pl.BlockSpec((bq, HEAD_DIM), lambda b, hd, i: (b * nq + i, hd)),
        out_shape=jax.ShapeDtypeStruct((t, C_HEADS * HEAD_DIM), MXU_DTYPE),
        scratch_shapes=[pltpu.VMEM((bq, HEAD_DIM), F32), pltpu.VMEM((bq, 1), F32)],
        compiler_params=_params(3),
        name="stick_breaking_attention",
    )(qkv, qkv, qkv)


def _top_values(s, count, with_rank=False):
    vals = []
    cur = s
    rank = jnp.full(s.shape, float(count), F32) if with_rank else None
    for k in range(count):
        m = jnp.max(cur, axis=0, keepdims=True)
        vals.append(m)
        hit = cur == m
        if with_rank:
            rank = jnp.where(hit, float(k), rank)
        cur = jnp.where(hit, NEG_INF, cur)
    return (vals, rank) if with_rank else vals


def _ranked(s, count):
    n = s.shape[0]
    rows = lax.broadcasted_iota(jnp.int32, s.shape, 0)
    vals = []
    cur = s
    rank = jnp.full(s.shape, float(count), F32)
    for k in range(count):
        m = jnp.max(cur, axis=0, keepdims=True)
        first = jnp.min(jnp.where(cur == m, rows, n), axis=0, keepdims=True)
        pick = rows == first
        vals.append(m)
        rank = jnp.where(pick, float(k), rank)
        cur = jnp.where(pick, NEG_INF, cur)
    return vals, rank


def _pair_sums(v1, v2):
    v1_all = jnp.concatenate(v1, axis=0)
    v2_all = jnp.concatenate(v2, axis=0)
    return jnp.concatenate([v1[0] + v2_all]
                           + [v1[p] + v2_all[:8] for p in range(1, 8)]
                           + [v2[0] + v1_all[8:]], axis=0)


def _count_at_least(s, threshold):
    return jnp.sum(jnp.where(s >= threshold, 1.0, 0.0), axis=0, keepdims=True)


def _route_distinct(s1, s2):
    v1 = _top_values(s1, PEER_TOPK)
    v2, rank2 = _top_values(s2, PEER_TOPK, with_rank=True)
    cand = _pair_sums(v1, v2)
    tops = _top_values(cand, PEER_TOPK + 1)
    cut = 0.5 * (tops[PEER_TOPK - 1] + tops[PEER_TOPK])
    top = tops[0]
    z = jnp.sum(jnp.where(cand >= cut, jnp.exp(cand - top), 0.0), axis=0, keepdims=True)
    need = cut - s1
    count = jnp.zeros_like(s1)
    for q in range(PEER_TOPK):
        count = count + jnp.where(v2[q] >= need, 1.0, 0.0)
    a = jnp.where(s1 >= v1[PEER_TOPK - 1], jnp.exp(s1 - v1[0]) / (2.0 * z), 0.0)
    b = jnp.where(s2 >= v2[PEER_TOPK - 1], jnp.exp(s2 - v2[0]), 0.0)
    distinct = ((_count_at_least(s1, v1[PEER_TOPK - 1]) == PEER_TOPK)
                & (_count_at_least(s2, v2[PEER_TOPK - 1]) == PEER_TOPK)
                & (_count_at_least(cand, tops[PEER_TOPK]) == PEER_TOPK + 1))
    return rank2, b, count, a, distinct


def _route_with_ties(s1, s2):
    v1, rank1 = _ranked(s1, PEER_TOPK)
    v2, rank2 = _ranked(s2, PEER_TOPK)
    cand = _pair_sums(v1, v2)
    _, cand_rank = _ranked(cand, PEER_TOPK)
    chosen = cand_rank < PEER_TOPK
    top = v1[0] + v2[0]
    z = jnp.sum(jnp.where(chosen, jnp.exp(cand - top), 0.0), axis=0, keepdims=True)
    picked = jnp.where(chosen, 1.0, 0.0)
    per_p = [jnp.sum(picked[:16], axis=0, keepdims=True)]
    per_p += [jnp.sum(picked[8 + 8 * p:16 + 8 * p], axis=0, keepdims=True) for p in range(1, 8)]
    per_p += [picked[64 + p:65 + p] for p in range(8, 16)]
    count = jnp.zeros_like(s1)
    for p in range(PEER_TOPK):
        count = count + jnp.where(rank1 == p, per_p[p], 0.0)
    a = jnp.where(rank1 < PEER_TOPK, jnp.exp(s1 - v1[0]) / (2.0 * z), 0.0)
    b = jnp.where(rank2 < PEER_TOPK, jnp.exp(s2 - v2[0]), 0.0)
    return rank2, b, count, a


def _peer_route_kernel(q_ref, keys_ref, r2_ref, b_ref, c_ref, a_ref):
    def head(hd, carry):
        col = pl.multiple_of(hd * 2 * PEER_HALF, 2 * PEER_HALF)
        q1 = q_ref[:, pl.ds(col, PEER_HALF)].astype(MXU_DTYPE)
        q2 = q_ref[:, pl.ds(col + PEER_HALF, PEER_HALF)].astype(MXU_DTYPE)
        s1 = lax.dot_general(keys_ref[hd, 0], q1, NT_DIMS, preferred_element_type=F32)
        s2 = lax.dot_general(keys_ref[hd, 1], q2, NT_DIMS, preferred_element_type=F32)

        def store(rank2, b, count, a):
            word_rows = pl.ds(pl.multiple_of(hd * (PEER_N_KEYS // 2), PEER_N_KEYS // 2), PEER_N_KEYS // 2)
            r2_ref[word_rows, :] = pltpu.bitcast(rank2.astype(jnp.bfloat16), jnp.uint32)
            b_ref[word_rows, :] = pltpu.bitcast(b.astype(jnp.bfloat16), jnp.uint32)
            c_ref[hd] = _pair_words(count)
            a_ref[hd] = _pair_words(a)

        rank2, b, count, a, distinct = _route_distinct(s1, s2)
        store(rank2, b, count, a)

        @pl.when(jnp.max(jnp.where(distinct, 0.0, 1.0)) > 0.0)
        def _():
            store(*_route_with_ties(s1, s2))

        return carry

    lax.fori_loop(0, PEER_HEADS, head, 0)


def _peer_route(qp, keys, tr=256):
    t = qp.shape[0]
    shape = (PEER_HEADS, PEER_N_KEYS, t)
    spec = pl.BlockSpec((PEER_HEADS, PEER_N_KEYS, tr), lambda i: (0, 0, i))
    flat = (PEER_HEADS * PEER_N_KEYS // 2, t)
    flat_spec = pl.BlockSpec((PEER_HEADS * PEER_N_KEYS // 2, tr), lambda i: (0, i))
    return pl.pallas_call(
        _peer_route_kernel,
        grid=(t // tr,),
        in_specs=[pl.BlockSpec((tr, qp.shape[1]), lambda i: (i, 0)),
                  pl.BlockSpec(keys.shape, lambda i: (0, 0, 0, 0))],
        out_specs=[flat_spec, flat_spec, spec, spec],
        out_shape=[jax.ShapeDtypeStruct(flat, jnp.uint32), jax.ShapeDtypeStruct(flat, jnp.uint32),
                   jax.ShapeDtypeStruct(shape, jnp.uint32), jax.ShapeDtypeStruct(shape, jnp.uint32)],
        compiler_params=_params(1),
        name="peer_route",
    )(qp, keys)


def _twice_gelu(x):
    return x * (1.0 + lax.erf(x * (2.0 ** -0.5)))


def _pair_words(rows):
    bits = lax.bitcast_convert_type(rows.astype(jnp.bfloat16).astype(F32), jnp.uint32)
    return bits | (bits >> 16)


def _row_tile(words, rows):
    packed = pltpu.bitcast(jnp.broadcast_to(words, (8, 128)), jnp.bfloat16)
    return jnp.concatenate([packed] * (rows // 16), axis=0)


def _peer_kernel(xt_ref, res_ref, u_ref, v_ref, r2_ref, b_ref, c_ref, a_ref, g_ref, bias_ref,
                 o_ref, ob_ref, st_ref, wt_ref, *, n_i, tb, alpha):
    e = pl.program_id(1)

    @pl.when(e == 0)
    def _():
        o_ref[...] = jnp.zeros_like(o_ref)

    st_ref[...] = jnp.dot(u_ref[...], xt_ref[...], preferred_element_type=F32)
    zero = jnp.zeros((PEER_N_KEYS, 128), MXU_DTYPE)
    assert 8 % n_i == 0
    per_group = 8 // n_i
    group = pl.multiple_of((e // per_group) * 8, 8)
    within = e % per_group

    def key_rows(ref, hd, lanes):
        grp = ref[hd, pl.ds(group, 8), lanes]
        rows = grp[:n_i]
        for k in range(1, per_group):
            rows = jnp.where(within == k, grp[k * n_i:(k + 1) * n_i], rows)
        return rows

    for l in range(tb // 128):
        lanes = slice(l * 128, (l + 1) * 128)
        gates = [zero for _ in range(n_i)]
        for hd in range(PEER_HEADS):
            words = slice(hd * (PEER_N_KEYS // 2), (hd + 1) * (PEER_N_KEYS // 2))
            r2 = pltpu.bitcast(r2_ref[words, lanes], jnp.bfloat16)
            b = pltpu.bitcast(b_ref[words, lanes], jnp.bfloat16)
            counts = key_rows(c_ref, hd, lanes)
            a_rows = key_rows(a_ref, hd, lanes)
            for ii in range(n_i):
                count = _row_tile(counts[ii:ii + 1], PEER_N_KEYS)
                a = _row_tile(a_rows[ii:ii + 1], PEER_N_KEYS)
                gates[ii] = gates[ii] + jnp.where(r2 < count, b, zero) * a
        for ii in range(n_i):
            rows = slice(ii * PEER_N_KEYS, (ii + 1) * PEER_N_KEYS)
            act = _twice_gelu(st_ref[rows, lanes]).astype(MXU_DTYPE)
            wt_ref[rows, lanes] = act * gates[ii]
    o_ref[...] += lax.dot_general(wt_ref[...], v_ref[...], TN_DIMS, preferred_element_type=F32)

    @pl.when(e == pl.num_programs(1) - 1)
    def _():
        y = _layer_norm_rows(alpha * res_ref[...] + o_ref[...], g_ref[...], bias_ref[...])
        o_ref[...] = y
        ob_ref[...] = y.astype(ob_ref.dtype)


def _peer_experts(xt, res, u, v, r2, b, c, a, gain, bias, alpha, tb=512, n_i=4):
    t, d = res.shape
    n_exp = u.shape[0]
    eb = n_i * PEER_N_KEYS
    tb = min(tb, t)
    tok = lambda i, e: (i, 0)
    full = pl.BlockSpec((PEER_HEADS, PEER_N_KEYS, tb), lambda i, e: (0, 0, i))
    flat = pl.BlockSpec((PEER_HEADS * PEER_N_KEYS // 2, tb), lambda i, e: (0, i))
    return pl.pallas_call(
        functools.partial(_peer_kernel, n_i=n_i, tb=tb, alpha=alpha),
        grid=(t // tb, n_exp // eb),
        in_specs=[pl.BlockSpec((d, tb), lambda i, e: (0, i)), pl.BlockSpec((tb, d), tok),
                  pl.BlockSpec((eb, d), lambda i, e: (e, 0)),
                  pl.BlockSpec((eb, d), lambda i, e: (e, 0)),
                  flat, flat, full, full,
                  pl.BlockSpec((1, d), lambda i, e: (0, 0)), pl.BlockSpec((1, d), lambda i, e: (0, 0))],
        out_specs=[pl.BlockSpec((tb, d), tok), pl.BlockSpec((tb, d), tok)],
        out_shape=[jax.ShapeDtypeStruct((t, d), F32), jax.ShapeDtypeStruct((t, d), MXU_DTYPE)],
        scratch_shapes=[pltpu.VMEM((eb, tb), F32), pltpu.VMEM((eb, tb), MXU_DTYPE)],
        compiler_params=_params(2),
        name="peer_experts",
    )(xt, res, u, v, r2, b, c, a, gain.reshape(1, d), bias.reshape(1, d))


def _peer_layer(x, xb, w_query, keys, u, v, gain, bias, alpha):
    qp = _matmul(xb, w_query, F32)
    r2, b, c, a = _peer_route(qp, keys)
    return _peer_experts(xb.T, x, u, v, r2, b, c, a, gain, bias, alpha)


def _even_mixer(x2d, batch, seq, w_in, q_norm, w_q_b, kv_norm, w_kv_b, tables):
    d = x2d.shape[1]
    o1 = 3 * A_HEADS * HEAD_DIM
    o3 = o1 + MLA_Q_LORA + MLA_KV_LORA
    w_in_p = jnp.concatenate(
        [w_in, jnp.zeros((d, 4096 - w_in.shape[1]), w_in.dtype)], axis=1).astype(MXU_DTYPE)
    h = _matmul(x2d, w_in_p, F32)
    cos_a, sin_a, cos_b, sin_lo, sin_hi = tables
    qa, ka, cqn, ckvn, kr = _even_prep(h, tables, q_norm, kv_norm, seq)
    out_a = _dilated_attention(qa, ka, h, batch, seq)
    wq = w_q_b.reshape(MLA_Q_LORA, B_HEADS, MLA_NOPE + MLA_ROPE)
    wq = jnp.pad(wq, ((0, 0), (0, 0), (0, MLA_QK_PAD - MLA_NOPE - MLA_ROPE)))
    wq = wq.reshape(MLA_Q_LORA, B_HEADS * MLA_QK_PAD).astype(MXU_DTYPE)
    wkv = w_kv_b.reshape(MLA_KV_LORA, B_HEADS, MLA_NOPE + MLA_V)
    wkv = jnp.concatenate([wkv[:, :, :MLA_NOPE].reshape(MLA_KV_LORA, -1),
                           wkv[:, :, MLA_NOPE:].reshape(MLA_KV_LORA, -1)], axis=1).astype(MXU_DTYPE)
    qf = _matmul(cqn, wq, F32)
    kvf = _matmul(ckvn, wkv, MXU_DTYPE)
    qm, km = _mla_prep(qf, kvf, kr, (cos_b, sin_lo, sin_hi), seq)
    out_b = _mla_attention(qm, km, kvf, batch, seq)
    del o3
    return out_a, out_b


def kernel(x, a_w_in, b_q_norm, b_w_q_b, b_kv_norm, b_w_kv_b, ab_w_out, c_w_in, c_w_out,
           peer_w_query, peer_sub_keys, peer_u, peer_v, ln_gain, ln_bias):
    batch, seq, d = x.shape
    depth = peer_u.shape[0]
    alpha = (2 * depth) ** 0.25
    tables = _rope_tables(seq)
    xf = x.reshape(batch * seq, d)
    xb = None
    for layer in range(depth):
        i = layer // 2
        if layer % 2 == 0:
            src = xf if xb is None else xb
            out_a, out_b = _even_mixer(src, batch, seq, a_w_in[i], b_q_norm[i], b_w_q_b[i],
                                       b_kv_norm[i], b_w_kv_b[i], tables)
            w_out = ab_w_out[i].astype(MXU_DTYPE)
            half = A_HEADS * HEAD_DIM
            xf, xb = _matmul_residual_ln([out_a, out_b], [w_out[:half], w_out[half:]], xf,
                                         ln_gain[layer, 0], ln_bias[layer, 0], alpha)
        else:
            src = xf if xb is None else xb
            n_q = C_HEADS * HEAD_DIM
            w_qkv = jnp.concatenate([c_w_in[i][:, :n_q] * (HEAD_DIM ** -0.5 * LOG2_E),
                                     c_w_in[i][:, n_q:]], axis=1).astype(MXU_DTYPE)
            qkv = _matmul(src, w_qkv, MXU_DTYPE)
            o = _stick_attention(qkv, batch, seq)
            xf, xb = _matmul_residual_ln([o], [c_w_out[i].astype(MXU_DTYPE)], xf,
                                         ln_gain[layer, 0], ln_bias[layer, 0], alpha)
        xf, xb = _peer_layer(xf, xb, peer_w_query[layer].astype(MXU_DTYPE),
                         peer_sub_keys[layer].astype(MXU_DTYPE),
                         peer_u[layer].astype(MXU_DTYPE), peer_v[layer].astype(MXU_DTYPE),
                         ln_gain[layer, 1], ln_bias[layer, 1], alpha)
    return xf.reshape(batch, seq, d)
```

```python
import functools

import jax
import jax.numpy as jnp
from jax import lax
from jax.experimental import pallas as pl
from jax.experimental.pallas import tpu as pltpu

F32 = jnp.float32
MXU_DTYPE = jnp.bfloat16

HEAD_DIM = 128
BLOCK = 128
DIL_STEPS = 128
DILATIONS = (1, 4, 16)
DIL_CHUNK = BLOCK * DILATIONS[-1]
A_HEADS = 8
B_HEADS = 8
MLA_Q_LORA = 512
MLA_KV_LORA = 256
MLA_NOPE = 128
MLA_ROPE = 64
MLA_V = 128
MLA_QK_PAD = 256
C_HEADS = 16
PEER_HEADS = 8
PEER_N_KEYS = 128
PEER_TOPK = 16
PEER_HALF = 128
ROPE_THETA = 10000.0
LN_EPS = 1e-5
RMS_EPS = 1e-6
NEG_INF = -1e30
LOG2_E = 1.4426950408889634
MLA_Q_SCALE = (MLA_NOPE + MLA_ROPE) ** -0.5 * LOG2_E

V7X_VMEM_LIMIT_BYTES = 56 * 1024 * 1024
NT_DIMS = (((1,), (1,)), ((), ()))
TN_DIMS = (((0,), (0,)), ((), ()))


def _params(n_axes):
    return pltpu.CompilerParams(dimension_semantics=("arbitrary",) * n_axes,
                                vmem_limit_bytes=V7X_VMEM_LIMIT_BYTES)


def _mm_kernel(a_ref, b_ref, o_ref):
    a = a_ref[...].astype(MXU_DTYPE)
    o_ref[...] = jnp.dot(a, b_ref[...], preferred_element_type=F32).astype(o_ref.dtype)


def _matmul(a, b, out_dtype, bm=1024, bn=1024):
    m, k = a.shape
    n = b.shape[1]
    bm, bn = min(bm, m), min(bn, n)
    assert m % bm == 0 and n % bn == 0
    return pl.pallas_call(
        _mm_kernel,
        grid=(m // bm, n // bn),
        in_specs=[pl.BlockSpec((bm, k), lambda i, j: (i, 0)),
                  pl.BlockSpec((k, bn), lambda i, j: (0, j))],
        out_specs=pl.BlockSpec((bm, bn), lambda i, j: (i, j)),
        out_shape=jax.ShapeDtypeStruct((m, n), out_dtype),
        compiler_params=_params(2),
        name="matmul",
    )(a, b)


def _layer_norm_rows(y, g, b):
    mu = jnp.mean(y, axis=-1, keepdims=True)
    d = y - mu
    var = jnp.mean(d * d, axis=-1, keepdims=True)
    return d * lax.rsqrt(var + LN_EPS) * g + b


def _mm_ln_kernel(*refs, n_pairs, alpha):
    a_refs = refs[:n_pairs]
    w_refs = refs[n_pairs:2 * n_pairs]
    res_ref, g_ref, b_ref, o_ref, ob_ref = refs[2 * n_pairs:]
    acc = jnp.dot(a_refs[0][...].astype(MXU_DTYPE), w_refs[0][...], preferred_element_type=F32)
    for a_ref, w_ref in zip(a_refs[1:], w_refs[1:]):
        acc = acc + jnp.dot(a_ref[...].astype(MXU_DTYPE), w_ref[...], preferred_element_type=F32)
    y = _layer_norm_rows(alpha * res_ref[...] + acc, g_ref[...], b_ref[...])
    o_ref[...] = y
    ob_ref[...] = y.astype(ob_ref.dtype)


def _matmul_residual_ln(a_list, w_list, res, gain, bias, alpha, bm=256):
    m, n = res.shape
    n_pairs = len(a_list)
    in_specs = [pl.BlockSpec((bm, a.shape[1]), lambda i: (i, 0)) for a in a_list]
    in_specs += [pl.BlockSpec(w.shape, lambda i: (0, 0)) for w in w_list]
    in_specs += [pl.BlockSpec((bm, n), lambda i: (i, 0)),
                 pl.BlockSpec((1, n), lambda i: (0, 0)),
                 pl.BlockSpec((1, n), lambda i: (0, 0))]
    return pl.pallas_call(
        functools.partial(_mm_ln_kernel, n_pairs=n_pairs, alpha=alpha),
        grid=(m // bm,),
        in_specs=in_specs,
        out_specs=[pl.BlockSpec((bm, n), lambda i: (i, 0)), pl.BlockSpec((bm, n), lambda i: (i, 0))],
        out_shape=[jax.ShapeDtypeStruct((m, n), F32), jax.ShapeDtypeStruct((m, n), MXU_DTYPE)],
        compiler_params=_params(1),
        name="matmul_residual_ln",
    )(*a_list, *w_list, res, gain.reshape(1, n), bias.reshape(1, n))


def _rope_tables(seq):
    pos = jnp.arange(seq, dtype=F32)[:, None]

    def cos_sin(half):
        inv = ROPE_THETA ** (-jnp.arange(half, dtype=F32) / half)
        ang = pos * inv[None, :]
        return jnp.cos(ang), jnp.sin(ang)

    c, s = cos_sin(HEAD_DIM // 2)
    cos_a = jnp.concatenate([c, c], axis=1)
    sin_a = jnp.concatenate([-s, s], axis=1)
    c, s = cos_sin(MLA_ROPE // 2)
    z32 = jnp.zeros_like(c)
    cos_b = jnp.concatenate([c, c, z32, z32], axis=1)
    sin_lo = jnp.concatenate([-s, z32, z32, z32], axis=1)
    sin_hi = jnp.concatenate([z32, s, z32, z32], axis=1)
    return cos_a, sin_a, cos_b, sin_lo, sin_hi


def _rope128(x, cos_a, sin_a):
    return x * cos_a + pltpu.roll(x, HEAD_DIM // 2, 1) * sin_a


def _rope64(x, cos_b, sin_lo, sin_hi):
    return x * cos_b + pltpu.roll(x, 96, 1) * sin_lo + pltpu.roll(x, 32, 1) * sin_hi


def _rms_norm_rows(x, g):
    return x * lax.rsqrt(jnp.mean(x * x, axis=-1, keepdims=True) + RMS_EPS) * g


def _even_prep_kernel(qk_ref, tail_ref, cos_a_ref, sin_a_ref, cos_b_ref, sin_lo_ref, sin_hi_ref,
                      qn_ref, kvn_ref, qa_ref, ka_ref, cq_ref, ckv_ref, kr_ref):
    cos_a, sin_a = cos_a_ref[...], sin_a_ref[...]
    n_qk = A_HEADS * HEAD_DIM
    for hd in range(A_HEADS):
        lo = hd * HEAD_DIM
        qa_ref[:, lo:lo + HEAD_DIM] = _rope128(qk_ref[:, lo:lo + HEAD_DIM], cos_a, sin_a)
        ka_ref[:, lo:lo + HEAD_DIM] = _rope128(qk_ref[:, n_qk + lo:n_qk + lo + HEAD_DIM], cos_a, sin_a)
    cq_ref[...] = _rms_norm_rows(tail_ref[:, :MLA_Q_LORA], qn_ref[...]).astype(cq_ref.dtype)
    o2 = MLA_Q_LORA + MLA_KV_LORA
    ckv_ref[...] = _rms_norm_rows(tail_ref[:, MLA_Q_LORA:o2], kvn_ref[...]).astype(ckv_ref.dtype)
    kr = _rope64(tail_ref[:, o2:o2 + 128], cos_b_ref[...], sin_lo_ref[...], sin_hi_ref[...])
    kr_ref[...] = kr.astype(kr_ref.dtype)


def _even_prep(h, tables, q_norm, kv_norm, seq, rows=256):
    t = h.shape[0]
    n_qk = A_HEADS * HEAD_DIM
    sb = seq // rows
    tab_spec = pl.BlockSpec((rows, 128), lambda i: (i % sb, 0))
    return pl.pallas_call(
        _even_prep_kernel,
        grid=(t // rows,),
        in_specs=[pl.BlockSpec((rows, 2 * n_qk), lambda i: (i, 0)),
                  pl.BlockSpec((rows, 1024), lambda i: (i, 3)),
                  tab_spec, tab_spec, tab_spec, tab_spec, tab_spec,
                  pl.BlockSpec((1, MLA_Q_LORA), lambda i: (0, 0)),
                  pl.BlockSpec((1, MLA_KV_LORA), lambda i: (0, 0))],
        out_specs=[pl.BlockSpec((rows, n_qk), lambda i: (i, 0)),
                   pl.BlockSpec((rows, n_qk), lambda i: (i, 0)),
                   pl.BlockSpec((rows, MLA_Q_LORA), lambda i: (i, 0)),
                   pl.BlockSpec((rows, MLA_KV_LORA), lambda i: (i, 0)),
                   pl.BlockSpec((rows, 128), lambda i: (i, 0))],
        out_shape=[jax.ShapeDtypeStruct((t, n_qk), F32),
                   jax.ShapeDtypeStruct((t, n_qk), F32),
                   jax.ShapeDtypeStruct((t, MLA_Q_LORA), MXU_DTYPE),
                   jax.ShapeDtypeStruct((t, MLA_KV_LORA), MXU_DTYPE),
                   jax.ShapeDtypeStruct((t, 128), MXU_DTYPE)],
        compiler_params=_params(1),
        name="even_prep",
    )(h, h, *tables, q_norm.reshape(1, -1), kv_norm.reshape(1, -1))


def _mla_prep_kernel(q_ref, kn_ref, kr_ref, cos_b_ref, sin_lo_ref, sin_hi_ref, qm_ref, km_ref):
    cos_b, sin_lo, sin_hi = cos_b_ref[...], sin_lo_ref[...], sin_hi_ref[...]
    kr = kr_ref[...]
    for hd in range(B_HEADS):
        lo = hd * MLA_QK_PAD
        qm_ref[:, lo:lo + MLA_NOPE] = (q_ref[:, lo:lo + MLA_NOPE] * MLA_Q_SCALE).astype(qm_ref.dtype)
        q_rope = _rope64(q_ref[:, lo + MLA_NOPE:lo + MLA_QK_PAD], cos_b, sin_lo, sin_hi)
        qm_ref[:, lo + MLA_NOPE:lo + MLA_QK_PAD] = (q_rope * MLA_Q_SCALE).astype(qm_ref.dtype)
        km_ref[:, lo:lo + MLA_NOPE] = kn_ref[:, hd * MLA_NOPE:(hd + 1) * MLA_NOPE]
        km_ref[:, lo + MLA_NOPE:lo + MLA_QK_PAD] = kr


def _mla_prep(qf, kvf, kr, tables, seq, rows=256):
    t = qf.shape[0]
    w = B_HEADS * MLA_QK_PAD
    sb = seq // rows
    tab_spec = pl.BlockSpec((rows, 128), lambda i: (i % sb, 0))
    return pl.pallas_call(
        _mla_prep_kernel,
        grid=(t // rows,),
        in_specs=[pl.BlockSpec((rows, w), lambda i: (i, 0)),
                  pl.BlockSpec((rows, B_HEADS * MLA_NOPE), lambda i: (i, 0)),
                  pl.BlockSpec((rows, 128), lambda i: (i, 0)),
                  tab_spec, tab_spec, tab_spec],
        out_specs=[pl.BlockSpec((rows, w), lambda i: (i, 0)), pl.BlockSpec((rows, w), lambda i: (i, 0))],
        out_shape=[jax.ShapeDtypeStruct((t, w), MXU_DTYPE), jax.ShapeDtypeStruct((t, w), MXU_DTYPE)],
        compiler_params=_params(1),
        name="mla_prep",
    )(qf, kvf, kr, *tables)


def _dilated_kernel(q_ref, kc_ref, kp_ref, vc_ref, vp_ref, o_ref,
                    kbuf, vbuf, o0, o1, o2, l0, l1, l2):
    c = pl.program_id(2)
    kbuf[:DIL_CHUNK, :] = kp_ref[...]
    kbuf[DIL_CHUNK:, :] = kc_ref[...]
    vbuf[:DIL_CHUNK, :] = vp_ref[...]
    vbuf[DIL_CHUNK:, :] = vc_ref[...]
    scale = HEAD_DIM ** -0.5
    row = lax.broadcasted_iota(jnp.int32, (BLOCK, 2 * BLOCK), 0)
    col = lax.broadcasted_iota(jnp.int32, (BLOCK, 2 * BLOCK), 1)
    band = (col >= row) & (col <= row + DIL_STEPS)

    def blocks(specs):
        staged = []
        for q_start, k_start, dil, first, o_g, l_g in specs:
            if dil == 1:
                q_idx, k_idx = pl.ds(q_start, BLOCK), pl.ds(k_start, 2 * BLOCK)
            else:
                q_idx = pl.ds(q_start, BLOCK, stride=dil)
                k_idx = pl.ds(k_start, 2 * BLOCK, stride=dil)
            q = q_ref[q_idx, :].astype(MXU_DTYPE)
            k = kbuf[k_idx, :].astype(MXU_DTYPE)
            v = vbuf[k_idx, :].astype(MXU_DTYPE)
            logits = lax.dot_general(q, k, NT_DIMS, preferred_element_type=F32) * scale
            staged.append((logits, v, q_idx, first, o_g, l_g))
        weighted = []
        for logits, v, q_idx, first, o_g, l_g in staged:
            valid_from = jnp.where(first, BLOCK, 0)
            logits = jnp.where(band & (col >= valid_from), logits, NEG_INF)
            m = jnp.max(logits, axis=-1, keepdims=True)
            p = jnp.exp(logits - m)
            denom = jnp.sum(p, axis=-1, keepdims=True)
            weighted.append(((p / denom).astype(MXU_DTYPE), m + jnp.log(denom)))
        for (logits, v, q_idx, first, o_g, l_g), (pn, lse) in zip(staged, weighted):
            o_g[q_idx, :] = jnp.dot(pn, v, preferred_element_type=F32)
            l_g[q_idx, :] = jnp.broadcast_to(lse, (BLOCK, HEAD_DIM))

    n_blocks = DIL_CHUNK // BLOCK
    per_trip = 2

    def body(j, carry):
        specs = []
        for u in range(per_trip):
            i = j * per_trip + u
            specs.append((pl.multiple_of(i * BLOCK, BLOCK),
                          pl.multiple_of(DIL_CHUNK + (i - 1) * BLOCK, BLOCK),
                          DILATIONS[0], (c == 0) & (i == 0), o0, l0))
            d1 = DILATIONS[1]
            n, r = i // d1, i % d1
            span = BLOCK * d1
            specs.append((n * span + r, DIL_CHUNK + (n - 1) * span + r, d1, (c == 0) & (n == 0), o1, l1))
            specs.append((i, DIL_CHUNK - BLOCK * DILATIONS[2] + i, DILATIONS[2], c == 0, o2, l2))
        blocks(specs)
        return carry

    lax.fori_loop(0, n_blocks // per_trip, body, 0)

    la, lb, lc = l0[...], l1[...], l2[...]
    mx = jnp.maximum(jnp.maximum(la, lb), lc)
    ea, eb, ec = jnp.exp(la - mx), jnp.exp(lb - mx), jnp.exp(lc - mx)
    merged = (ea * o0[...] + eb * o1[...] + ec * o2[...]) / (ea + eb + ec)
    o_ref[...] = merged.astype(o_ref.dtype)


def _dilated_attention(qa, ka, h, batch, seq):
    t = qa.shape[0]
    nc = seq // DIL_CHUNK
    v_col0 = 2 * A_HEADS
    cur = lambda b, hd, c: (b * nc + c, hd)
    prev = lambda b, hd, c: (b * nc + jnp.maximum(c - 1, 0), hd)
    blk = (DIL_CHUNK, HEAD_DIM)
    return pl.pallas_call(
        _dilated_kernel,
        grid=(batch, A_HEADS, nc),
        in_specs=[pl.BlockSpec(blk, cur),
                  pl.BlockSpec(blk, cur), pl.BlockSpec(blk, prev),
                  pl.BlockSpec(blk, lambda b, hd, c: (b * nc + c, v_col0 + hd)),
                  pl.BlockSpec(blk, lambda b, hd, c: (b * nc + jnp.maximum(c - 1, 0), v_col0 + hd))],
        out_specs=pl.BlockSpec(blk, cur),
        out_shape=jax.ShapeDtypeStruct((t, A_HEADS * HEAD_DIM), MXU_DTYPE),
        scratch_shapes=[pltpu.VMEM((2 * DIL_CHUNK, HEAD_DIM), F32)] * 2
                       + [pltpu.VMEM(blk, F32)] * 6,
        compiler_params=_params(3),
        name="dilated_attention",
    )(qa, ka, ka, h, h)


def _mla_kernel(q_ref, k_ref, v_ref, o_ref, m_ref, l_ref, acc_ref, *, bq, bk, unroll):
    qi = pl.program_id(2)
    n_sub = bq // bk
    m_ref[...] = jnp.full_like(m_ref, NEG_INF)
    l_ref[...] = jnp.zeros_like(l_ref)
    acc_ref[...] = jnp.zeros_like(acc_ref)

    def group(k_starts, r0, diagonal):
        cols = bq - r0
        q = q_ref[r0:, :]
        ss = [lax.dot_general(k_ref[pl.ds(ks, bk), :], q, NT_DIMS, preferred_element_type=F32)
              for ks in k_starts]
        if diagonal:
            key = lax.broadcasted_iota(jnp.int32, (bk, cols), 0)
            qry = lax.broadcasted_iota(jnp.int32, (bk, cols), 1)
            ss = [jnp.where(key <= qry, s, NEG_INF) for s in ss]
        m_prev = m_ref[:, r0:]
        m_new = m_prev
        for s in ss:
            m_new = jnp.maximum(m_new, jnp.max(s, axis=0, keepdims=True))
        alpha = jnp.exp2(m_prev - m_new)
        l = alpha * l_ref[:, r0:]
        acc = alpha * acc_ref[:, r0:]
        for ks, s in zip(k_starts, ss):
            p = jnp.exp2(s - m_new)
            l = l + jnp.sum(p, axis=0, keepdims=True)
            acc = acc + lax.dot_general(v_ref[pl.ds(ks, bk), :], p.astype(MXU_DTYPE), TN_DIMS,
                                        preferred_element_type=F32)
        m_ref[:, r0:] = m_new
        l_ref[:, r0:] = l
        acc_ref[:, r0:] = acc

    def body(j, carry):
        group([pl.multiple_of((j * unroll + u) * bk, bk) for u in range(unroll)], 0, False)
        return carry

    assert n_sub % unroll == 0
    lax.fori_loop(0, qi * (n_sub // unroll), body, 0)
    for jj in range(n_sub):
        group([pl.multiple_of(qi * bq + jj * bk, bk)], jj * bk, True)
    o_ref[...] = (acc_ref[...] / l_ref[...]).T.astype(o_ref.dtype)


def _mla_attention(qm, km, kvf, batch, seq, bq=1024, bk=512, unroll=2):
    t = qm.shape[0]
    bq, bk = min(bq, seq), min(bk, seq)
    nq = seq // bq
    return pl.pallas_call(
        functools.partial(_mla_kernel, bq=bq, bk=bk, unroll=min(unroll, bq // bk)),
        grid=(batch, B_HEADS, nq),
        in_specs=[pl.BlockSpec((bq, MLA_QK_PAD), lambda b, hd, i: (b * nq + i, hd)),
                  pl.BlockSpec((seq, MLA_QK_PAD), lambda b, hd, i: (b, hd)),
                  pl.BlockSpec((seq, MLA_V), lambda b, hd, i: (b, B_HEADS + hd))],
        out_specs=pl.BlockSpec((bq, MLA_V), lambda b, hd, i: (b * nq + i, hd)),
        out_shape=jax.ShapeDtypeStruct((t, B_HEADS * MLA_V), MXU_DTYPE),
        scratch_shapes=[pltpu.VMEM((1, bq), F32), pltpu.VMEM((1, bq), F32),
                        pltpu.VMEM((MLA_V, bq), F32)],
        compiler_params=_params(3),
        name="mla_attention",
    )(qm, km, kvf)


def _stick_kernel(q_ref, k_ref, v_ref, o_ref, acc_ref, run_ref, *, bq, bk, unroll):
    qi = pl.program_id(2)
    n_sub = bq // bk
    tri_r = lax.broadcasted_iota(jnp.int32, (bk, bk), 0)
    tri_c = lax.broadcasted_iota(jnp.int32, (bk, bk), 1)
    at_or_after = jnp.where(tri_r >= tri_c, 1.0, 0.0).astype(MXU_DTYPE)
    acc_ref[...] = jnp.zeros_like(acc_ref)
    run_ref[...] = jnp.zeros_like(run_ref)

    def group(tiles):
        staged = []
        for ks, r0, diagonal in tiles:
            rows = bq - r0
            z = lax.dot_general(q_ref[r0:, :], k_ref[pl.ds(ks, bk), :], NT_DIMS,
                                preferred_element_type=F32)
            neg_abs = lax.bitcast_convert_type(
                lax.bitcast_convert_type(z, jnp.uint32) | jnp.uint32(0x80000000), F32)
            sp = jnp.maximum(z, 0.0) + jnp.log(1.0 + jnp.exp2(neg_abs)) * LOG2_E
            mask = None
            if diagonal:
                row = lax.broadcasted_iota(jnp.int32, (rows, bk), 0)
                col = lax.broadcasted_iota(jnp.int32, (rows, bk), 1)
                mask = col < row
                sp = jnp.where(mask, sp, 0.0)
            staged.append((z, sp, mask))
        sufs = []
        for _, sp, _ in staged:
            hi = sp.astype(MXU_DTYPE)
            lo = (sp - hi.astype(F32)).astype(MXU_DTYPE)
            sufs.append(jnp.dot(hi, at_or_after, preferred_element_type=F32)
                        + jnp.dot(lo, at_or_after, preferred_element_type=F32))
        run = run_ref[...]
        acc = acc_ref[...]
        for (ks, r0, _), (z, _, mask), suf in zip(tiles, staged, sufs):
            att = jnp.exp2(z - suf - run[r0:])
            if mask is not None:
                att = jnp.where(mask, att, 0.0)
            part = jnp.dot(att.astype(MXU_DTYPE), v_ref[pl.ds(ks, bk), :], preferred_element_type=F32)
            total = suf[:, :1]
            if r0 == 0:
                acc, run = acc + part, run + total
            else:
                acc = jnp.concatenate([acc[:r0], acc[r0:] + part], axis=0)
                run = jnp.concatenate([run[:r0], run[r0:] + total], axis=0)
        acc_ref[...] = acc
        run_ref[...] = run

    group([(pl.multiple_of(qi * bq + jj * bk, bk), jj * bk, True) for jj in reversed(range(n_sub))])

    def body(j, carry):
        group([(pl.multiple_of((qi * n_sub - 1 - j * unroll - u) * bk, bk), 0, False)
               for u in range(unroll)])
        return carry

    assert n_sub % unroll == 0
    lax.fori_loop(0, qi * (n_sub // unroll), body, 0)
    o_ref[...] = acc_ref[...].astype(o_ref.dtype)


def _stick_attention(qkv, batch, seq, bq=1024, bk=256, unroll=4):
    t = qkv.shape[0]
    bq, bk = min(bq, seq), min(bk, seq)
    nq = seq // bq
    return pl.pallas_call(
        functools.partial(_stick_kernel, bq=bq, bk=bk, unroll=min(unroll, bq // bk)),
        grid=(batch, C_HEADS, nq),
        in_specs=[pl.BlockSpec((bq, HEAD_DIM), lambda b, hd, i: (b * nq + i, hd)),
                  pl.BlockSpec((seq, HEAD_DIM), lambda b, hd, i: (b, C_HEADS + hd)),
                  pl.BlockSpec((seq, HEAD_DIM), lambda b, hd, i: (b, 2 * C_HEADS + hd))],
        out_specs=pl.BlockSpec((bq, HEAD_DIM), lambda b, hd, i: (b * nq + i, hd)),
        out_shape=jax.ShapeDtypeStruct((t, C_HEADS * HEAD_DIM), MXU_DTYPE),
        scratch_shapes=[pltpu.VMEM((bq, HEAD_DIM), F32), pltpu.VMEM((bq, 1), F32)],
        compiler_params=_params(3),
        name="stick_breaking_attention",
    )(qkv, qkv, qkv)


def _top_values(s, count, with_rank=False):
    vals = []
    cur = s
    rank = jnp.full(s.shape, float(count), F32) if with_rank else None
    for k in range(count):
        m = jnp.max(cur, axis=0, keepdims=True)
        vals.append(m)
        hit = cur == m
        if with_rank:
            rank = jnp.where(hit, float(k), rank)
        cur = jnp.where(hit, NEG_INF, cur)
    return (vals, rank) if with_rank else vals


def _ranked(s, count):
    n = s.shape[0]
    rows = lax.broadcasted_iota(jnp.int32, s.shape, 0)
    vals = []
    cur = s
    rank = jnp.full(s.shape, float(count), F32)
    for k in range(count):
        m = jnp.max(cur, axis=0, keepdims=True)
        first = jnp.min(jnp.where(cur == m, rows, n), axis=0, keepdims=True)
        pick = rows == first
        vals.append(m)
        rank = jnp.where(pick, float(k), rank)
        cur = jnp.where(pick, NEG_INF, cur)
    return vals, rank


def _pair_sums(v1, v2):
    v1_all = jnp.concatenate(v1, axis=0)
    v2_all = jnp.concatenate(v2, axis=0)
    return jnp.concatenate([v1[0] + v2_all]
                           + [v1[p] + v2_all[:8] for p in range(1, 8)]
                           + [v2[0] + v1_all[8:]], axis=0)


def _count_at_least(s, threshold):
    return jnp.sum(jnp.where(s >= threshold, 1.0, 0.0), axis=0, keepdims=True)


def _route_distinct(s1, s2):
    v1 = _top_values(s1, PEER_TOPK)
    v2, rank2 = _top_values(s2, PEER_TOPK, with_rank=True)
    cand = _pair_sums(v1, v2)
    tops = _top_values(cand, PEER_TOPK + 1)
    cut = 0.5 * (tops[PEER_TOPK - 1] + tops[PEER_TOPK])
    top = tops[0]
    z = jnp.sum(jnp.where(cand >= cut, jnp.exp(cand - top), 0.0), axis=0, keepdims=True)
    need = cut - s1
    count = jnp.zeros_like(s1)
    for q in range(PEER_TOPK):
        count = count + jnp.where(v2[q] >= need, 1.0, 0.0)
    a = jnp.where(s1 >= v1[PEER_TOPK - 1], jnp.exp(s1 - v1[0]) / (2.0 * z), 0.0)
    b = jnp.where(s2 >= v2[PEER_TOPK - 1], jnp.exp(s2 - v2[0]), 0.0)
    distinct = ((_count_at_least(s1, v1[PEER_TOPK - 1]) == PEER_TOPK)
                & (_count_at_least(s2, v2[PEER_TOPK - 1]) == PEER_TOPK)
                & (_count_at_least(cand, tops[PEER_TOPK]) == PEER_TOPK + 1))
    return rank2, b, count, a, distinct


def _route_with_ties(s1, s2):
    v1, rank1 = _ranked(s1, PEER_TOPK)
    v2, rank2 = _ranked(s2, PEER_TOPK)
    cand = _pair_sums(v1, v2)
    _, cand_rank = _ranked(cand, PEER_TOPK)
    chosen = cand_rank < PEER_TOPK
    top = v1[0] + v2[0]
    z = jnp.sum(jnp.where(chosen, jnp.exp(cand - top), 0.0), axis=0, keepdims=True)
    picked = jnp.where(chosen, 1.0, 0.0)
    per_p = [jnp.sum(picked[:16], axis=0, keepdims=True)]
    per_p += [jnp.sum(picked[8 + 8 * p:16 + 8 * p], axis=0, keepdims=True) for p in range(1, 8)]
    per_p += [picked[64 + p:65 + p] for p in range(8, 16)]
    count = jnp.zeros_like(s1)
    for p in range(PEER_TOPK):
        count = count + jnp.where(rank1 == p, per_p[p], 0.0)
    a = jnp.where(rank1 < PEER_TOPK, jnp.exp(s1 - v1[0]) / (2.0 * z), 0.0)
    b = jnp.where(rank2 < PEER_TOPK, jnp.exp(s2 - v2[0]), 0.0)
    return rank2, b, count, a


def _peer_route_kernel(q_ref, keys_ref, r2_ref, b_ref, c_ref, a_ref):
    def head(hd, carry):
        col = pl.multiple_of(hd * 2 * PEER_HALF, 2 * PEER_HALF)
        q1 = q_ref[:, pl.ds(col, PEER_HALF)].astype(MXU_DTYPE)
        q2 = q_ref[:, pl.ds(col + PEER_HALF, PEER_HALF)].astype(MXU_DTYPE)
        s1 = lax.dot_general(keys_ref[hd, 0], q1, NT_DIMS, preferred_element_type=F32)
        s2 = lax.dot_general(keys_ref[hd, 1], q2, NT_DIMS, preferred_element_type=F32)

        def store(rank2, b, count, a):
            word_rows = pl.ds(pl.multiple_of(hd * (PEER_N_KEYS // 2), PEER_N_KEYS // 2), PEER_N_KEYS // 2)
            r2_ref[word_rows, :] = pltpu.bitcast(rank2.astype(jnp.bfloat16), jnp.uint32)
            b_ref[word_rows, :] = pltpu.bitcast(b.astype(jnp.bfloat16), jnp.uint32)
            c_ref[hd] = _pair_words(count)
            a_ref[hd] = _pair_words(a)

        rank2, b, count, a, distinct = _route_distinct(s1, s2)
        store(rank2, b, count, a)

        @pl.when(jnp.max(jnp.where(distinct, 0.0, 1.0)) > 0.0)
        def _():
            store(*_route_with_ties(s1, s2))

        return carry

    lax.fori_loop(0, PEER_HEADS, head, 0)


def _peer_route(qp, keys, tr=256):
    t = qp.shape[0]
    shape = (PEER_HEADS, PEER_N_KEYS, t)
    spec = pl.BlockSpec((PEER_HEADS, PEER_N_KEYS, tr), lambda i: (0, 0, i))
    flat = (PEER_HEADS * PEER_N_KEYS // 2, t)
    flat_spec = pl.BlockSpec((PEER_HEADS * PEER_N_KEYS // 2, tr), lambda i: (0, i))
    return pl.pallas_call(
        _peer_route_kernel,
        grid=(t // tr,),
        in_specs=[pl.BlockSpec((tr, qp.shape[1]), lambda i: (i, 0)),
                  pl.BlockSpec(keys.shape, lambda i: (0, 0, 0, 0))],
        out_specs=[flat_spec, flat_spec, spec, spec],
        out_shape=[jax.ShapeDtypeStruct(flat, jnp.uint32), jax.ShapeDtypeStruct(flat, jnp.uint32),
                   jax.ShapeDtypeStruct(shape, jnp.uint32), jax.ShapeDtypeStruct(shape, jnp.uint32)],
        compiler_params=_params(1),
        name="peer_route",
    )(qp, keys)


def _twice_gelu(x):
    return x * (1.0 + lax.erf(x * (2.0 ** -0.5)))


def _pair_words(rows):
    bits = lax.bitcast_convert_type(rows.astype(jnp.bfloat16).astype(F32), jnp.uint32)
    return bits | (bits >> 16)


def _row_tile(words, rows):
    packed = pltpu.bitcast(jnp.broadcast_to(words, (8, 128)), jnp.bfloat16)
    return jnp.concatenate([packed] * (rows // 16), axis=0)


def _peer_kernel(xt_ref, res_ref, u_ref, v_ref, r2_ref, b_ref, c_ref, a_ref, g_ref, bias_ref,
                 o_ref, ob_ref, st_ref, wt_ref, *, n_i, tb, alpha):
    e = pl.program_id(1)

    @pl.when(e == 0)
    def _():
        o_ref[...] = jnp.zeros_like(o_ref)

    st_ref[...] = jnp.dot(u_ref[...], xt_ref[...], preferred_element_type=F32)
    zero = jnp.zeros((PEER_N_KEYS, 128), MXU_DTYPE)
    assert 8 % n_i == 0
    per_group = 8 // n_i
    within = e % per_group

    def key_rows(ref, hd, lanes):
        grp = ref[hd, 0, :, lanes]
        rows = grp[:n_i]
        for k in range(1, per_group):
            rows = jnp.where(within == k, grp[k * n_i:(k + 1) * n_i], rows)
        return rows

    live = min(n_i, 4)
    for l in range(tb // 128):
        lanes = slice(l * 128, (l + 1) * 128)
        for i0 in range(0, n_i, live):
            gates = [zero for _ in range(live)]
            for hd in range(PEER_HEADS):
                words = slice(hd * (PEER_N_KEYS // 2), (hd + 1) * (PEER_N_KEYS // 2))
                r2 = pltpu.bitcast(r2_ref[words, lanes], jnp.bfloat16)
                b = pltpu.bitcast(b_ref[words, lanes], jnp.bfloat16)
                counts = key_rows(c_ref, hd, lanes)
                a_rows = key_rows(a_ref, hd, lanes)
                for k in range(live):
                    count = _row_tile(counts[i0 + k:i0 + k + 1], PEER_N_KEYS)
                    a = _row_tile(a_rows[i0 + k:i0 + k + 1], PEER_N_KEYS)
                    gates[k] = gates[k] + jnp.where(r2 < count, b, zero) * a
            for k in range(live):
                rows = slice((i0 + k) * PEER_N_KEYS, (i0 + k + 1) * PEER_N_KEYS)
                act = _twice_gelu(st_ref[rows, lanes]).astype(MXU_DTYPE)
                wt_ref[rows, lanes] = act * gates[k]
    o_ref[...] += lax.dot_general(wt_ref[...], v_ref[...], TN_DIMS, preferred_element_type=F32)

    @pl.when(e == pl.num_programs(1) - 1)
    def _():
        y = _layer_norm_rows(alpha * res_ref[...] + o_ref[...], g_ref[...], bias_ref[...])
        o_ref[...] = y
        ob_ref[...] = y.astype(ob_ref.dtype)


def _peer_experts(xt, res, u, v, r2, b, c, a, gain, bias, alpha, tb=512, n_i=8):
    t, d = res.shape
    n_exp = u.shape[0]
    eb = n_i * PEER_N_KEYS
    tb = min(tb, t)
    tok = lambda i, e: (i, 0)
    per_group = 8 // n_i
    c = c.reshape(PEER_HEADS, PEER_N_KEYS // 8, 8, t)
    a = a.reshape(PEER_HEADS, PEER_N_KEYS // 8, 8, t)
    full = pl.BlockSpec((PEER_HEADS, 1, 8, tb), lambda i, e: (0, e // per_group, 0, i))
    flat = pl.BlockSpec((PEER_HEADS * PEER_N_KEYS // 2, tb), lambda i, e: (0, i))
    return pl.pallas_call(
        functools.partial(_peer_kernel, n_i=n_i, tb=tb, alpha=alpha),
        grid=(t // tb, n_exp // eb),
        in_specs=[pl.BlockSpec((d, tb), lambda i, e: (0, i)), pl.BlockSpec((tb, d), tok),
                  pl.BlockSpec((eb, d), lambda i, e: (e, 0)),
                  pl.BlockSpec((eb, d), lambda i, e: (e, 0)),
                  flat, flat, full, full,
                  pl.BlockSpec((1, d), lambda i, e: (0, 0)), pl.BlockSpec((1, d), lambda i, e: (0, 0))],
        out_specs=[pl.BlockSpec((tb, d), tok), pl.BlockSpec((tb, d), tok)],
        out_shape=[jax.ShapeDtypeStruct((t, d), F32), jax.ShapeDtypeStruct((t, d), MXU_DTYPE)],
        scratch_shapes=[pltpu.VMEM((eb, tb), F32), pltpu.VMEM((eb, tb), MXU_DTYPE)],
        compiler_params=_params(2),
        name="peer_experts",
    )(xt, res, u, v, r2, b, c, a, gain.reshape(1, d), bias.reshape(1, d))


def _peer_layer(x, xb, w_query, keys, u, v, gain, bias, alpha):
    qp = _matmul(xb, w_query, F32)
    r2, b, c, a = _peer_route(qp, keys)
    return _peer_experts(xb.T, x, u, v, r2, b, c, a, gain, bias, alpha)


def _even_mixer(x2d, batch, seq, w_in, q_norm, w_q_b, kv_norm, w_kv_b, tables):
    d = x2d.shape[1]
    o1 = 3 * A_HEADS * HEAD_DIM
    o3 = o1 + MLA_Q_LORA + MLA_KV_LORA
    w_in_p = jnp.concatenate(
        [w_in, jnp.zeros((d, 4096 - w_in.shape[1]), w_in.dtype)], axis=1).astype(MXU_DTYPE)
    h = _matmul(x2d, w_in_p, F32)
    cos_a, sin_a, cos_b, sin_lo, sin_hi = tables
    qa, ka, cqn, ckvn, kr = _even_prep(h, tables, q_norm, kv_norm, seq)
    out_a = _dilated_attention(qa, ka, h, batch, seq)
    wq = w_q_b.reshape(MLA_Q_LORA, B_HEADS, MLA_NOPE + MLA_ROPE)
    wq = jnp.pad(wq, ((0, 0), (0, 0), (0, MLA_QK_PAD - MLA_NOPE - MLA_ROPE)))
    wq = wq.reshape(MLA_Q_LORA, B_HEADS * MLA_QK_PAD).astype(MXU_DTYPE)
    wkv = w_kv_b.reshape(MLA_KV_LORA, B_HEADS, MLA_NOPE + MLA_V)
    wkv = jnp.concatenate([wkv[:, :, :MLA_NOPE].reshape(MLA_KV_LORA, -1),
                           wkv[:, :, MLA_NOPE:].reshape(MLA_KV_LORA, -1)], axis=1).astype(MXU_DTYPE)
    qf = _matmul(cqn, wq, F32)
    kvf = _matmul(ckvn, wkv, MXU_DTYPE)
    qm, km = _mla_prep(qf, kvf, kr, (cos_b, sin_lo, sin_hi), seq)
    out_b = _mla_attention(qm, km, kvf, batch, seq)
    del o3
    return out_a, out_b


def kernel(x, a_w_in, b_q_norm, b_w_q_b, b_kv_norm, b_w_kv_b, ab_w_out, c_w_in, c_w_out,
           peer_w_query, peer_sub_keys, peer_u, peer_v, ln_gain, ln_bias):
    batch, seq, d = x.shape
    depth = peer_u.shape[0]
    alpha = (2 * depth) ** 0.25
    tables = _rope_tables(seq)
    xf = x.reshape(batch * seq, d)
    xb = None
    for layer in range(depth):
        i = layer // 2
        if layer % 2 == 0:
            src = xf if xb is None else xb
            out_a, out_b = _even_mixer(src, batch, seq, a_w_in[i], b_q_norm[i], b_w_q_b[i],
                                       b_kv_norm[i], b_w_kv_b[i], tables)
            w_out = ab_w_out[i].astype(MXU_DTYPE)
            half = A_HEADS * HEAD_DIM
            xf, xb = _matmul_residual_ln([out_a, out_b], [w_out[:half], w_out[half:]], xf,
                                         ln_gain[layer, 0], ln_bias[layer, 0], alpha)
        else:
            src = xf if xb is None else xb
            n_q = C_HEADS * HEAD_DIM
            w_qkv = jnp.concatenate([c_w_in[i][:, :n_q] * (HEAD_DIM ** -0.5 * LOG2_E),
                                     c_w_in[i][:, n_q:]], axis=1).astype(MXU_DTYPE)
            qkv = _matmul(src, w_qkv, MXU_DTYPE)
            o = _stick_attention(qkv, batch, seq)
            xf, xb = _matmul_residual_ln([o], [c_w_out[i].astype(MXU_DTYPE)], xf,
                                         ln_gain[layer, 0], ln_bias[layer, 0], alpha)
        xf, xb = _peer_layer(xf, xb, peer_w_query[layer].astype(MXU_DTYPE),
                         peer_sub_keys[layer].astype(MXU_DTYPE),
                         peer_u[layer].astype(MXU_DTYPE), peer_v[layer].astype(MXU_DTYPE),
                         ln_gain[layer, 1], ln_bias[layer, 1], alpha)
    return xf.reshape(batch, seq, d)
```

```python
import functools

import jax
import jax.numpy as jnp
from jax import lax
from jax.experimental import pallas as pl
from jax.experimental.pallas import tpu as pltpu

F32 = jnp.float32
MXU_DTYPE = jnp.bfloat16

HEAD_DIM = 128
BLOCK = 128
DIL_STEPS = 128
DILATIONS = (1, 4, 16)
DIL_CHUNK = BLOCK * DILATIONS[-1]
A_HEADS = 8
B_HEADS = 8
MLA_Q_LORA = 512
MLA_KV_LORA = 256
MLA_NOPE = 128
MLA_ROPE = 64
MLA_V = 128
MLA_QK_PAD = 256
C_HEADS = 16
PEER_HEADS = 8
PEER_N_KEYS = 128
PEER_TOPK = 16
PEER_HALF = 128
KEY_GROUPS = PEER_N_KEYS // 16
ROPE_THETA = 10000.0
LN_EPS = 1e-5
RMS_EPS = 1e-6
NEG_INF = -1e30
LOG2_E = 1.4426950408889634
MLA_Q_SCALE = (MLA_NOPE + MLA_ROPE) ** -0.5 * LOG2_E

V7X_VMEM_LIMIT_BYTES = 56 * 1024 * 1024
NT_DIMS = (((1,), (1,)), ((), ()))
TN_DIMS = (((0,), (0,)), ((), ()))


def _params(n_axes):
    return pltpu.CompilerParams(dimension_semantics=("arbitrary",) * n_axes,
                                vmem_limit_bytes=V7X_VMEM_LIMIT_BYTES)


def _mm_kernel(a_ref, b_ref, o_ref):
    a = a_ref[...].astype(MXU_DTYPE)
    o_ref[...] = jnp.dot(a, b_ref[...], preferred_element_type=F32).astype(o_ref.dtype)


def _matmul(a, b, out_dtype, bm=1024, bn=1024):
    m, k = a.shape
    n = b.shape[1]
    bm, bn = min(bm, m), min(bn, n)
    assert m % bm == 0 and n % bn == 0
    return pl.pallas_call(
        _mm_kernel,
        grid=(m // bm, n // bn),
        in_specs=[pl.BlockSpec((bm, k), lambda i, j: (i, 0)),
                  pl.BlockSpec((k, bn), lambda i, j: (0, j))],
        out_specs=pl.BlockSpec((bm, bn), lambda i, j: (i, j)),
        out_shape=jax.ShapeDtypeStruct((m, n), out_dtype),
        compiler_params=_params(2),
        name="matmul",
    )(a, b)


def _layer_norm_rows(y, g, b):
    mu = jnp.mean(y, axis=-1, keepdims=True)
    d = y - mu
    var = jnp.mean(d * d, axis=-1, keepdims=True)
    return d * lax.rsqrt(var + LN_EPS) * g + b


def _mm_ln_kernel(*refs, n_pairs, alpha):
    a_refs = refs[:n_pairs]
    w_refs = refs[n_pairs:2 * n_pairs]
    res_ref, g_ref, b_ref, o_ref, ob_ref = refs[2 * n_pairs:]
    acc = jnp.dot(a_refs[0][...].astype(MXU_DTYPE), w_refs[0][...], preferred_element_type=F32)
    for a_ref, w_ref in zip(a_refs[1:], w_refs[1:]):
        acc = acc + jnp.dot(a_ref[...].astype(MXU_DTYPE), w_ref[...], preferred_element_type=F32)
    y = _layer_norm_rows(alpha * res_ref[...] + acc, g_ref[...], b_ref[...])
    o_ref[...] = y
    ob_ref[...] = y.astype(ob_ref.dtype)


def _matmul_residual_ln(a_list, w_list, res, gain, bias, alpha, bm=256):
    m, n = res.shape
    n_pairs = len(a_list)
    in_specs = [pl.BlockSpec((bm, a.shape[1]), lambda i: (i, 0)) for a in a_list]
    in_specs += [pl.BlockSpec(w.shape, lambda i: (0, 0)) for w in w_list]
    in_specs += [pl.BlockSpec((bm, n), lambda i: (i, 0)),
                 pl.BlockSpec((1, n), lambda i: (0, 0)),
                 pl.BlockSpec((1, n), lambda i: (0, 0))]
    return pl.pallas_call(
        functools.partial(_mm_ln_kernel, n_pairs=n_pairs, alpha=alpha),
        grid=(m // bm,),
        in_specs=in_specs,
        out_specs=[pl.BlockSpec((bm, n), lambda i: (i, 0)), pl.BlockSpec((bm, n), lambda i: (i, 0))],
        out_shape=[jax.ShapeDtypeStruct((m, n), F32), jax.ShapeDtypeStruct((m, n), MXU_DTYPE)],
        compiler_params=_params(1),
        name="matmul_residual_ln",
    )(*a_list, *w_list, res, gain.reshape(1, n), bias.reshape(1, n))


def _rope_tables(seq):
    pos = jnp.arange(seq, dtype=F32)[:, None]

    def cos_sin(half):
        inv = ROPE_THETA ** (-jnp.arange(half, dtype=F32) / half)
        ang = pos * inv[None, :]
        return jnp.cos(ang), jnp.sin(ang)

    c, s = cos_sin(HEAD_DIM // 2)
    cos_a = jnp.concatenate([c, c], axis=1)
    sin_a = jnp.concatenate([-s, s], axis=1)
    c, s = cos_sin(MLA_ROPE // 2)
    z32 = jnp.zeros_like(c)
    cos_b = jnp.concatenate([c, c, z32, z32], axis=1)
    sin_lo = jnp.concatenate([-s, z32, z32, z32], axis=1)
    sin_hi = jnp.concatenate([z32, s, z32, z32], axis=1)
    return cos_a, sin_a, cos_b, sin_lo, sin_hi


def _rope128(x, cos_a, sin_a):
    return x * cos_a + pltpu.roll(x, HEAD_DIM // 2, 1) * sin_a


def _rope64(x, cos_b, sin_lo, sin_hi):
    return x * cos_b + pltpu.roll(x, 96, 1) * sin_lo + pltpu.roll(x, 32, 1) * sin_hi


def _rms_norm_rows(x, g):
    return x * lax.rsqrt(jnp.mean(x * x, axis=-1, keepdims=True) + RMS_EPS) * g


def _even_prep_kernel(qk_ref, tail_ref, cos_a_ref, sin_a_ref, cos_b_ref, sin_lo_ref, sin_hi_ref,
                      qn_ref, kvn_ref, qa_ref, ka_ref, cq_ref, ckv_ref, kr_ref):
    cos_a, sin_a = cos_a_ref[...], sin_a_ref[...]
    n_qk = A_HEADS * HEAD_DIM
    for hd in range(A_HEADS):
        lo = hd * HEAD_DIM
        qa_ref[:, lo:lo + HEAD_DIM] = _rope128(qk_ref[:, lo:lo + HEAD_DIM], cos_a, sin_a)
        ka_ref[:, lo:lo + HEAD_DIM] = _rope128(qk_ref[:, n_qk + lo:n_qk + lo + HEAD_DIM], cos_a, sin_a)
    cq_ref[...] = _rms_norm_rows(tail_ref[:, :MLA_Q_LORA], qn_ref[...]).astype(cq_ref.dtype)
    o2 = MLA_Q_LORA + MLA_KV_LORA
    ckv_ref[...] = _rms_norm_rows(tail_ref[:, MLA_Q_LORA:o2], kvn_ref[...]).astype(ckv_ref.dtype)
    kr = _rope64(tail_ref[:, o2:o2 + 128], cos_b_ref[...], sin_lo_ref[...], sin_hi_ref[...])
    kr_ref[...] = kr.astype(kr_ref.dtype)


def _even_prep(h, tables, q_norm, kv_norm, seq, rows=256):
    t = h.shape[0]
    n_qk = A_HEADS * HEAD_DIM
    sb = seq // rows
    tab_spec = pl.BlockSpec((rows, 128), lambda i: (i % sb, 0))
    return pl.pallas_call(
        _even_prep_kernel,
        grid=(t // rows,),
        in_specs=[pl.BlockSpec((rows, 2 * n_qk), lambda i: (i, 0)),
                  pl.BlockSpec((rows, 1024), lambda i: (i, 3)),
                  tab_spec, tab_spec, tab_spec, tab_spec, tab_spec,
                  pl.BlockSpec((1, MLA_Q_LORA), lambda i: (0, 0)),
                  pl.BlockSpec((1, MLA_KV_LORA), lambda i: (0, 0))],
        out_specs=[pl.BlockSpec((rows, n_qk), lambda i: (i, 0)),
                   pl.BlockSpec((rows, n_qk), lambda i: (i, 0)),
                   pl.BlockSpec((rows, MLA_Q_LORA), lambda i: (i, 0)),
                   pl.BlockSpec((rows, MLA_KV_LORA), lambda i: (i, 0)),
                   pl.BlockSpec((rows, 128), lambda i: (i, 0))],
        out_shape=[jax.ShapeDtypeStruct((t, n_qk), F32),
                   jax.ShapeDtypeStruct((t, n_qk), F32),
                   jax.ShapeDtypeStruct((t, MLA_Q_LORA), MXU_DTYPE),
                   jax.ShapeDtypeStruct((t, MLA_KV_LORA), MXU_DTYPE),
                   jax.ShapeDtypeStruct((t, 128), MXU_DTYPE)],
        compiler_params=_params(1),
        name="even_prep",
    )(h, h, *tables, q_norm.reshape(1, -1), kv_norm.reshape(1, -1))


def _mla_prep_kernel(q_ref, kn_ref, kr_ref, cos_b_ref, sin_lo_ref, sin_hi_ref, qm_ref, km_ref):
    cos_b, sin_lo, sin_hi = cos_b_ref[...], sin_lo_ref[...], sin_hi_ref[...]
    kr = kr_ref[...]
    for hd in range(B_HEADS):
        lo = hd * MLA_QK_PAD
        qm_ref[:, lo:lo + MLA_NOPE] = (q_ref[:, lo:lo + MLA_NOPE] * MLA_Q_SCALE).astype(qm_ref.dtype)
        q_rope = _rope64(q_ref[:, lo + MLA_NOPE:lo + MLA_QK_PAD], cos_b, sin_lo, sin_hi)
        qm_ref[:, lo + MLA_NOPE:lo + MLA_QK_PAD] = (q_rope * MLA_Q_SCALE).astype(qm_ref.dtype)
        km_ref[:, lo:lo + MLA_NOPE] = kn_ref[:, hd * MLA_NOPE:(hd + 1) * MLA_NOPE]
        km_ref[:, lo + MLA_NOPE:lo + MLA_QK_PAD] = kr


def _mla_prep(qf, kvf, kr, tables, seq, rows=256):
    t = qf.shape[0]
    w = B_HEADS * MLA_QK_PAD
    sb = seq // rows
    tab_spec = pl.BlockSpec((rows, 128), lambda i: (i % sb, 0))
    return pl.pallas_call(
        _mla_prep_kernel,
        grid=(t // rows,),
        in_specs=[pl.BlockSpec((rows, w), lambda i: (i, 0)),
                  pl.BlockSpec((rows, B_HEADS * MLA_NOPE), lambda i: (i, 0)),
                  pl.BlockSpec((rows, 128), lambda i: (i, 0)),
                  tab_spec, tab_spec, tab_spec],
        out_specs=[pl.BlockSpec((rows, w), lambda i: (i, 0)), pl.BlockSpec((rows, w), lambda i: (i, 0))],
        out_shape=[jax.ShapeDtypeStruct((t, w), MXU_DTYPE), jax.ShapeDtypeStruct((t, w), MXU_DTYPE)],
        compiler_params=_params(1),
        name="mla_prep",
    )(qf, kvf, kr, *tables)


def _dilated_kernel(q_ref, kc_ref, kp_ref, vc_ref, vp_ref, o_ref,
                    kbuf, vbuf, o0, o1, o2, l0, l1, l2):
    c = pl.program_id(2)
    kbuf[:DIL_CHUNK, :] = kp_ref[...]
    kbuf[DIL_CHUNK:, :] = kc_ref[...]
    vbuf[:DIL_CHUNK, :] = vp_ref[...]
    vbuf[DIL_CHUNK:, :] = vc_ref[...]
    scale = HEAD_DIM ** -0.5
    row = lax.broadcasted_iota(jnp.int32, (BLOCK, 2 * BLOCK), 0)
    col = lax.broadcasted_iota(jnp.int32, (BLOCK, 2 * BLOCK), 1)
    band = (col >= row) & (col <= row + DIL_STEPS)

    def blocks(specs):
        staged = []
        for q_start, k_start, dil, first, o_g, l_g in specs:
            if dil == 1:
                q_idx, k_idx = pl.ds(q_start, BLOCK), pl.ds(k_start, 2 * BLOCK)
            else:
                q_idx = pl.ds(q_start, BLOCK, stride=dil)
                k_idx = pl.ds(k_start, 2 * BLOCK, stride=dil)
            q = q_ref[q_idx, :].astype(MXU_DTYPE)
            k = kbuf[k_idx, :].astype(MXU_DTYPE)
            v = vbuf[k_idx, :].astype(MXU_DTYPE)
            logits = lax.dot_general(q, k, NT_DIMS, preferred_element_type=F32) * scale
            staged.append((logits, v, q_idx, first, o_g, l_g))
        weighted = []
        for logits, v, q_idx, first, o_g, l_g in staged:
            valid_from = jnp.where(first, BLOCK, 0)
            logits = jnp.where(band & (col >= valid_from), logits, NEG_INF)
            m = jnp.max(logits, axis=-1, keepdims=True)
            p = jnp.exp(logits - m)
            denom = jnp.sum(p, axis=-1, keepdims=True)
            weighted.append(((p / denom).astype(MXU_DTYPE), m + jnp.log(denom)))
        for (logits, v, q_idx, first, o_g, l_g), (pn, lse) in zip(staged, weighted):
            o_g[q_idx, :] = jnp.dot(pn, v, preferred_element_type=F32)
            l_g[q_idx, :] = jnp.broadcast_to(lse, (BLOCK, HEAD_DIM))

    n_blocks = DIL_CHUNK // BLOCK
    per_trip = 2

    def body(j, carry):
        specs = []
        for u in range(per_trip):
            i = j * per_trip + u
            specs.append((pl.multiple_of(i * BLOCK, BLOCK),
                          pl.multiple_of(DIL_CHUNK + (i - 1) * BLOCK, BLOCK),
                          DILATIONS[0], (c == 0) & (i == 0), o0, l0))
            d1 = DILATIONS[1]
            n, r = i // d1, i % d1
            span = BLOCK * d1
            specs.append((n * span + r, DIL_CHUNK + (n - 1) * span + r, d1, (c == 0) & (n == 0), o1, l1))
            specs.append((i, DIL_CHUNK - BLOCK * DILATIONS[2] + i, DILATIONS[2], c == 0, o2, l2))
        blocks(specs)
        return carry

    lax.fori_loop(0, n_blocks // per_trip, body, 0)

    la, lb, lc = l0[...], l1[...], l2[...]
    mx = jnp.maximum(jnp.maximum(la, lb), lc)
    ea, eb, ec = jnp.exp(la - mx), jnp.exp(lb - mx), jnp.exp(lc - mx)
    merged = (ea * o0[...] + eb * o1[...] + ec * o2[...]) / (ea + eb + ec)
    o_ref[...] = merged.astype(o_ref.dtype)


def _dilated_attention(qa, ka, h, batch, seq):
    t = qa.shape[0]
    nc = seq // DIL_CHUNK
    v_col0 = 2 * A_HEADS
    cur = lambda b, hd, c: (b * nc + c, hd)
    prev = lambda b, hd, c: (b * nc + jnp.maximum(c - 1, 0), hd)
    blk = (DIL_CHUNK, HEAD_DIM)
    return pl.pallas_call(
        _dilated_kernel,
        grid=(batch, A_HEADS, nc),
        in_specs=[pl.BlockSpec(blk, cur),
                  pl.BlockSpec(blk, cur), pl.BlockSpec(blk, prev),
                  pl.BlockSpec(blk, lambda b, hd, c: (b * nc + c, v_col0 + hd)),
                  pl.BlockSpec(blk, lambda b, hd, c: (b * nc + jnp.maximum(c - 1, 0), v_col0 + hd))],
        out_specs=pl.BlockSpec(blk, cur),
        out_shape=jax.ShapeDtypeStruct((t, A_HEADS * HEAD_DIM), MXU_DTYPE),
        scratch_shapes=[pltpu.VMEM((2 * DIL_CHUNK, HEAD_DIM), F32)] * 2
                       + [pltpu.VMEM(blk, F32)] * 6,
        compiler_params=_params(3),
        name="dilated_attention",
    )(qa, ka, ka, h, h)


def _mla_kernel(q_ref, k_ref, v_ref, o_ref, m_ref, l_ref, acc_ref, *, bq, bk, unroll):
    qi = pl.program_id(2)
    n_sub = bq // bk
    m_ref[...] = jnp.full_like(m_ref, NEG_INF)
    l_ref[...] = jnp.zeros_like(l_ref)
    acc_ref[...] = jnp.zeros_like(acc_ref)

    def group(k_starts, r0, diagonal):
        cols = bq - r0
        q = q_ref[r0:, :]
        ss = [lax.dot_general(k_ref[pl.ds(ks, bk), :], q, NT_DIMS, preferred_element_type=F32)
              for ks in k_starts]
        if diagonal:
            key = lax.broadcasted_iota(jnp.int32, (bk, cols), 0)
            qry = lax.broadcasted_iota(jnp.int32, (bk, cols), 1)
            ss = [jnp.where(key <= qry, s, NEG_INF) for s in ss]
        m_prev = m_ref[:, r0:]
        m_new = m_prev
        for s in ss:
            m_new = jnp.maximum(m_new, jnp.max(s, axis=0, keepdims=True))
        alpha = jnp.exp2(m_prev - m_new)
        l = alpha * l_ref[:, r0:]
        acc = alpha * acc_ref[:, r0:]
        for ks, s in zip(k_starts, ss):
            p = jnp.exp2(s - m_new)
            l = l + jnp.sum(p, axis=0, keepdims=True)
            acc = acc + lax.dot_general(v_ref[pl.ds(ks, bk), :], p.astype(MXU_DTYPE), TN_DIMS,
                                        preferred_element_type=F32)
        m_ref[:, r0:] = m_new
        l_ref[:, r0:] = l
        acc_ref[:, r0:] = acc

    def body(j, carry):
        group([pl.multiple_of((j * unroll + u) * bk, bk) for u in range(unroll)], 0, False)
        return carry

    assert n_sub % unroll == 0
    lax.fori_loop(0, qi * (n_sub // unroll), body, 0)
    for jj in range(n_sub):
        group([pl.multiple_of(qi * bq + jj * bk, bk)], jj * bk, True)
    o_ref[...] = (acc_ref[...] / l_ref[...]).T.astype(o_ref.dtype)


def _mla_attention(qm, km, kvf, batch, seq, bq=1024, bk=512, unroll=2):
    t = qm.shape[0]
    bq, bk = min(bq, seq), min(bk, seq)
    nq = seq // bq
    return pl.pallas_call(
        functools.partial(_mla_kernel, bq=bq, bk=bk, unroll=min(unroll, bq // bk)),
        grid=(batch, B_HEADS, nq),
        in_specs=[pl.BlockSpec((bq, MLA_QK_PAD), lambda b, hd, i: (b * nq + i, hd)),
                  pl.BlockSpec((seq, MLA_QK_PAD), lambda b, hd, i: (b, hd)),
                  pl.BlockSpec((seq, MLA_V), lambda b, hd, i: (b, B_HEADS + hd))],
        out_specs=pl.BlockSpec((bq, MLA_V), lambda b, hd, i: (b * nq + i, hd)),
        out_shape=jax.ShapeDtypeStruct((t, B_HEADS * MLA_V), MXU_DTYPE),
        scratch_shapes=[pltpu.VMEM((1, bq), F32), pltpu.VMEM((1, bq), F32),
                        pltpu.VMEM((MLA_V, bq), F32)],
        compiler_params=_params(3),
        name="mla_attention",
    )(qm, km, kvf)


def _stick_kernel(q_ref, k_ref, v_ref, o_ref, acc_ref, run_ref, *, bq, bk, unroll):
    qi = pl.program_id(2)
    n_sub = bq // bk
    tri_r = lax.broadcasted_iota(jnp.int32, (bk, bk), 0)
    tri_c = lax.broadcasted_iota(jnp.int32, (bk, bk), 1)
    at_or_after = jnp.where(tri_r >= tri_c, 1.0, 0.0).astype(MXU_DTYPE)
    acc_ref[...] = jnp.zeros_like(acc_ref)
    run_ref[...] = jnp.zeros_like(run_ref)

    def group(tiles):
        staged = []
        for ks, r0, diagonal in tiles:
            rows = bq - r0
            z = lax.dot_general(q_ref[r0:, :], k_ref[pl.ds(ks, bk), :], NT_DIMS,
                                preferred_element_type=F32)
            neg_abs = lax.bitcast_convert_type(
                lax.bitcast_convert_type(z, jnp.uint32) | jnp.uint32(0x80000000), F32)
            sp = jnp.maximum(z, 0.0) + jnp.log(1.0 + jnp.exp2(neg_abs)) * LOG2_E
            mask = None
            if diagonal:
                row = lax.broadcasted_iota(jnp.int32, (rows, bk), 0)
                col = lax.broadcasted_iota(jnp.int32, (rows, bk), 1)
                mask = col < row
                sp = jnp.where(mask, sp, 0.0)
            staged.append((z, sp, mask))
        sufs = []
        for _, sp, _ in staged:
            hi = sp.astype(MXU_DTYPE)
            lo = (sp - hi.astype(F32)).astype(MXU_DTYPE)
            sufs.append(jnp.dot(hi, at_or_after, preferred_element_type=F32)
                        + jnp.dot(lo, at_or_after, preferred_element_type=F32))
        run = run_ref[...]
        acc = acc_ref[...]
        for (ks, r0, _), (z, _, mask), suf in zip(tiles, staged, sufs):
            att = jnp.exp2(z - suf - run[r0:])
            if mask is not None:
                att = jnp.where(mask, att, 0.0)
            part = jnp.dot(att.astype(MXU_DTYPE), v_ref[pl.ds(ks, bk), :], preferred_element_type=F32)
            total = suf[:, :1]
            if r0 == 0:
                acc, run = acc + part, run + total
            else:
                acc = jnp.concatenate([acc[:r0], acc[r0:] + part], axis=0)
                run = jnp.concatenate([run[:r0], run[r0:] + total], axis=0)
        acc_ref[...] = acc
        run_ref[...] = run

    group([(pl.multiple_of(qi * bq + jj * bk, bk), jj * bk, True) for jj in reversed(range(n_sub))])

    def body(j, carry):
        group([(pl.multiple_of((qi * n_sub - 1 - j * unroll - u) * bk, bk), 0, False)
               for u in range(unroll)])
        return carry

    assert n_sub % unroll == 0
    lax.fori_loop(0, qi * (n_sub // unroll), body, 0)
    o_ref[...] = acc_ref[...].astype(o_ref.dtype)


def _stick_attention(qkv, batch, seq, bq=1024, bk=256, unroll=4):
    t = qkv.shape[0]
    bq, bk = min(bq, seq), min(bk, seq)
    nq = seq // bq
    return pl.pallas_call(
        functools.partial(_stick_kernel, bq=bq, bk=bk, unroll=min(unroll, bq // bk)),
        grid=(batch, C_HEADS, nq),
        in_specs=[pl.BlockSpec((bq, HEAD_DIM), lambda b, hd, i: (b * nq + i, hd)),
                  pl.BlockSpec((seq, HEAD_DIM), lambda b, hd, i: (b, C_HEADS + hd)),
                  pl.BlockSpec((seq, HEAD_DIM), lambda b, hd, i: (b, 2 * C_HEADS + hd))],
        out_specs=pl.BlockSpec((bq, HEAD_DIM), lambda b, hd, i: (b * nq + i, hd)),
        out_shape=jax.ShapeDtypeStruct((t, C_HEADS * HEAD_DIM), MXU_DTYPE),
        scratch_shapes=[pltpu.VMEM((bq, HEAD_DIM), F32), pltpu.VMEM((bq, 1), F32)],
        compiler_params=_params(3),
        name="stick_breaking_attention",
    )(qkv, qkv, qkv)


def _top_values(s, count, with_rank=False):
    vals = []
    cur = s
    rank = jnp.full(s.shape, float(count), F32) if with_rank else None
    for k in range(count):
        m = jnp.max(cur, axis=0, keepdims=True)
        vals.append(m)
        hit = cur == m
        if with_rank:
            rank = jnp.where(hit, float(k), rank)
        cur = jnp.where(hit, NEG_INF, cur)
    return (vals, rank) if with_rank else vals


def _ranked(s, count):
    n = s.shape[0]
    rows = lax.broadcasted_iota(jnp.int32, s.shape, 0)
    vals = []
    cur = s
    rank = jnp.full(s.shape, float(count), F32)
    for k in range(count):
        m = jnp.max(cur, axis=0, keepdims=True)
        first = jnp.min(jnp.where(cur == m, rows, n), axis=0, keepdims=True)
        pick = rows == first
        vals.append(m)
        rank = jnp.where(pick, float(k), rank)
        cur = jnp.where(pick, NEG_INF, cur)
    return vals, rank


def _pair_sums(v1, v2):
    v1_all = jnp.concatenate(v1, axis=0)
    v2_all = jnp.concatenate(v2, axis=0)
    return jnp.concatenate([v1[0] + v2_all]
                           + [v1[p] + v2_all[:8] for p in range(1, 8)]
                           + [v2[0] + v1_all[8:]], axis=0)


def _count_at_least(s, threshold):
    return jnp.sum(jnp.where(s >= threshold, 1.0, 0.0), axis=0, keepdims=True)


def _route_distinct(s1, s2):
    v1 = _top_values(s1, PEER_TOPK)
    v2, rank2 = _top_values(s2, PEER_TOPK, with_rank=True)
    cand = _pair_sums(v1, v2)
    tops = _top_values(cand, PEER_TOPK + 1)
    cut = 0.5 * (tops[PEER_TOPK - 1] + tops[PEER_TOPK])
    top = tops[0]
    z = jnp.sum(jnp.where(cand >= cut, jnp.exp(cand - top), 0.0), axis=0, keepdims=True)
    need = cut - s1
    count = jnp.zeros_like(s1)
    for q in range(PEER_TOPK):
        count = count + jnp.where(v2[q] >= need, 1.0, 0.0)
    a = jnp.where(s1 >= v1[PEER_TOPK - 1], jnp.exp(s1 - v1[0]) / (2.0 * z), 0.0)
    b = jnp.where(s2 >= v2[PEER_TOPK - 1], jnp.exp(s2 - v2[0]), 0.0)
    distinct = ((_count_at_least(s1, v1[PEER_TOPK - 1]) == PEER_TOPK)
                & (_count_at_least(s2, v2[PEER_TOPK - 1]) == PEER_TOPK)
                & (_count_at_least(cand, tops[PEER_TOPK]) == PEER_TOPK + 1))
    return rank2, b, count, a, distinct


def _route_with_ties(s1, s2):
    v1, rank1 = _ranked(s1, PEER_TOPK)
    v2, rank2 = _ranked(s2, PEER_TOPK)
    cand = _pair_sums(v1, v2)
    _, cand_rank = _ranked(cand, PEER_TOPK)
    chosen = cand_rank < PEER_TOPK
    top = v1[0] + v2[0]
    z = jnp.sum(jnp.where(chosen, jnp.exp(cand - top), 0.0), axis=0, keepdims=True)
    picked = jnp.where(chosen, 1.0, 0.0)
    per_p = [jnp.sum(picked[:16], axis=0, keepdims=True)]
    per_p += [jnp.sum(picked[8 + 8 * p:16 + 8 * p], axis=0, keepdims=True) for p in range(1, 8)]
    per_p += [picked[64 + p:65 + p] for p in range(8, 16)]
    count = jnp.zeros_like(s1)
    for p in range(PEER_TOPK):
        count = count + jnp.where(rank1 == p, per_p[p], 0.0)
    a = jnp.where(rank1 < PEER_TOPK, jnp.exp(s1 - v1[0]) / (2.0 * z), 0.0)
    b = jnp.where(rank2 < PEER_TOPK, jnp.exp(s2 - v2[0]), 0.0)
    return rank2, b, count, a


def _peer_route_kernel(q_ref, keys_ref, r2_ref, b_ref, c_ref, a_ref):
    def head(hd, carry):
        col = pl.multiple_of(hd * 2 * PEER_HALF, 2 * PEER_HALF)
        q1 = q_ref[:, pl.ds(col, PEER_HALF)].astype(MXU_DTYPE)
        q2 = q_ref[:, pl.ds(col + PEER_HALF, PEER_HALF)].astype(MXU_DTYPE)
        s1 = lax.dot_general(keys_ref[hd, 0], q1, NT_DIMS, preferred_element_type=F32)
        s2 = lax.dot_general(keys_ref[hd, 1], q2, NT_DIMS, preferred_element_type=F32)

        def store(rank2, b, count, a):
            groups = pl.ds(pl.multiple_of(hd * KEY_GROUPS, KEY_GROUPS), KEY_GROUPS)
            tokens = rank2.shape[1]
            r2_ref[groups] = rank2.astype(r2_ref.dtype).reshape(KEY_GROUPS, 16, tokens)
            b_ref[groups] = b.astype(b_ref.dtype).reshape(KEY_GROUPS, 16, tokens)
            c_ref[hd] = count
            a_ref[hd] = a

        rank2, b, count, a, distinct = _route_distinct(s1, s2)
        store(rank2, b, count, a)

        @pl.when(jnp.max(jnp.where(distinct, 0.0, 1.0)) > 0.0)
        def _():
            store(*_route_with_ties(s1, s2))

        return carry

    lax.fori_loop(0, PEER_HEADS, head, 0)


def _peer_route(qp, keys, tr=256):
    t = qp.shape[0]
    shape = (PEER_HEADS, PEER_N_KEYS, t)
    spec = pl.BlockSpec((PEER_HEADS, PEER_N_KEYS, tr), lambda i: (0, 0, i))
    flat = (PEER_HEADS * KEY_GROUPS, 16, t)
    flat_spec = pl.BlockSpec((PEER_HEADS * KEY_GROUPS, 16, tr), lambda i: (0, 0, i))
    return pl.pallas_call(
        _peer_route_kernel,
        grid=(t // tr,),
        in_specs=[pl.BlockSpec((tr, qp.shape[1]), lambda i: (i, 0)),
                  pl.BlockSpec(keys.shape, lambda i: (0, 0, 0, 0))],
        out_specs=[flat_spec, flat_spec, spec, spec],
        out_shape=[jax.ShapeDtypeStruct(flat, MXU_DTYPE), jax.ShapeDtypeStruct(flat, MXU_DTYPE),
                   jax.ShapeDtypeStruct(shape, F32), jax.ShapeDtypeStruct(shape, F32)],
        compiler_params=_params(1),
        name="peer_route",
    )(qp, keys)


def _twice_gelu(x):
    return x * (1.0 + lax.erf(x * (2.0 ** -0.5)))


def _row_tile(row, rows):
    packed = jnp.broadcast_to(row, (16, 128)).astype(MXU_DTYPE)
    return jnp.concatenate([packed] * (rows // 16), axis=0)


def _peer_kernel(xt_ref, res_ref, u_ref, v_ref, r2_ref, b_ref, c_ref, a_ref, g_ref, bias_ref,
                 o_ref, ob_ref, st_ref, wt_ref, *, n_i, tb, alpha):
    e = pl.program_id(1)

    @pl.when(e == 0)
    def _():
        o_ref[...] = jnp.zeros_like(o_ref)

    st_ref[...] = jnp.dot(u_ref[...], xt_ref[...], preferred_element_type=F32)
    zero = jnp.zeros((PEER_N_KEYS, 128), MXU_DTYPE)
    assert 8 % n_i == 0
    per_group = 8 // n_i
    within = e % per_group

    def key_rows(ref, hd, lanes):
        grp = ref[hd, 0, :, lanes]
        rows = grp[:n_i]
        for k in range(1, per_group):
            rows = jnp.where(within == k, grp[k * n_i:(k + 1) * n_i], rows)
        return rows

    live = min(n_i, 4)
    for l in range(tb // 128):
        lanes = slice(l * 128, (l + 1) * 128)
        for i0 in range(0, n_i, live):
            gates = [zero for _ in range(live)]
            for hd in range(PEER_HEADS):
                r2 = jnp.concatenate([r2_ref[hd * KEY_GROUPS + g, :, lanes] for g in range(KEY_GROUPS)],
                                     axis=0)
                b = jnp.concatenate([b_ref[hd * KEY_GROUPS + g, :, lanes] for g in range(KEY_GROUPS)],
                                    axis=0)
                counts = key_rows(c_ref, hd, lanes)
                a_rows = key_rows(a_ref, hd, lanes)
                for k in range(live):
                    count = _row_tile(counts[i0 + k:i0 + k + 1], PEER_N_KEYS)
                    a = _row_tile(a_rows[i0 + k:i0 + k + 1], PEER_N_KEYS)
                    gates[k] = gates[k] + jnp.where(r2 < count, b, zero) * a
            for k in range(live):
                rows = slice((i0 + k) * PEER_N_KEYS, (i0 + k + 1) * PEER_N_KEYS)
                act = _twice_gelu(st_ref[rows, lanes]).astype(MXU_DTYPE)
                wt_ref[rows, lanes] = act * gates[k]
    o_ref[...] += lax.dot_general(wt_ref[...], v_ref[...], TN_DIMS, preferred_element_type=F32)

    @pl.when(e == pl.num_programs(1) - 1)
    def _():
        y = _layer_norm_rows(alpha * res_ref[...] + o_ref[...], g_ref[...], bias_ref[...])
        o_ref[...] = y
        ob_ref[...] = y.astype(ob_ref.dtype)


def _peer_experts(xt, res, u, v, r2, b, c, a, gain, bias, alpha, tb=512, n_i=8):
    t, d = res.shape
    n_exp = u.shape[0]
    eb = n_i * PEER_N_KEYS
    tb = min(tb, t)
    tok = lambda i, e: (i, 0)
    per_group = 8 // n_i
    c = c.reshape(PEER_HEADS, PEER_N_KEYS // 8, 8, t)
    a = a.reshape(PEER_HEADS, PEER_N_KEYS // 8, 8, t)
    full = pl.BlockSpec((PEER_HEADS, 1, 8, tb), lambda i, e: (0, e // per_group, 0, i))
    flat = pl.BlockSpec((PEER_HEADS * KEY_GROUPS, 16, tb), lambda i, e: (0, 0, i))
    return pl.pallas_call(
        functools.partial(_peer_kernel, n_i=n_i, tb=tb, alpha=alpha),
        grid=(t // tb, n_exp // eb),
        in_specs=[pl.BlockSpec((d, tb), lambda i, e: (0, i)), pl.BlockSpec((tb, d), tok),
                  pl.BlockSpec((eb, d), lambda i, e: (e, 0)),
                  pl.BlockSpec((eb, d), lambda i, e: (e, 0)),
                  flat, flat, full, full,
                  pl.BlockSpec((1, d), lambda i, e: (0, 0)), pl.BlockSpec((1, d), lambda i, e: (0, 0))],
        out_specs=[pl.BlockSpec((tb, d), tok), pl.BlockSpec((tb, d), tok)],
        out_shape=[jax.ShapeDtypeStruct((t, d), F32), jax.ShapeDtypeStruct((t, d), MXU_DTYPE)],
        scratch_shapes=[pltpu.VMEM((eb, tb), F32), pltpu.VMEM((eb, tb), MXU_DTYPE)],
        compiler_params=_params(2),
        name="peer_experts",
    )(xt, res, u, v, r2, b, c, a, gain.reshape(1, d), bias.reshape(1, d))


def _peer_layer(x, xb, w_query, keys, u, v, gain, bias, alpha):
    qp = _matmul(xb, w_query, F32)
    r2, b, c, a = _peer_route(qp, keys)
    return _peer_experts(xb.T, x, u, v, r2, b, c, a, gain, bias, alpha)


def _even_mixer(x2d, batch, seq, w_in, q_norm, w_q_b, kv_norm, w_kv_b, tables):
    d = x2d.shape[1]
    w_in_p = jnp.concatenate(
        [w_in, jnp.zeros((d, 4096 - w_in.shape[1]), w_in.dtype)], axis=1).astype(MXU_DTYPE)
    h = _matmul(x2d, w_in_p, F32)
    cos_a, sin_a, cos_b, sin_lo, sin_hi = tables
    qa, ka, cqn, ckvn, kr = _even_prep(h, tables, q_norm, kv_norm, seq)
    out_a = _dilated_attention(qa, ka, h, batch, seq)
    wq = w_q_b.reshape(MLA_Q_LORA, B_HEADS, MLA_NOPE + MLA_ROPE)
    wq = jnp.pad(wq, ((0, 0), (0, 0), (0, MLA_QK_PAD - MLA_NOPE - MLA_ROPE)))
    wq = wq.reshape(MLA_Q_LORA, B_HEADS * MLA_QK_PAD).astype(MXU_DTYPE)
    wkv = w_kv_b.reshape(MLA_KV_LORA, B_HEADS, MLA_NOPE + MLA_V)
    wkv = jnp.concatenate([wkv[:, :, :MLA_NOPE].reshape(MLA_KV_LORA, -1),
                           wkv[:, :, MLA_NOPE:].reshape(MLA_KV_LORA, -1)], axis=1).astype(MXU_DTYPE)
    qf = _matmul(cqn, wq, F32)
    kvf = _matmul(ckvn, wkv, MXU_DTYPE)
    qm, km = _mla_prep(qf, kvf, kr, (cos_b, sin_lo, sin_hi), seq)
    out_b = _mla_attention(qm, km, kvf, batch, seq)
    return out_a, out_b


def kernel(x, a_w_in, b_q_norm, b_w_q_b, b_kv_norm, b_w_kv_b, ab_w_out, c_w_in, c_w_out,
           peer_w_query, peer_sub_keys, peer_u, peer_v, ln_gain, ln_bias):
    batch, seq, d = x.shape
    depth = peer_u.shape[0]
    alpha = (2 * depth) ** 0.25
    tables = _rope_tables(seq)
    u_all, v_all = peer_u.astype(MXU_DTYPE), peer_v.astype(MXU_DTYPE)
    xf = x.reshape(batch * seq, d)
    xb = None
    for layer in range(depth):
        i = layer // 2
        if layer % 2 == 0:
            src = xf if xb is None else xb
            out_a, out_b = _even_mixer(src, batch, seq, a_w_in[i], b_q_norm[i], b_w_q_b[i],
                                       b_kv_norm[i], b_w_kv_b[i], tables)
            w_out = ab_w_out[i].astype(MXU_DTYPE)
            half = A_HEADS * HEAD_DIM
            xf, xb = _matmul_residual_ln([out_a, out_b], [w_out[:half], w_out[half:]], xf,
                                         ln_gain[layer, 0], ln_bias[layer, 0], alpha)
        else:
            src = xf if xb is None else xb
            n_q = C_HEADS * HEAD_DIM
            w_qkv = jnp.concatenate([c_w_in[i][:, :n_q] * (HEAD_DIM ** -0.5 * LOG2_E),
                                     c_w_in[i][:, n_q:]], axis=1).astype(MXU_DTYPE)
            qkv = _matmul(src, w_qkv, MXU_DTYPE)
            o = _stick_attention(qkv, batch, seq)
            xf, xb = _matmul_residual_ln([o], [c_w_out[i].astype(MXU_DTYPE)], xf,
                                         ln_gain[layer, 0], ln_bias[layer, 0], alpha)
        xf, xb = _peer_layer(xf, xb, peer_w_query[layer].astype(MXU_DTYPE),
                         peer_sub_keys[layer].astype(MXU_DTYPE),
                         u_all[layer], v_all[layer],
                         ln_gain[layer, 1], ln_bias[layer, 1], alpha)
    return xf.reshape(batch, seq, d)
```

```python
import functools

import jax
import jax.numpy as jnp
from jax import lax
from jax.experimental import pallas as pl
from jax.experimental.pallas import tpu as pltpu

F32 = jnp.float32
MXU_DTYPE = jnp.bfloat16

HEAD_DIM = 128
BLOCK = 128
DIL_STEPS = 128
DILATIONS = (1, 4, 16)
DIL_CHUNK = BLOCK * DILATIONS[-1]
A_HEADS = 8
B_HEADS = 8
MLA_Q_LORA = 512
MLA_KV_LORA = 256
MLA_NOPE = 128
MLA_ROPE = 64
MLA_V = 128
MLA_QK_PAD = 256
C_HEADS = 16
PEER_HEADS = 8
PEER_N_KEYS = 128
PEER_TOPK = 16
PEER_HALF = 128
KEY_GROUPS = PEER_N_KEYS // 16
ROPE_THETA = 10000.0
LN_EPS = 1e-5
RMS_EPS = 1e-6
NEG_INF = -1e30
LOG2_E = 1.4426950408889634
MLA_Q_SCALE = (MLA_NOPE + MLA_ROPE) ** -0.5 * LOG2_E

V7X_VMEM_LIMIT_BYTES = 56 * 1024 * 1024
NT_DIMS = (((1,), (1,)), ((), ()))
TN_DIMS = (((0,), (0,)), ((), ()))


def _params(n_axes):
    return pltpu.CompilerParams(dimension_semantics=("arbitrary",) * n_axes,
                                vmem_limit_bytes=V7X_VMEM_LIMIT_BYTES)


def _mm_kernel(a_ref, b_ref, o_ref):
    a = a_ref[...].astype(MXU_DTYPE)
    o_ref[...] = jnp.dot(a, b_ref[...], preferred_element_type=F32).astype(o_ref.dtype)


def _matmul(a, b, out_dtype, bm=1024, bn=1024):
    m, k = a.shape
    n = b.shape[1]
    bm, bn = min(bm, m), min(bn, n)
    assert m % bm == 0 and n % bn == 0
    return pl.pallas_call(
        _mm_kernel,
        grid=(m // bm, n // bn),
        in_specs=[pl.BlockSpec((bm, k), lambda i, j: (i, 0)),
                  pl.BlockSpec((k, bn), lambda i, j: (0, j))],
        out_specs=pl.BlockSpec((bm, bn), lambda i, j: (i, j)),
        out_shape=jax.ShapeDtypeStruct((m, n), out_dtype),
        compiler_params=_params(2),
        name="matmul",
    )(a, b)


def _layer_norm_rows(y, g, b):
    mu = jnp.mean(y, axis=-1, keepdims=True)
    d = y - mu
    var = jnp.mean(d * d, axis=-1, keepdims=True)
    return d * lax.rsqrt(var + LN_EPS) * g + b


def _mm_ln_kernel(*refs, n_pairs, alpha):
    a_refs = refs[:n_pairs]
    w_refs = refs[n_pairs:2 * n_pairs]
    res_ref, g_ref, b_ref, o_ref, ob_ref = refs[2 * n_pairs:]
    acc = jnp.dot(a_refs[0][...].astype(MXU_DTYPE), w_refs[0][...], preferred_element_type=F32)
    for a_ref, w_ref in zip(a_refs[1:], w_refs[1:]):
        acc = acc + jnp.dot(a_ref[...].astype(MXU_DTYPE), w_ref[...], preferred_element_type=F32)
    y = _layer_norm_rows(alpha * res_ref[...] + acc, g_ref[...], b_ref[...])
    o_ref[...] = y
    ob_ref[...] = y.astype(ob_ref.dtype)


def _matmul_residual_ln(a_list, w_list, res, gain, bias, alpha, bm=256):
    m, n = res.shape
    n_pairs = len(a_list)
    in_specs = [pl.BlockSpec((bm, a.shape[1]), lambda i: (i, 0)) for a in a_list]
    in_specs += [pl.BlockSpec(w.shape, lambda i: (0, 0)) for w in w_list]
    in_specs += [pl.BlockSpec((bm, n), lambda i: (i, 0)),
                 pl.BlockSpec((1, n), lambda i: (0, 0)),
                 pl.BlockSpec((1, n), lambda i: (0, 0))]
    return pl.pallas_call(
        functools.partial(_mm_ln_kernel, n_pairs=n_pairs, alpha=alpha),
        grid=(m // bm,),
        in_specs=in_specs,
        out_specs=[pl.BlockSpec((bm, n), lambda i: (i, 0)), pl.BlockSpec((bm, n), lambda i: (i, 0))],
        out_shape=[jax.ShapeDtypeStruct((m, n), F32), jax.ShapeDtypeStruct((m, n), MXU_DTYPE)],
        compiler_params=_params(1),
        name="matmul_residual_ln",
    )(*a_list, *w_list, res, gain.reshape(1, n), bias.reshape(1, n))


def _rope_tables(seq):
    pos = jnp.arange(seq, dtype=F32)[:, None]

    def cos_sin(half):
        inv = ROPE_THETA ** (-jnp.arange(half, dtype=F32) / half)
        ang = pos * inv[None, :]
        return jnp.cos(ang), jnp.sin(ang)

    c, s = cos_sin(HEAD_DIM // 2)
    cos_a = jnp.concatenate([c, c], axis=1)
    sin_a = jnp.concatenate([-s, s], axis=1)
    c, s = cos_sin(MLA_ROPE // 2)
    z32 = jnp.zeros_like(c)
    cos_b = jnp.concatenate([c, c, z32, z32], axis=1)
    sin_lo = jnp.concatenate([-s, z32, z32, z32], axis=1)
    sin_hi = jnp.concatenate([z32, s, z32, z32], axis=1)
    return cos_a, sin_a, cos_b, sin_lo, sin_hi


def _rope128(x, cos_a, sin_a):
    return x * cos_a + pltpu.roll(x, HEAD_DIM // 2, 1) * sin_a


def _rope64(x, cos_b, sin_lo, sin_hi):
    return x * cos_b + pltpu.roll(x, 96, 1) * sin_lo + pltpu.roll(x, 32, 1) * sin_hi


def _rms_norm_rows(x, g):
    return x * lax.rsqrt(jnp.mean(x * x, axis=-1, keepdims=True) + RMS_EPS) * g


def _even_prep_kernel(qk_ref, tail_ref, cos_a_ref, sin_a_ref, cos_b_ref, sin_lo_ref, sin_hi_ref,
                      qn_ref, kvn_ref, qa_ref, ka_ref, cq_ref, ckv_ref, kr_ref):
    cos_a, sin_a = cos_a_ref[...], sin_a_ref[...]
    n_qk = A_HEADS * HEAD_DIM
    for hd in range(A_HEADS):
        lo = hd * HEAD_DIM
        qa_ref[:, lo:lo + HEAD_DIM] = _rope128(qk_ref[:, lo:lo + HEAD_DIM], cos_a, sin_a)
        ka_ref[:, lo:lo + HEAD_DIM] = _rope128(qk_ref[:, n_qk + lo:n_qk + lo + HEAD_DIM], cos_a, sin_a)
    cq_ref[...] = _rms_norm_rows(tail_ref[:, :MLA_Q_LORA], qn_ref[...]).astype(cq_ref.dtype)
    o2 = MLA_Q_LORA + MLA_KV_LORA
    ckv_ref[...] = _rms_norm_rows(tail_ref[:, MLA_Q_LORA:o2], kvn_ref[...]).astype(ckv_ref.dtype)
    kr = _rope64(tail_ref[:, o2:o2 + 128], cos_b_ref[...], sin_lo_ref[...], sin_hi_ref[...])
    kr_ref[...] = kr.astype(kr_ref.dtype)


def _even_prep(h, tables, q_norm, kv_norm, seq, rows=256):
    t = h.shape[0]
    n_qk = A_HEADS * HEAD_DIM
    sb = seq // rows
    tab_spec = pl.BlockSpec((rows, 128), lambda i: (i % sb, 0))
    return pl.pallas_call(
        _even_prep_kernel,
        grid=(t // rows,),
        in_specs=[pl.BlockSpec((rows, 2 * n_qk), lambda i: (i, 0)),
                  pl.BlockSpec((rows, 1024), lambda i: (i, 3)),
                  tab_spec, tab_spec, tab_spec, tab_spec, tab_spec,
                  pl.BlockSpec((1, MLA_Q_LORA), lambda i: (0, 0)),
                  pl.BlockSpec((1, MLA_KV_LORA), lambda i: (0, 0))],
        out_specs=[pl.BlockSpec((rows, n_qk), lambda i: (i, 0)),
                   pl.BlockSpec((rows, n_qk), lambda i: (i, 0)),
                   pl.BlockSpec((rows, MLA_Q_LORA), lambda i: (i, 0)),
                   pl.BlockSpec((rows, MLA_KV_LORA), lambda i: (i, 0)),
                   pl.BlockSpec((rows, 128), lambda i: (i, 0))],
        out_shape=[jax.ShapeDtypeStruct((t, n_qk), F32),
                   jax.ShapeDtypeStruct((t, n_qk), F32),
                   jax.ShapeDtypeStruct((t, MLA_Q_LORA), MXU_DTYPE),
                   jax.ShapeDtypeStruct((t, MLA_KV_LORA), MXU_DTYPE),
                   jax.ShapeDtypeStruct((t, 128), MXU_DTYPE)],
        compiler_params=_params(1),
        name="even_prep",
    )(h, h, *tables, q_norm.reshape(1, -1), kv_norm.reshape(1, -1))


def _mla_prep_kernel(q_ref, kn_ref, kr_ref, cos_b_ref, sin_lo_ref, sin_hi_ref, qm_ref, km_ref):
    cos_b, sin_lo, sin_hi = cos_b_ref[...], sin_lo_ref[...], sin_hi_ref[...]
    kr = kr_ref[...]
    for hd in range(B_HEADS):
        lo = hd * MLA_QK_PAD
        qm_ref[:, lo:lo + MLA_NOPE] = (q_ref[:, lo:lo + MLA_NOPE] * MLA_Q_SCALE).astype(qm_ref.dtype)
        q_rope = _rope64(q_ref[:, lo + MLA_NOPE:lo + MLA_QK_PAD], cos_b, sin_lo, sin_hi)
        qm_ref[:, lo + MLA_NOPE:lo + MLA_QK_PAD] = (q_rope * MLA_Q_SCALE).astype(qm_ref.dtype)
        km_ref[:, lo:lo + MLA_NOPE] = kn_ref[:, hd * MLA_NOPE:(hd + 1) * MLA_NOPE]
        km_ref[:, lo + MLA_NOPE:lo + MLA_QK_PAD] = kr


def _mla_prep(qf, kvf, kr, tables, seq, rows=256):
    t = qf.shape[0]
    w = B_HEADS * MLA_QK_PAD
    sb = seq // rows
    tab_spec = pl.BlockSpec((rows, 128), lambda i: (i % sb, 0))
    return pl.pallas_call(
        _mla_prep_kernel,
        grid=(t // rows,),
        in_specs=[pl.BlockSpec((rows, w), lambda i: (i, 0)),
                  pl.BlockSpec((rows, B_HEADS * MLA_NOPE), lambda i: (i, 0)),
                  pl.BlockSpec((rows, 128), lambda i: (i, 0)),
                  tab_spec, tab_spec, tab_spec],
        out_specs=[pl.BlockSpec((rows, w), lambda i: (i, 0)), pl.BlockSpec((rows, w), lambda i: (i, 0))],
        out_shape=[jax.ShapeDtypeStruct((t, w), MXU_DTYPE), jax.ShapeDtypeStruct((t, w), MXU_DTYPE)],
        compiler_params=_params(1),
        name="mla_prep",
    )(qf, kvf, kr, *tables)


def _dilated_kernel(q_ref, kc_ref, kp_ref, vc_ref, vp_ref, o_ref,
                    kbuf, vbuf, o0, o1, o2, l0, l1, l2):
    c = pl.program_id(2)
    kbuf[:DIL_CHUNK, :] = kp_ref[...]
    kbuf[DIL_CHUNK:, :] = kc_ref[...]
    vbuf[:DIL_CHUNK, :] = vp_ref[...]
    vbuf[DIL_CHUNK:, :] = vc_ref[...]
    scale = HEAD_DIM ** -0.5
    row = lax.broadcasted_iota(jnp.int32, (BLOCK, 2 * BLOCK), 0)
    col = lax.broadcasted_iota(jnp.int32, (BLOCK, 2 * BLOCK), 1)
    band = (col >= row) & (col <= row + DIL_STEPS)

    def blocks(specs):
        staged = []
        for q_start, k_start, dil, first, o_g, l_g in specs:
            if dil == 1:
                q_idx, k_idx = pl.ds(q_start, BLOCK), pl.ds(k_start, 2 * BLOCK)
            else:
                q_idx = pl.ds(q_start, BLOCK, stride=dil)
                k_idx = pl.ds(k_start, 2 * BLOCK, stride=dil)
            q = q_ref[q_idx, :].astype(MXU_DTYPE)
            k = kbuf[k_idx, :].astype(MXU_DTYPE)
            v = vbuf[k_idx, :].astype(MXU_DTYPE)
            logits = lax.dot_general(q, k, NT_DIMS, preferred_element_type=F32) * scale
            staged.append((logits, v, q_idx, first, o_g, l_g))
        weighted = []
        for logits, v, q_idx, first, o_g, l_g in staged:
            valid_from = jnp.where(first, BLOCK, 0)
            logits = jnp.where(band & (col >= valid_from), logits, NEG_INF)
            m = jnp.max(logits, axis=-1, keepdims=True)
            p = jnp.exp(logits - m)
            denom = jnp.sum(p, axis=-1, keepdims=True)
            weighted.append(((p / denom).astype(MXU_DTYPE), m + jnp.log(denom)))
        for (logits, v, q_idx, first, o_g, l_g), (pn, lse) in zip(staged, weighted):
            o_g[q_idx, :] = jnp.dot(pn, v, preferred_element_type=F32)
            l_g[q_idx, :] = jnp.broadcast_to(lse, (BLOCK, HEAD_DIM))

    n_blocks = DIL_CHUNK // BLOCK
    per_trip = 2

    def body(j, carry):
        specs = []
        for u in range(per_trip):
            i = j * per_trip + u
            specs.append((pl.multiple_of(i * BLOCK, BLOCK),
                          pl.multiple_of(DIL_CHUNK + (i - 1) * BLOCK, BLOCK),
                          DILATIONS[0], (c == 0) & (i == 0), o0, l0))
            d1 = DILATIONS[1]
            n, r = i // d1, i % d1
            span = BLOCK * d1
            specs.append((n * span + r, DIL_CHUNK + (n - 1) * span + r, d1, (c == 0) & (n == 0), o1, l1))
            specs.append((i, DIL_CHUNK - BLOCK * DILATIONS[2] + i, DILATIONS[2], c == 0, o2, l2))
        blocks(specs)
        return carry

    lax.fori_loop(0, n_blocks // per_trip, body, 0)

    la, lb, lc = l0[...], l1[...], l2[...]
    mx = jnp.maximum(jnp.maximum(la, lb), lc)
    ea, eb, ec = jnp.exp(la - mx), jnp.exp(lb - mx), jnp.exp(lc - mx)
    merged = (ea * o0[...] + eb * o1[...] + ec * o2[...]) / (ea + eb + ec)
    o_ref[...] = merged.astype(o_ref.dtype)


def _dilated_attention(qa, ka, h, batch, seq):
    t = qa.shape[0]
    nc = seq // DIL_CHUNK
    v_col0 = 2 * A_HEADS
    cur = lambda b, hd, c: (b * nc + c, hd)
    prev = lambda b, hd, c: (b * nc + jnp.maximum(c - 1, 0), hd)
    blk = (DIL_CHUNK, HEAD_DIM)
    return pl.pallas_call(
        _dilated_kernel,
        grid=(batch, A_HEADS, nc),
        in_specs=[pl.BlockSpec(blk, cur),
                  pl.BlockSpec(blk, cur), pl.BlockSpec(blk, prev),
                  pl.BlockSpec(blk, lambda b, hd, c: (b * nc + c, v_col0 + hd)),
                  pl.BlockSpec(blk, lambda b, hd, c: (b * nc + jnp.maximum(c - 1, 0), v_col0 + hd))],
        out_specs=pl.BlockSpec(blk, cur),
        out_shape=jax.ShapeDtypeStruct((t, A_HEADS * HEAD_DIM), MXU_DTYPE),
        scratch_shapes=[pltpu.VMEM((2 * DIL_CHUNK, HEAD_DIM), F32)] * 2
                       + [pltpu.VMEM(blk, F32)] * 6,
        compiler_params=_params(3),
        name="dilated_attention",
    )(qa, ka, ka, h, h)


def _mla_kernel(q_ref, k_ref, v_ref, o_ref, m_ref, l_ref, acc_ref, *, bq, bk, unroll):
    qi = pl.program_id(2)
    n_sub = bq // bk
    m_ref[...] = jnp.full_like(m_ref, NEG_INF)
    l_ref[...] = jnp.zeros_like(l_ref)
    acc_ref[...] = jnp.zeros_like(acc_ref)

    def group(k_starts, r0, diagonal):
        cols = bq - r0
        q = q_ref[r0:, :]
        ss = [lax.dot_general(k_ref[pl.ds(ks, bk), :], q, NT_DIMS, preferred_element_type=F32)
              for ks in k_starts]
        if diagonal:
            key = lax.broadcasted_iota(jnp.int32, (bk, cols), 0)
            qry = lax.broadcasted_iota(jnp.int32, (bk, cols), 1)
            ss = [jnp.where(key <= qry, s, NEG_INF) for s in ss]
        m_prev = m_ref[:, r0:]
        m_new = m_prev
        for s in ss:
            m_new = jnp.maximum(m_new, jnp.max(s, axis=0, keepdims=True))
        alpha = jnp.exp2(m_prev - m_new)
        l = alpha * l_ref[:, r0:]
        acc = alpha * acc_ref[:, r0:]
        for ks, s in zip(k_starts, ss):
            p = jnp.exp2(s - m_new)
            l = l + jnp.sum(p, axis=0, keepdims=True)
            acc = acc + lax.dot_general(v_ref[pl.ds(ks, bk), :], p.astype(MXU_DTYPE), TN_DIMS,
                                        preferred_element_type=F32)
        m_ref[:, r0:] = m_new
        l_ref[:, r0:] = l
        acc_ref[:, r0:] = acc

    def body(j, carry):
        group([pl.multiple_of((j * unroll + u) * bk, bk) for u in range(unroll)], 0, False)
        return carry

    assert n_sub % unroll == 0
    lax.fori_loop(0, qi * (n_sub // unroll), body, 0)
    for jj in range(n_sub):
        group([pl.multiple_of(qi * bq + jj * bk, bk)], jj * bk, True)
    o_ref[...] = (acc_ref[...] / l_ref[...]).T.astype(o_ref.dtype)


def _mla_attention(qm, km, kvf, batch, seq, bq=1024, bk=512, unroll=2):
    t = qm.shape[0]
    bq, bk = min(bq, seq), min(bk, seq)
    nq = seq // bq
    return pl.pallas_call(
        functools.partial(_mla_kernel, bq=bq, bk=bk, unroll=min(unroll, bq // bk)),
        grid=(batch, B_HEADS, nq),
        in_specs=[pl.BlockSpec((bq, MLA_QK_PAD), lambda b, hd, i: (b * nq + i, hd)),
                  pl.BlockSpec((seq, MLA_QK_PAD), lambda b, hd, i: (b, hd)),
                  pl.BlockSpec((seq, MLA_V), lambda b, hd, i: (b, B_HEADS + hd))],
        out_specs=pl.BlockSpec((bq, MLA_V), lambda b, hd, i: (b * nq + i, hd)),
        out_shape=jax.ShapeDtypeStruct((t, B_HEADS * MLA_V), MXU_DTYPE),
        scratch_shapes=[pltpu.VMEM((1, bq), F32), pltpu.VMEM((1, bq), F32),
                        pltpu.VMEM((MLA_V, bq), F32)],
        compiler_params=_params(3),
        name="mla_attention",
    )(qm, km, kvf)


def _stick_kernel(q_ref, k_ref, v_ref, o_ref, acc_ref, run_ref, *, bq, bk, unroll):
    qi = pl.program_id(2)
    n_sub = bq // bk
    tri_r = lax.broadcasted_iota(jnp.int32, (bk, bk), 0)
    tri_c = lax.broadcasted_iota(jnp.int32, (bk, bk), 1)
    at_or_after = jnp.where(tri_c >= tri_r, 1.0, 0.0).astype(MXU_DTYPE)
    acc_ref[...] = jnp.zeros_like(acc_ref)
    run_ref[...] = jnp.zeros_like(run_ref)

    def group(tiles):
        staged = []
        for ks, c0, diagonal in tiles:
            cols = bq - c0
            z = lax.dot_general(k_ref[pl.ds(ks, bk), :], q_ref[c0:, :], NT_DIMS,
                                preferred_element_type=F32)
            neg_abs = lax.bitcast_convert_type(
                lax.bitcast_convert_type(z, jnp.uint32) | jnp.uint32(0x80000000), F32)
            sp = jnp.maximum(z, 0.0) + jnp.log(1.0 + jnp.exp2(neg_abs)) * LOG2_E
            mask = None
            if diagonal:
                key = lax.broadcasted_iota(jnp.int32, (bk, cols), 0)
                qry = lax.broadcasted_iota(jnp.int32, (bk, cols), 1)
                mask = key < qry
                sp = jnp.where(mask, sp, 0.0)
            staged.append((z, sp, mask))
        sufs = []
        for _, sp, _ in staged:
            hi = sp.astype(MXU_DTYPE)
            lo = (sp - hi.astype(F32)).astype(MXU_DTYPE)
            sufs.append(jnp.dot(at_or_after, hi, preferred_element_type=F32)
                        + jnp.dot(at_or_after, lo, preferred_element_type=F32))
        run = run_ref[...]
        acc = acc_ref[...]
        for (ks, c0, _), (z, _, mask), suf in zip(tiles, staged, sufs):
            att = jnp.exp2(z - suf - run[:, c0:])
            if mask is not None:
                att = jnp.where(mask, att, 0.0)
            part = lax.dot_general(v_ref[pl.ds(ks, bk), :], att.astype(MXU_DTYPE), TN_DIMS,
                                   preferred_element_type=F32)
            total = suf[:1, :]
            if c0 == 0:
                acc, run = acc + part, run + total
            else:
                acc = jnp.concatenate([acc[:, :c0], acc[:, c0:] + part], axis=1)
                run = jnp.concatenate([run[:, :c0], run[:, c0:] + total], axis=1)
        acc_ref[...] = acc
        run_ref[...] = run

    group([(pl.multiple_of(qi * bq + jj * bk, bk), jj * bk, True) for jj in reversed(range(n_sub))])

    def body(j, carry):
        group([(pl.multiple_of((qi * n_sub - 1 - j * unroll - u) * bk, bk), 0, False)
               for u in range(unroll)])
        return carry

    assert n_sub % unroll == 0
    lax.fori_loop(0, qi * (n_sub // unroll), body, 0)
    o_ref[...] = acc_ref[...].T.astype(o_ref.dtype)


def _stick_attention(qkv, batch, seq, bq=1024, bk=256, unroll=4):
    t = qkv.shape[0]
    bq, bk = min(bq, seq), min(bk, seq)
    nq = seq // bq
    return pl.pallas_call(
        functools.partial(_stick_kernel, bq=bq, bk=bk, unroll=min(unroll, bq // bk)),
        grid=(batch, C_HEADS, nq),
        in_specs=[pl.BlockSpec((bq, HEAD_DIM), lambda b, hd, i: (b * nq + i, hd)),
                  pl.BlockSpec((seq, HEAD_DIM), lambda b, hd, i: (b, C_HEADS + hd)),
                  pl.BlockSpec((seq, HEAD_DIM), lambda b, hd, i: (b, 2 * C_HEADS + hd))],
        out_specs=pl.BlockSpec((bq, HEAD_DIM), lambda b, hd, i: (b * nq + i, hd)),
        out_shape=jax.ShapeDtypeStruct((t, C_HEADS * HEAD_DIM), MXU_DTYPE),
        scratch_shapes=[pltpu.VMEM((HEAD_DIM, bq), F32), pltpu.VMEM((1, bq), F32)],
        compiler_params=_params(3),
        name="stick_breaking_attention",
    )(qkv, qkv, qkv)


def _top_values(s, count, with_rank=False):
    vals = []
    cur = s
    rank = jnp.full(s.shape, float(count), F32) if with_rank else None
    for k in range(count):
        m = jnp.max(cur, axis=0, keepdims=True)
        vals.append(m)
        hit = cur == m
        if with_rank:
            rank = jnp.where(hit, float(k), rank)
        cur = jnp.where(hit, NEG_INF, cur)
    return (vals, rank) if with_rank else vals


def _ranked(s, count):
    n = s.shape[0]
    rows = lax.broadcasted_iota(jnp.int32, s.shape, 0)
    vals = []
    cur = s
    rank = jnp.full(s.shape, float(count), F32)
    for k in range(count):
        m = jnp.max(cur, axis=0, keepdims=True)
        first = jnp.min(jnp.where(cur == m, rows, n), axis=0, keepdims=True)
        pick = rows == first
        vals.append(m)
        rank = jnp.where(pick, float(k), rank)
        cur = jnp.where(pick, NEG_INF, cur)
    return vals, rank


def _pair_sums(v1, v2):
    v1_all = jnp.concatenate(v1, axis=0)
    v2_all = jnp.concatenate(v2, axis=0)
    return jnp.concatenate([v1[0] + v2_all]
                           + [v1[p] + v2_all[:8] for p in range(1, 8)]
                           + [v2[0] + v1_all[8:]], axis=0)


def _count_at_least(s, threshold):
    return jnp.sum(jnp.where(s >= threshold, 1.0, 0.0), axis=0, keepdims=True)


def _route_distinct(s1, s2):
    v1 = _top_values(s1, PEER_TOPK)
    v2, rank2 = _top_values(s2, PEER_TOPK, with_rank=True)
    cand = _pair_sums(v1, v2)
    tops = _top_values(cand, PEER_TOPK + 1)
    cut = 0.5 * (tops[PEER_TOPK - 1] + tops[PEER_TOPK])
    top = tops[0]
    z = jnp.sum(jnp.where(cand >= cut, jnp.exp(cand - top), 0.0), axis=0, keepdims=True)
    need = cut - s1
    count = jnp.zeros_like(s1)
    for q in range(PEER_TOPK):
        count = count + jnp.where(v2[q] >= need, 1.0, 0.0)
    a = jnp.where(s1 >= v1[PEER_TOPK - 1], jnp.exp(s1 - v1[0]) / (2.0 * z), 0.0)
    b = jnp.where(s2 >= v2[PEER_TOPK - 1], jnp.exp(s2 - v2[0]), 0.0)
    distinct = ((_count_at_least(s1, v1[PEER_TOPK - 1]) == PEER_TOPK)
                & (_count_at_least(s2, v2[PEER_TOPK - 1]) == PEER_TOPK)
                & (_count_at_least(cand, tops[PEER_TOPK]) == PEER_TOPK + 1))
    return rank2, b, count, a, distinct


def _route_with_ties(s1, s2):
    v1, rank1 = _ranked(s1, PEER_TOPK)
    v2, rank2 = _ranked(s2, PEER_TOPK)
    cand = _pair_sums(v1, v2)
    _, cand_rank = _ranked(cand, PEER_TOPK)
    chosen = cand_rank < PEER_TOPK
    top = v1[0] + v2[0]
    z = jnp.sum(jnp.where(chosen, jnp.exp(cand - top), 0.0), axis=0, keepdims=True)
    picked = jnp.where(chosen, 1.0, 0.0)
    per_p = [jnp.sum(picked[:16], axis=0, keepdims=True)]
    per_p += [jnp.sum(picked[8 + 8 * p:16 + 8 * p], axis=0, keepdims=True) for p in range(1, 8)]
    per_p += [picked[64 + p:65 + p] for p in range(8, 16)]
    count = jnp.zeros_like(s1)
    for p in range(PEER_TOPK):
        count = count + jnp.where(rank1 == p, per_p[p], 0.0)
    a = jnp.where(rank1 < PEER_TOPK, jnp.exp(s1 - v1[0]) / (2.0 * z), 0.0)
    b = jnp.where(rank2 < PEER_TOPK, jnp.exp(s2 - v2[0]), 0.0)
    return rank2, b, count, a


def _peer_route_kernel(q_ref, keys_ref, r2_ref, b_ref, c_ref, a_ref):
    def head(hd, carry):
        col = pl.multiple_of(hd * 2 * PEER_HALF, 2 * PEER_HALF)
        q1 = q_ref[:, pl.ds(col, PEER_HALF)].astype(MXU_DTYPE)
        q2 = q_ref[:, pl.ds(col + PEER_HALF, PEER_HALF)].astype(MXU_DTYPE)
        s1 = lax.dot_general(keys_ref[hd, 0], q1, NT_DIMS, preferred_element_type=F32)
        s2 = lax.dot_general(keys_ref[hd, 1], q2, NT_DIMS, preferred_element_type=F32)

        def store(rank2, b, count, a):
            groups = pl.ds(pl.multiple_of(hd * KEY_GROUPS, KEY_GROUPS), KEY_GROUPS)
            tokens = rank2.shape[1]
            r2_ref[groups] = rank2.astype(r2_ref.dtype).reshape(KEY_GROUPS, 16, tokens)
            b_ref[groups] = b.astype(b_ref.dtype).reshape(KEY_GROUPS, 16, tokens)
            c_ref[hd] = count
            a_ref[hd] = a

        rank2, b, count, a, distinct = _route_distinct(s1, s2)
        store(rank2, b, count, a)

        @pl.when(jnp.max(jnp.where(distinct, 0.0, 1.0)) > 0.0)
        def _():
            store(*_route_with_ties(s1, s2))

        return carry

    lax.fori_loop(0, PEER_HEADS, head, 0)


def _peer_route(qp, keys, tr=256):
    t = qp.shape[0]
    shape = (PEER_HEADS, PEER_N_KEYS, t)
    spec = pl.BlockSpec((PEER_HEADS, PEER_N_KEYS, tr), lambda i: (0, 0, i))
    flat = (PEER_HEADS * KEY_GROUPS, 16, t)
    flat_spec = pl.BlockSpec((PEER_HEADS * KEY_GROUPS, 16, tr), lambda i: (0, 0, i))
    return pl.pallas_call(
        _peer_route_kernel,
        grid=(t // tr,),
        in_specs=[pl.BlockSpec((tr, qp.shape[1]), lambda i: (i, 0)),
                  pl.BlockSpec(keys.shape, lambda i: (0, 0, 0, 0))],
        out_specs=[flat_spec, flat_spec, spec, spec],
        out_shape=[jax.ShapeDtypeStruct(flat, MXU_DTYPE), jax.ShapeDtypeStruct(flat, MXU_DTYPE),
                   jax.ShapeDtypeStruct(shape, F32), jax.ShapeDtypeStruct(shape, F32)],
        compiler_params=_params(1),
        name="peer_route",
    )(qp, keys)


def _twice_gelu(x):
    return x * (1.0 + lax.erf(x * (2.0 ** -0.5)))


def _row_tile(row, rows):
    packed = jnp.broadcast_to(row, (16, 128)).astype(MXU_DTYPE)
    return jnp.concatenate([packed] * (rows // 16), axis=0)


def _peer_kernel(xt_ref, res_ref, u_ref, v_ref, r2_ref, b_ref, c_ref, a_ref, g_ref, bias_ref,
                 o_ref, ob_ref, st_ref, wt_ref, *, n_i, tb, alpha):
    e = pl.program_id(1)

    @pl.when(e == 0)
    def _():
        o_ref[...] = jnp.zeros_like(o_ref)

    st_ref[...] = jnp.dot(u_ref[...], xt_ref[...], preferred_element_type=F32)
    zero = jnp.zeros((PEER_N_KEYS, 128), MXU_DTYPE)
    assert 8 % n_i == 0
    per_group = 8 // n_i
    within = e % per_group

    def key_rows(ref, hd, lanes):
        grp = ref[hd, 0, :, lanes]
        rows = grp[:n_i]
        for k in range(1, per_group):
            rows = jnp.where(within == k, grp[k * n_i:(k + 1) * n_i], rows)
        return rows

    live = min(n_i, 4)
    for l in range(tb // 128):
        lanes = slice(l * 128, (l + 1) * 128)
        for i0 in range(0, n_i, live):
            gates = [zero for _ in range(live)]
            for hd in range(PEER_HEADS):
                r2 = jnp.concatenate([r2_ref[hd * KEY_GROUPS + g, :, lanes] for g in range(KEY_GROUPS)],
                                     axis=0)
                b = jnp.concatenate([b_ref[hd * KEY_GROUPS + g, :, lanes] for g in range(KEY_GROUPS)],
                                    axis=0)
                counts = key_rows(c_ref, hd, lanes)
                a_rows = key_rows(a_ref, hd, lanes)
                for k in range(live):
                    count = _row_tile(counts[i0 + k:i0 + k + 1], PEER_N_KEYS)
                    a = _row_tile(a_rows[i0 + k:i0 + k + 1], PEER_N_KEYS)
                    gates[k] = gates[k] + jnp.where(r2 < count, b, zero) * a
            for k in range(live):
                rows = slice((i0 + k) * PEER_N_KEYS, (i0 + k + 1) * PEER_N_KEYS)
                act = _twice_gelu(st_ref[rows, lanes]).astype(MXU_DTYPE)
                wt_ref[rows, lanes] = act * gates[k]
    o_ref[...] += lax.dot_general(wt_ref[...], v_ref[...], TN_DIMS, preferred_element_type=F32)

    @pl.when(e == pl.num_programs(1) - 1)
    def _():
        y = _layer_norm_rows(alpha * res_ref[...] + o_ref[...], g_ref[...], bias_ref[...])
        o_ref[...] = y
        ob_ref[...] = y.astype(ob_ref.dtype)


def _peer_experts(xt, res, u, v, r2, b, c, a, gain, bias, alpha, tb=512, n_i=8):
    t, d = res.shape
    n_exp = u.shape[0]
    eb = n_i * PEER_N_KEYS
    tb = min(tb, t)
    tok = lambda i, e: (i, 0)
    per_group = 8 // n_i
    c = c.reshape(PEER_HEADS, PEER_N_KEYS // 8, 8, t)
    a = a.reshape(PEER_HEADS, PEER_N_KEYS // 8, 8, t)
    full = pl.BlockSpec((PEER_HEADS, 1, 8, tb), lambda i, e: (0, e // per_group, 0, i))
    flat = pl.BlockSpec((PEER_HEADS * KEY_GROUPS, 16, tb), lambda i, e: (0, 0, i))
    return pl.pallas_call(
        functools.partial(_peer_kernel, n_i=n_i, tb=tb, alpha=alpha),
        grid=(t // tb, n_exp // eb),
        in_specs=[pl.BlockSpec((d, tb), lambda i, e: (0, i)), pl.BlockSpec((tb, d), tok),
                  pl.BlockSpec((eb, d), lambda i, e: (e, 0)),
                  pl.BlockSpec((eb, d), lambda i, e: (e, 0)),
                  flat, flat, full, full,
                  pl.BlockSpec((1, d), lambda i, e: (0, 0)), pl.BlockSpec((1, d), lambda i, e: (0, 0))],
        out_specs=[pl.BlockSpec((tb, d), tok), pl.BlockSpec((tb, d), tok)],
        out_shape=[jax.ShapeDtypeStruct((t, d), F32), jax.ShapeDtypeStruct((t, d), MXU_DTYPE)],
        scratch_shapes=[pltpu.VMEM((eb, tb), F32), pltpu.VMEM((eb, tb), MXU_DTYPE)],
        compiler_params=_params(2),
        name="peer_experts",
    )(xt, res, u, v, r2, b, c, a, gain.reshape(1, d), bias.reshape(1, d))


def _peer_layer(x, xb, w_query, keys, u, v, gain, bias, alpha):
    qp = _matmul(xb, w_query, F32, bn=w_query.shape[1])
    r2, b, c, a = _peer_route(qp, keys)
    return _peer_experts(xb.T, x, u, v, r2, b, c, a, gain, bias, alpha)


def _even_mixer(x2d, batch, seq, w_in, q_norm, w_q_b, kv_norm, w_kv_b, tables):
    d = x2d.shape[1]
    w_in_p = jnp.concatenate(
        [w_in, jnp.zeros((d, 4096 - w_in.shape[1]), w_in.dtype)], axis=1).astype(MXU_DTYPE)
    h = _matmul(x2d, w_in_p, F32)
    cos_a, sin_a, cos_b, sin_lo, sin_hi = tables
    qa, ka, cqn, ckvn, kr = _even_prep(h, tables, q_norm, kv_norm, seq)
    out_a = _dilated_attention(qa, ka, h, batch, seq)
    wq = w_q_b.reshape(MLA_Q_LORA, B_HEADS, MLA_NOPE + MLA_ROPE)
    wq = jnp.pad(wq, ((0, 0), (0, 0), (0, MLA_QK_PAD - MLA_NOPE - MLA_ROPE)))
    wq = wq.reshape(MLA_Q_LORA, B_HEADS * MLA_QK_PAD).astype(MXU_DTYPE)
    wkv = w_kv_b.reshape(MLA_KV_LORA, B_HEADS, MLA_NOPE + MLA_V)
    wkv = jnp.concatenate([wkv[:, :, :MLA_NOPE].reshape(MLA_KV_LORA, -1),
                           wkv[:, :, MLA_NOPE:].reshape(MLA_KV_LORA, -1)], axis=1).astype(MXU_DTYPE)
    qf = _matmul(cqn, wq, F32)
    kvf = _matmul(ckvn, wkv, MXU_DTYPE)
    qm, km = _mla_prep(qf, kvf, kr, (cos_b, sin_lo, sin_hi), seq)
    out_b = _mla_attention(qm, km, kvf, batch, seq)
    return out_a, out_b


def kernel(x, a_w_in, b_q_norm, b_w_q_b, b_kv_norm, b_w_kv_b, ab_w_out, c_w_in, c_w_out,
           peer_w_query, peer_sub_keys, peer_u, peer_v, ln_gain, ln_bias):
    batch, seq, d = x.shape
    depth = peer_u.shape[0]
    alpha = (2 * depth) ** 0.25
    tables = _rope_tables(seq)
    u_all, v_all = peer_u.astype(MXU_DTYPE), peer_v.astype(MXU_DTYPE)
    xf = x.reshape(batch * seq, d)
    xb = None
    for layer in range(depth):
        i = layer // 2
        if layer % 2 == 0:
            src = xf if xb is None else xb
            out_a, out_b = _even_mixer(src, batch, seq, a_w_in[i], b_q_norm[i], b_w_q_b[i],
                                       b_kv_norm[i], b_w_kv_b[i], tables)
            w_out = ab_w_out[i].astype(MXU_DTYPE)
            half = A_HEADS * HEAD_DIM
            xf, xb = _matmul_residual_ln([out_a, out_b], [w_out[:half], w_out[half:]], xf,
                                         ln_gain[layer, 0], ln_bias[layer, 0], alpha)
        else:
            src = xf if xb is None else xb
            n_q = C_HEADS * HEAD_DIM
            w_qkv = jnp.concatenate([c_w_in[i][:, :n_q] * (HEAD_DIM ** -0.5 * LOG2_E),
                                     c_w_in[i][:, n_q:]], axis=1).astype(MXU_DTYPE)
            qkv = _matmul(src, w_qkv, MXU_DTYPE)
            o = _stick_attention(qkv, batch, seq)
            xf, xb = _matmul_residual_ln([o], [c_w_out[i].astype(MXU_DTYPE)], xf,
                                         ln_gain[layer, 0], ln_bias[layer, 0], alpha)
        xf, xb = _peer_layer(xf, xb, peer_w_query[layer].astype(MXU_DTYPE),
                         peer_sub_keys[layer].astype(MXU_DTYPE),
                         u_all[layer], v_all[layer],
                         ln_gain[layer, 1], ln_bias[layer, 1], alpha)
    return xf.reshape(batch, seq, d)
```

```python
import functools

import jax
import jax.numpy as jnp
from jax import lax
from jax.experimental import pallas as pl
from jax.experimental.pallas import tpu as pltpu

F32 = jnp.float32
MXU_DTYPE = jnp.bfloat16

HEAD_DIM = 128
BLOCK = 128
DIL_STEPS = 128
DILATIONS = (1, 4, 16)
DIL_CHUNK = BLOCK * DILATIONS[-1]
A_HEADS = 8
B_HEADS = 8
MLA_Q_LORA = 512
MLA_KV_LORA = 256
MLA_NOPE = 128
MLA_ROPE = 64
MLA_V = 128
MLA_QK_PAD = 256
C_HEADS = 16
PEER_HEADS = 8
PEER_N_KEYS = 128
PEER_TOPK = 16
PEER_HALF = 128
KEY_GROUPS = PEER_N_KEYS // 16
ROPE_THETA = 10000.0
LN_EPS = 1e-5
RMS_EPS = 1e-6
NEG_INF = -1e30
LOG2_E = 1.4426950408889634
MLA_Q_SCALE = (MLA_NOPE + MLA_ROPE) ** -0.5 * LOG2_E

V7X_VMEM_LIMIT_BYTES = 56 * 1024 * 1024
NT_DIMS = (((1,), (1,)), ((), ()))
TN_DIMS = (((0,), (0,)), ((), ()))


def _params(n_axes):
    return pltpu.CompilerParams(dimension_semantics=("arbitrary",) * n_axes,
                                vmem_limit_bytes=V7X_VMEM_LIMIT_BYTES)


def _mm_kernel(a_ref, b_ref, o_ref):
    a = a_ref[...].astype(MXU_DTYPE)
    o_ref[...] = jnp.dot(a, b_ref[...], preferred_element_type=F32).astype(o_ref.dtype)


def _matmul(a, b, out_dtype, bm=1024, bn=1024):
    m, k = a.shape
    n = b.shape[1]
    bm, bn = min(bm, m), min(bn, n)
    assert m % bm == 0 and n % bn == 0
    return pl.pallas_call(
        _mm_kernel,
        grid=(m // bm, n // bn),
        in_specs=[pl.BlockSpec((bm, k), lambda i, j: (i, 0)),
                  pl.BlockSpec((k, bn), lambda i, j: (0, j))],
        out_specs=pl.BlockSpec((bm, bn), lambda i, j: (i, j)),
        out_shape=jax.ShapeDtypeStruct((m, n), out_dtype),
        compiler_params=_params(2),
        name="matmul",
    )(a, b)


def _layer_norm_rows(y, g, b):
    mu = jnp.mean(y, axis=-1, keepdims=True)
    d = y - mu
    var = jnp.mean(d * d, axis=-1, keepdims=True)
    return d * lax.rsqrt(var + LN_EPS) * g + b


def _mm_ln_kernel(*refs, n_pairs, alpha):
    a_refs = refs[:n_pairs]
    w_refs = refs[n_pairs:2 * n_pairs]
    res_ref, g_ref, b_ref, o_ref, ob_ref = refs[2 * n_pairs:]
    acc = jnp.dot(a_refs[0][...].astype(MXU_DTYPE), w_refs[0][...], preferred_element_type=F32)
    for a_ref, w_ref in zip(a_refs[1:], w_refs[1:]):
        acc = acc + jnp.dot(a_ref[...].astype(MXU_DTYPE), w_ref[...], preferred_element_type=F32)
    y = _layer_norm_rows(alpha * res_ref[...] + acc, g_ref[...], b_ref[...])
    o_ref[...] = y
    ob_ref[...] = y.astype(ob_ref.dtype)


def _matmul_residual_ln(a_list, w_list, res, gain, bias, alpha, bm=256):
    m, n = res.shape
    n_pairs = len(a_list)
    in_specs = [pl.BlockSpec((bm, a.shape[1]), lambda i: (i, 0)) for a in a_list]
    in_specs += [pl.BlockSpec(w.shape, lambda i: (0, 0)) for w in w_list]
    in_specs += [pl.BlockSpec((bm, n), lambda i: (i, 0)),
                 pl.BlockSpec((1, n), lambda i: (0, 0)),
                 pl.BlockSpec((1, n), lambda i: (0, 0))]
    return pl.pallas_call(
        functools.partial(_mm_ln_kernel, n_pairs=n_pairs, alpha=alpha),
        grid=(m // bm,),
        in_specs=in_specs,
        out_specs=[pl.BlockSpec((bm, n), lambda i: (i, 0)), pl.BlockSpec((bm, n), lambda i: (i, 0))],
        out_shape=[jax.ShapeDtypeStruct((m, n), F32), jax.ShapeDtypeStruct((m, n), MXU_DTYPE)],
        compiler_params=_params(1),
        name="matmul_residual_ln",
    )(*a_list, *w_list, res, gain.reshape(1, n), bias.reshape(1, n))


def _rope_tables(seq):
    pos = jnp.arange(seq, dtype=F32)[:, None]

    def cos_sin(half):
        inv = ROPE_THETA ** (-jnp.arange(half, dtype=F32) / half)
        ang = pos * inv[None, :]
        return jnp.cos(ang), jnp.sin(ang)

    c, s = cos_sin(HEAD_DIM // 2)
    cos_a = jnp.concatenate([c, c], axis=1)
    sin_a = jnp.concatenate([-s, s], axis=1)
    c, s = cos_sin(MLA_ROPE // 2)
    z32 = jnp.zeros_like(c)
    cos_b = jnp.concatenate([c, c, z32, z32], axis=1)
    sin_lo = jnp.concatenate([-s, z32, z32, z32], axis=1)
    sin_hi = jnp.concatenate([z32, s, z32, z32], axis=1)
    return cos_a, sin_a, cos_b, sin_lo, sin_hi


def _rope128(x, cos_a, sin_a):
    return x * cos_a + pltpu.roll(x, HEAD_DIM // 2, 1) * sin_a


def _rope64(x, cos_b, sin_lo, sin_hi):
    return x * cos_b + pltpu.roll(x, 96, 1) * sin_lo + pltpu.roll(x, 32, 1) * sin_hi


def _rms_norm_rows(x, g):
    return x * lax.rsqrt(jnp.mean(x * x, axis=-1, keepdims=True) + RMS_EPS) * g


def _even_prep_kernel(qk_ref, tail_ref, cos_a_ref, sin_a_ref, cos_b_ref, sin_lo_ref, sin_hi_ref,
                      qn_ref, kvn_ref, qa_ref, ka_ref, cq_ref, ckv_ref, kr_ref):
    cos_a, sin_a = cos_a_ref[...], sin_a_ref[...]
    n_qk = A_HEADS * HEAD_DIM
    for hd in range(A_HEADS):
        lo = hd * HEAD_DIM
        qa_ref[:, lo:lo + HEAD_DIM] = _rope128(qk_ref[:, lo:lo + HEAD_DIM], cos_a, sin_a)
        ka_ref[:, lo:lo + HEAD_DIM] = _rope128(qk_ref[:, n_qk + lo:n_qk + lo + HEAD_DIM], cos_a, sin_a)
    cq_ref[...] = _rms_norm_rows(tail_ref[:, :MLA_Q_LORA], qn_ref[...]).astype(cq_ref.dtype)
    o2 = MLA_Q_LORA + MLA_KV_LORA
    ckv_ref[...] = _rms_norm_rows(tail_ref[:, MLA_Q_LORA:o2], kvn_ref[...]).astype(ckv_ref.dtype)
    kr = _rope64(tail_ref[:, o2:o2 + 128], cos_b_ref[...], sin_lo_ref[...], sin_hi_ref[...])
    kr_ref[...] = kr.astype(kr_ref.dtype)


def _even_prep(h, tables, q_norm, kv_norm, seq, rows=256):
    t = h.shape[0]
    n_qk = A_HEADS * HEAD_DIM
    sb = seq // rows
    tab_spec = pl.BlockSpec((rows, 128), lambda i: (i % sb, 0))
    return pl.pallas_call(
        _even_prep_kernel,
        grid=(t // rows,),
        in_specs=[pl.BlockSpec((rows, 2 * n_qk), lambda i: (i, 0)),
                  pl.BlockSpec((rows, 1024), lambda i: (i, 3)),
                  tab_spec, tab_spec, tab_spec, tab_spec, tab_spec,
                  pl.BlockSpec((1, MLA_Q_LORA), lambda i: (0, 0)),
                  pl.BlockSpec((1, MLA_KV_LORA), lambda i: (0, 0))],
        out_specs=[pl.BlockSpec((rows, n_qk), lambda i: (i, 0)),
                   pl.BlockSpec((rows, n_qk), lambda i: (i, 0)),
                   pl.BlockSpec((rows, MLA_Q_LORA), lambda i: (i, 0)),
                   pl.BlockSpec((rows, MLA_KV_LORA), lambda i: (i, 0)),
                   pl.BlockSpec((rows, 128), lambda i: (i, 0))],
        out_shape=[jax.ShapeDtypeStruct((t, n_qk), F32),
                   jax.ShapeDtypeStruct((t, n_qk), F32),
                   jax.ShapeDtypeStruct((t, MLA_Q_LORA), MXU_DTYPE),
                   jax.ShapeDtypeStruct((t, MLA_KV_LORA), MXU_DTYPE),
                   jax.ShapeDtypeStruct((t, 128), MXU_DTYPE)],
        compiler_params=_params(1),
        name="even_prep",
    )(h, h, *tables, q_norm.reshape(1, -1), kv_norm.reshape(1, -1))


def _mla_prep_kernel(q_ref, kn_ref, kr_ref, cos_b_ref, sin_lo_ref, sin_hi_ref, qm_ref, km_ref):
    cos_b, sin_lo, sin_hi = cos_b_ref[...], sin_lo_ref[...], sin_hi_ref[...]
    kr = kr_ref[...]
    for hd in range(B_HEADS):
        lo = hd * MLA_QK_PAD
        qm_ref[:, lo:lo + MLA_NOPE] = (q_ref[:, lo:lo + MLA_NOPE] * MLA_Q_SCALE).astype(qm_ref.dtype)
        q_rope = _rope64(q_ref[:, lo + MLA_NOPE:lo + MLA_QK_PAD], cos_b, sin_lo, sin_hi)
        qm_ref[:, lo + MLA_NOPE:lo + MLA_QK_PAD] = (q_rope * MLA_Q_SCALE).astype(qm_ref.dtype)
        km_ref[:, lo:lo + MLA_NOPE] = kn_ref[:, hd * MLA_NOPE:(hd + 1) * MLA_NOPE]
        km_ref[:, lo + MLA_NOPE:lo + MLA_QK_PAD] = kr


def _mla_prep(qf, kvf, kr, tables, seq, rows=256):
    t = qf.shape[0]
    w = B_HEADS * MLA_QK_PAD
    sb = seq // rows
    tab_spec = pl.BlockSpec((rows, 128), lambda i: (i % sb, 0))
    return pl.pallas_call(
        _mla_prep_kernel,
        grid=(t // rows,),
        in_specs=[pl.BlockSpec((rows, w), lambda i: (i, 0)),
                  pl.BlockSpec((rows, B_HEADS * MLA_NOPE), lambda i: (i, 0)),
                  pl.BlockSpec((rows, 128), lambda i: (i, 0)),
                  tab_spec, tab_spec, tab_spec],
        out_specs=[pl.BlockSpec((rows, w), lambda i: (i, 0)), pl.BlockSpec((rows, w), lambda i: (i, 0))],
        out_shape=[jax.ShapeDtypeStruct((t, w), MXU_DTYPE), jax.ShapeDtypeStruct((t, w), MXU_DTYPE)],
        compiler_params=_params(1),
        name="mla_prep",
    )(qf, kvf, kr, *tables)


def _dilated_kernel(q_ref, kc_ref, kp_ref, vc_ref, vp_ref, o_ref,
                    kbuf, vbuf, o0, o1, o2, l0, l1, l2):
    c = pl.program_id(2)
    kbuf[:DIL_CHUNK, :] = kp_ref[...]
    kbuf[DIL_CHUNK:, :] = kc_ref[...]
    vbuf[:DIL_CHUNK, :] = vp_ref[...]
    vbuf[DIL_CHUNK:, :] = vc_ref[...]
    scale = HEAD_DIM ** -0.5
    row = lax.broadcasted_iota(jnp.int32, (BLOCK, 2 * BLOCK), 0)
    col = lax.broadcasted_iota(jnp.int32, (BLOCK, 2 * BLOCK), 1)
    band = (col >= row) & (col <= row + DIL_STEPS)

    def blocks(specs):
        staged = []
        for q_start, k_start, dil, first, o_g, l_g in specs:
            if dil == 1:
                q_idx, k_idx = pl.ds(q_start, BLOCK), pl.ds(k_start, 2 * BLOCK)
            else:
                q_idx = pl.ds(q_start, BLOCK, stride=dil)
                k_idx = pl.ds(k_start, 2 * BLOCK, stride=dil)
            q = q_ref[q_idx, :].astype(MXU_DTYPE)
            k = kbuf[k_idx, :].astype(MXU_DTYPE)
            v = vbuf[k_idx, :].astype(MXU_DTYPE)
            logits = lax.dot_general(q, k, NT_DIMS, preferred_element_type=F32) * scale
            staged.append((logits, v, q_idx, first, o_g, l_g))
        weighted = []
        for logits, v, q_idx, first, o_g, l_g in staged:
            valid_from = jnp.where(first, BLOCK, 0)
            logits = jnp.where(band & (col >= valid_from), logits, NEG_INF)
            m = jnp.max(logits, axis=-1, keepdims=True)
            p = jnp.exp(logits - m)
            denom = jnp.sum(p, axis=-1, keepdims=True)
            weighted.append(((p / denom).astype(MXU_DTYPE), m + jnp.log(denom)))
        for (logits, v, q_idx, first, o_g, l_g), (pn, lse) in zip(staged, weighted):
            o_g[q_idx, :] = jnp.dot(pn, v, preferred_element_type=F32)
            l_g[q_idx, :] = jnp.broadcast_to(lse, (BLOCK, HEAD_DIM))

    n_blocks = DIL_CHUNK // BLOCK
    per_trip = 2

    def body(j, carry):
        specs = []
        for u in range(per_trip):
            i = j * per_trip + u
            specs.append((pl.multiple_of(i * BLOCK, BLOCK),
                          pl.multiple_of(DIL_CHUNK + (i - 1) * BLOCK, BLOCK),
                          DILATIONS[0], (c == 0) & (i == 0), o0, l0))
            d1 = DILATIONS[1]
            n, r = i // d1, i % d1
            span = BLOCK * d1
            specs.append((n * span + r, DIL_CHUNK + (n - 1) * span + r, d1, (c == 0) & (n == 0), o1, l1))
            specs.append((i, DIL_CHUNK - BLOCK * DILATIONS[2] + i, DILATIONS[2], c == 0, o2, l2))
        blocks(specs)
        return carry

    lax.fori_loop(0, n_blocks // per_trip, body, 0)

    la, lb, lc = l0[...], l1[...], l2[...]
    mx = jnp.maximum(jnp.maximum(la, lb), lc)
    ea, eb, ec = jnp.exp(la - mx), jnp.exp(lb - mx), jnp.exp(lc - mx)
    merged = (ea * o0[...] + eb * o1[...] + ec * o2[...]) / (ea + eb + ec)
    o_ref[...] = merged.astype(o_ref.dtype)


def _dilated_attention(qa, ka, h, batch, seq):
    t = qa.shape[0]
    nc = seq // DIL_CHUNK
    v_col0 = 2 * A_HEADS
    cur = lambda b, hd, c: (b * nc + c, hd)
    prev = lambda b, hd, c: (b * nc + jnp.maximum(c - 1, 0), hd)
    blk = (DIL_CHUNK, HEAD_DIM)
    return pl.pallas_call(
        _dilated_kernel,
        grid=(batch, A_HEADS, nc),
        in_specs=[pl.BlockSpec(blk, cur),
                  pl.BlockSpec(blk, cur), pl.BlockSpec(blk, prev),
                  pl.BlockSpec(blk, lambda b, hd, c: (b * nc + c, v_col0 + hd)),
                  pl.BlockSpec(blk, lambda b, hd, c: (b * nc + jnp.maximum(c - 1, 0), v_col0 + hd))],
        out_specs=pl.BlockSpec(blk, cur),
        out_shape=jax.ShapeDtypeStruct((t, A_HEADS * HEAD_DIM), MXU_DTYPE),
        scratch_shapes=[pltpu.VMEM((2 * DIL_CHUNK, HEAD_DIM), F32)] * 2
                       + [pltpu.VMEM(blk, F32)] * 6,
        compiler_params=_params(3),
        name="dilated_attention",
    )(qa, ka, ka, h, h)


def _mla_kernel(q_ref, k_ref, v_ref, o_ref, m_ref, l_ref, acc_ref, *, bq, bk, unroll):
    qi = pl.program_id(2)
    n_sub = bq // bk
    m_ref[...] = jnp.full_like(m_ref, NEG_INF)
    l_ref[...] = jnp.zeros_like(l_ref)
    acc_ref[...] = jnp.zeros_like(acc_ref)

    def group(k_starts, r0, diagonal):
        cols = bq - r0
        q = q_ref[r0:, :]
        ss = [lax.dot_general(k_ref[pl.ds(ks, bk), :], q, NT_DIMS, preferred_element_type=F32)
              for ks in k_starts]
        if diagonal:
            key = lax.broadcasted_iota(jnp.int32, (bk, cols), 0)
            qry = lax.broadcasted_iota(jnp.int32, (bk, cols), 1)
            ss = [jnp.where(key <= qry, s, NEG_INF) for s in ss]
        m_prev = m_ref[:, r0:]
        m_new = m_prev
        for s in ss:
            m_new = jnp.maximum(m_new, jnp.max(s, axis=0, keepdims=True))
        alpha = jnp.exp2(m_prev - m_new)
        l = alpha * l_ref[:, r0:]
        acc = alpha * acc_ref[:, r0:]
        for ks, s in zip(k_starts, ss):
            p = jnp.exp2(s - m_new)
            l = l + jnp.sum(p, axis=0, keepdims=True)
            acc = acc + lax.dot_general(v_ref[pl.ds(ks, bk), :], p.astype(MXU_DTYPE), TN_DIMS,
                                        preferred_element_type=F32)
        m_ref[:, r0:] = m_new
        l_ref[:, r0:] = l
        acc_ref[:, r0:] = acc

    def body(j, carry):
        group([pl.multiple_of((j * unroll + u) * bk, bk) for u in range(unroll)], 0, False)
        return carry

    assert n_sub % unroll == 0
    lax.fori_loop(0, qi * (n_sub // unroll), body, 0)
    for jj in range(n_sub):
        group([pl.multiple_of(qi * bq + jj * bk, bk)], jj * bk, True)
    o_ref[...] = (acc_ref[...] / l_ref[...]).T.astype(o_ref.dtype)


def _mla_attention(qm, km, kvf, batch, seq, bq=1024, bk=512, unroll=2):
    t = qm.shape[0]
    bq, bk = min(bq, seq), min(bk, seq)
    nq = seq // bq
    return pl.pallas_call(
        functools.partial(_mla_kernel, bq=bq, bk=bk, unroll=min(unroll, bq // bk)),
        grid=(batch, B_HEADS, nq),
        in_specs=[pl.BlockSpec((bq, MLA_QK_PAD), lambda b, hd, i: (b * nq + i, hd)),
                  pl.BlockSpec((seq, MLA_QK_PAD), lambda b, hd, i: (b, hd)),
                  pl.BlockSpec((seq, MLA_V), lambda b, hd, i: (b, B_HEADS + hd))],
        out_specs=pl.BlockSpec((bq, MLA_V), lambda b, hd, i: (b * nq + i, hd)),
        out_shape=jax.ShapeDtypeStruct((t, B_HEADS * MLA_V), MXU_DTYPE),
        scratch_shapes=[pltpu.VMEM((1, bq), F32), pltpu.VMEM((1, bq), F32),
                        pltpu.VMEM((MLA_V, bq), F32)],
        compiler_params=_params(3),
        name="mla_attention",
    )(qm, km, kvf)


def _stick_kernel(q_ref, k_ref, v_ref, o_ref, acc_ref, run_ref, *, bq, bk, unroll):
    qi = pl.program_id(2)
    n_sub = bq // bk
    tri_r = lax.broadcasted_iota(jnp.int32, (bk, bk), 0)
    tri_c = lax.broadcasted_iota(jnp.int32, (bk, bk), 1)
    at_or_after = jnp.where(tri_r >= tri_c, 1.0, 0.0).astype(MXU_DTYPE)
    acc_ref[...] = jnp.zeros_like(acc_ref)
    run_ref[...] = jnp.zeros_like(run_ref)

    def group(tiles):
        staged = []
        for ks, r0, diagonal in tiles:
            rows = bq - r0
            z = lax.dot_general(q_ref[r0:, :], k_ref[pl.ds(ks, bk), :], NT_DIMS,
                                preferred_element_type=F32)
            neg_abs = lax.bitcast_convert_type(
                lax.bitcast_convert_type(z, jnp.uint32) | jnp.uint32(0x80000000), F32)
            sp = jnp.maximum(z, 0.0) + jnp.log(1.0 + jnp.exp2(neg_abs)) * LOG2_E
            mask = None
            if diagonal:
                row = lax.broadcasted_iota(jnp.int32, (rows, bk), 0)
                col = lax.broadcasted_iota(jnp.int32, (rows, bk), 1)
                mask = col < row
                sp = jnp.where(mask, sp, 0.0)
            staged.append((z, sp, mask))
        sufs = []
        for _, sp, _ in staged:
            hi = sp.astype(MXU_DTYPE)
            lo = (sp - hi.astype(F32)).astype(MXU_DTYPE)
            sufs.append(jnp.dot(hi, at_or_after, preferred_element_type=F32)
                        + jnp.dot(lo, at_or_after, preferred_element_type=F32))
        run = run_ref[...]
        acc = acc_ref[...]
        for (ks, r0, _), (z, _, mask), suf in zip(tiles, staged, sufs):
            att = jnp.exp2(z - suf - run[r0:])
            if mask is not None:
                att = jnp.where(mask, att, 0.0)
            part = jnp.dot(att.astype(MXU_DTYPE), v_ref[pl.ds(ks, bk), :], preferred_element_type=F32)
            total = suf[:, :1]
            if r0 == 0:
                acc, run = acc + part, run + total
            else:
                acc = jnp.concatenate([acc[:r0], acc[r0:] + part], axis=0)
                run = jnp.concatenate([run[:r0], run[r0:] + total], axis=0)
        acc_ref[...] = acc
        run_ref[...] = run

    group([(pl.multiple_of(qi * bq + jj * bk, bk), jj * bk, True) for jj in reversed(range(n_sub))])

    def body(j, carry):
        group([(pl.multiple_of((qi * n_sub - 1 - j * unroll - u) * bk, bk), 0, False)
               for u in range(unroll)])
        return carry

    assert n_sub % unroll == 0
    lax.fori_loop(0, qi * (n_sub // unroll), body, 0)
    o_ref[...] = acc_ref[...].astype(o_ref.dtype)


def _stick_attention(qkv, batch, seq, bq=1024, bk=256, unroll=4):
    t = qkv.shape[0]
    bq, bk = min(bq, seq), min(bk, seq)
    nq = seq // bq
    return pl.pallas_call(
        functools.partial(_stick_kernel, bq=bq, bk=bk, unroll=min(unroll, bq // bk)),
        grid=(batch, C_HEADS, nq),
        in_specs=[pl.BlockSpec((bq, HEAD_DIM), lambda b, hd, i: (b * nq + i, hd)),
                  pl.BlockSpec((seq, HEAD_DIM), lambda b, hd, i: (b, C_HEADS + hd)),
                  pl.BlockSpec((seq, HEAD_DIM), lambda b, hd, i: (b, 2 * C_HEADS + hd))],
        out_specs=pl.BlockSpec((bq, HEAD_DIM), lambda b, hd, i: (b * nq + i, hd)),
        out_shape=jax.ShapeDtypeStruct((t, C_HEADS * HEAD_DIM), MXU_DTYPE),
        scratch_shapes=[pltpu.VMEM((bq, HEAD_DIM), F32), pltpu.VMEM((bq, 1), F32)],
        compiler_params=_params(3),
        name="stick_breaking_attention",
    )(qkv, qkv, qkv)


def _top_values(s, count, with_rank=False):
    vals = []
    cur = s
    rank = jnp.full(s.shape, float(count), F32) if with_rank else None
    for k in range(count):
        m = jnp.max(cur, axis=0, keepdims=True)
        vals.append(m)
        hit = cur == m
        if with_rank:
            rank = jnp.where(hit, float(k), rank)
        cur = jnp.where(hit, NEG_INF, cur)
    return (vals, rank) if with_rank else vals


def _ranked(s, count):
    n = s.shape[0]
    rows = lax.broadcasted_iota(jnp.int32, s.shape, 0)
    vals = []
    cur = s
    rank = jnp.full(s.shape, float(count), F32)
    for k in range(count):
        m = jnp.max(cur, axis=0, keepdims=True)
        first = jnp.min(jnp.where(cur == m, rows, n), axis=0, keepdims=True)
        pick = rows == first
        vals.append(m)
        rank = jnp.where(pick, float(k), rank)
        cur = jnp.where(pick, NEG_INF, cur)
    return vals, rank


def _pair_sums(v1, v2):
    v1_all = jnp.concatenate(v1, axis=0)
    v2_all = jnp.concatenate(v2, axis=0)
    return jnp.concatenate([v1[0] + v2_all]
                           + [v1[p] + v2_all[:8] for p in range(1, 8)]
                           + [v2[0] + v1_all[8:]], axis=0)


def _count_at_least(s, threshold):
    return jnp.sum(jnp.where(s >= threshold, 1.0, 0.0), axis=0, keepdims=True)


def _route_distinct(s1, s2):
    v1 = _top_values(s1, PEER_TOPK)
    v2, rank2 = _top_values(s2, PEER_TOPK, with_rank=True)
    cand = _pair_sums(v1, v2)
    tops = _top_values(cand, PEER_TOPK + 1)
    cut = 0.5 * (tops[PEER_TOPK - 1] + tops[PEER_TOPK])
    top = tops[0]
    z = jnp.sum(jnp.where(cand >= cut, jnp.exp(cand - top), 0.0), axis=0, keepdims=True)
    need = cut - s1
    count = jnp.zeros_like(s1)
    for q in range(PEER_TOPK):
        count = count + jnp.where(v2[q] >= need, 1.0, 0.0)
    a = jnp.where(s1 >= v1[PEER_TOPK - 1], jnp.exp(s1 - v1[0]) / (2.0 * z), 0.0)
    b = jnp.where(s2 >= v2[PEER_TOPK - 1], jnp.exp(s2 - v2[0]), 0.0)
    distinct = ((_count_at_least(s1, v1[PEER_TOPK - 1]) == PEER_TOPK)
                & (_count_at_least(s2, v2[PEER_TOPK - 1]) == PEER_TOPK)
                & (_count_at_least(cand, tops[PEER_TOPK]) == PEER_TOPK + 1))
    return rank2, b, count, a, distinct


def _route_with_ties(s1, s2):
    v1, rank1 = _ranked(s1, PEER_TOPK)
    v2, rank2 = _ranked(s2, PEER_TOPK)
    cand = _pair_sums(v1, v2)
    _, cand_rank = _ranked(cand, PEER_TOPK)
    chosen = cand_rank < PEER_TOPK
    top = v1[0] + v2[0]
    z = jnp.sum(jnp.where(chosen, jnp.exp(cand - top), 0.0), axis=0, keepdims=True)
    picked = jnp.where(chosen, 1.0, 0.0)
    per_p = [jnp.sum(picked[:16], axis=0, keepdims=True)]
    per_p += [jnp.sum(picked[8 + 8 * p:16 + 8 * p], axis=0, keepdims=True) for p in range(1, 8)]
    per_p += [picked[64 + p:65 + p] for p in range(8, 16)]
    count = jnp.zeros_like(s1)
    for p in range(PEER_TOPK):
        count = count + jnp.where(rank1 == p, per_p[p], 0.0)
    a = jnp.where(rank1 < PEER_TOPK, jnp.exp(s1 - v1[0]) / (2.0 * z), 0.0)
    b = jnp.where(rank2 < PEER_TOPK, jnp.exp(s2 - v2[0]), 0.0)
    return rank2, b, count, a


def _peer_route_kernel(q_ref, keys_ref, r2_ref, b_ref, c_ref, a_ref):
    def head(hd, carry):
        col = pl.multiple_of(hd * 2 * PEER_HALF, 2 * PEER_HALF)
        q1 = q_ref[:, pl.ds(col, PEER_HALF)].astype(MXU_DTYPE)
        q2 = q_ref[:, pl.ds(col + PEER_HALF, PEER_HALF)].astype(MXU_DTYPE)
        s1 = lax.dot_general(keys_ref[hd, 0], q1, NT_DIMS, preferred_element_type=F32)
        s2 = lax.dot_general(keys_ref[hd, 1], q2, NT_DIMS, preferred_element_type=F32)

        def store(rank2, b, count, a):
            groups = pl.ds(pl.multiple_of(hd * KEY_GROUPS, KEY_GROUPS), KEY_GROUPS)
            tokens = rank2.shape[1]
            r2_ref[groups] = rank2.astype(r2_ref.dtype).reshape(KEY_GROUPS, 16, tokens)
            b_ref[groups] = b.astype(b_ref.dtype).reshape(KEY_GROUPS, 16, tokens)
            c_ref[hd] = count
            a_ref[hd] = a

        rank2, b, count, a, distinct = _route_distinct(s1, s2)
        store(rank2, b, count, a)

        @pl.when(jnp.max(jnp.where(distinct, 0.0, 1.0)) > 0.0)
        def _():
            store(*_route_with_ties(s1, s2))

        return carry

    lax.fori_loop(0, PEER_HEADS, head, 0)


def _peer_route(qp, keys, tr=256):
    t = qp.shape[0]
    shape = (PEER_HEADS, PEER_N_KEYS, t)
    spec = pl.BlockSpec((PEER_HEADS, PEER_N_KEYS, tr), lambda i: (0, 0, i))
    flat = (PEER_HEADS * KEY_GROUPS, 16, t)
    flat_spec = pl.BlockSpec((PEER_HEADS * KEY_GROUPS, 16, tr), lambda i: (0, 0, i))
    return pl.pallas_call(
        _peer_route_kernel,
        grid=(t // tr,),
        in_specs=[pl.BlockSpec((tr, qp.shape[1]), lambda i: (i, 0)),
                  pl.BlockSpec(keys.shape, lambda i: (0, 0, 0, 0))],
        out_specs=[flat_spec, flat_spec, spec, spec],
        out_shape=[jax.ShapeDtypeStruct(flat, MXU_DTYPE), jax.ShapeDtypeStruct(flat, MXU_DTYPE),
                   jax.ShapeDtypeStruct(shape, F32), jax.ShapeDtypeStruct(shape, F32)],
        compiler_params=_params(1),
        name="peer_route",
    )(qp, keys)


def _twice_gelu(x):
    return x * (1.0 + lax.erf(x * (2.0 ** -0.5)))


def _row_tile(row, rows):
    packed = jnp.broadcast_to(row, (16, 128)).astype(MXU_DTYPE)
    return jnp.concatenate([packed] * (rows // 16), axis=0)


def _peer_kernel(xt_ref, res_ref, u_ref, v_ref, r2_ref, b_ref, c_ref, a_ref, g_ref, bias_ref,
                 o_ref, ob_ref, st_ref, wt_ref, *, n_i, tb, alpha):
    e = pl.program_id(1)

    @pl.when(e == 0)
    def _():
        o_ref[...] = jnp.zeros_like(o_ref)

    st_ref[...] = jnp.dot(u_ref[...], xt_ref[...], preferred_element_type=F32)
    zero = jnp.zeros((PEER_N_KEYS, 128), MXU_DTYPE)
    assert 8 % n_i == 0
    per_group = 8 // n_i
    within = e % per_group

    def key_rows(ref, hd, lanes):
        grp = ref[hd, 0, :, lanes]
        rows = grp[:n_i]
        for k in range(1, per_group):
            rows = jnp.where(within == k, grp[k * n_i:(k + 1) * n_i], rows)
        return rows

    live = min(n_i, 4)
    for l in range(tb // 128):
        lanes = slice(l * 128, (l + 1) * 128)
        for i0 in range(0, n_i, live):
            gates = [zero for _ in range(live)]
            for hd in range(PEER_HEADS):
                r2 = jnp.concatenate([r2_ref[hd * KEY_GROUPS + g, :, lanes] for g in range(KEY_GROUPS)],
                                     axis=0)
                b = jnp.concatenate([b_ref[hd * KEY_GROUPS + g, :, lanes] for g in range(KEY_GROUPS)],
                                    axis=0)
                counts = key_rows(c_ref, hd, lanes)
                a_rows = key_rows(a_ref, hd, lanes)
                for k in range(live):
                    count = _row_tile(counts[i0 + k:i0 + k + 1], PEER_N_KEYS)
                    a = _row_tile(a_rows[i0 + k:i0 + k + 1], PEER_N_KEYS)
                    gates[k] = gates[k] + jnp.where(r2 < count, b, zero) * a
            for k in range(live):
                rows = slice((i0 + k) * PEER_N_KEYS, (i0 + k + 1) * PEER_N_KEYS)
                act = _twice_gelu(st_ref[rows, lanes]).astype(MXU_DTYPE)
                wt_ref[rows, lanes] = act * gates[k]
    o_ref[...] += lax.dot_general(wt_ref[...], v_ref[...], TN_DIMS, preferred_element_type=F32)

    @pl.when(e == pl.num_programs(1) - 1)
    def _():
        y = _layer_norm_rows(alpha * res_ref[...] + o_ref[...], g_ref[...], bias_ref[...])
        o_ref[...] = y
        ob_ref[...] = y.astype(ob_ref.dtype)


def _peer_experts(xt, res, u, v, r2, b, c, a, gain, bias, alpha, tb=512, n_i=8):
    t, d = res.shape
    n_exp = u.shape[0]
    eb = n_i * PEER_N_KEYS
    tb = min(tb, t)
    tok = lambda i, e: (i, 0)
    per_group = 8 // n_i
    c = c.reshape(PEER_HEADS, PEER_N_KEYS // 8, 8, t)
    a = a.reshape(PEER_HEADS, PEER_N_KEYS // 8, 8, t)
    full = pl.BlockSpec((PEER_HEADS, 1, 8, tb), lambda i, e: (0, e // per_group, 0, i))
    flat = pl.BlockSpec((PEER_HEADS * KEY_GROUPS, 16, tb), lambda i, e: (0, 0, i))
    return pl.pallas_call(
        functools.partial(_peer_kernel, n_i=n_i, tb=tb, alpha=alpha),
        grid=(t // tb, n_exp // eb),
        in_specs=[pl.BlockSpec((d, tb), lambda i, e: (0, i)), pl.BlockSpec((tb, d), tok),
                  pl.BlockSpec((eb, d), lambda i, e: (e, 0)),
                  pl.BlockSpec((eb, d), lambda i, e: (e, 0)),
                  flat, flat, full, full,
                  pl.BlockSpec((1, d), lambda i, e: (0, 0)), pl.BlockSpec((1, d), lambda i, e: (0, 0))],
        out_specs=[pl.BlockSpec((tb, d), tok), pl.BlockSpec((tb, d), tok)],
        out_shape=[jax.ShapeDtypeStruct((t, d), F32), jax.ShapeDtypeStruct((t, d), MXU_DTYPE)],
        scratch_shapes=[pltpu.VMEM((eb, tb), F32), pltpu.VMEM((eb, tb), MXU_DTYPE)],
        compiler_params=_params(2),
        name="peer_experts",
    )(xt, res, u, v, r2, b, c, a, gain.reshape(1, d), bias.reshape(1, d))


def _peer_layer(x, xb, w_query, keys, u, v, gain, bias, alpha):
    qp = _matmul(xb, w_query, F32, bn=w_query.shape[1])
    r2, b, c, a = _peer_route(qp, keys)
    return _peer_experts(xb.T, x, u, v, r2, b, c, a, gain, bias, alpha)


def _even_mixer(x2d, batch, seq, w_in, q_norm, w_q_b, kv_norm, w_kv_b, tables):
    d = x2d.shape[1]
    w_in_p = jnp.concatenate(
        [w_in, jnp.zeros((d, 4096 - w_in.shape[1]), w_in.dtype)], axis=1).astype(MXU_DTYPE)
    h = _matmul(x2d, w_in_p, F32)
    cos_a, sin_a, cos_b, sin_lo, sin_hi = tables
    qa, ka, cqn, ckvn, kr = _even_prep(h, tables, q_norm, kv_norm, seq)
    out_a = _dilated_attention(qa, ka, h, batch, seq)
    wq = w_q_b.reshape(MLA_Q_LORA, B_HEADS, MLA_NOPE + MLA_ROPE)
    wq = jnp.pad(wq, ((0, 0), (0, 0), (0, MLA_QK_PAD - MLA_NOPE - MLA_ROPE)))
    wq = wq.reshape(MLA_Q_LORA, B_HEADS * MLA_QK_PAD).astype(MXU_DTYPE)
    wkv = w_kv_b.reshape(MLA_KV_LORA, B_HEADS, MLA_NOPE + MLA_V)
    wkv = jnp.concatenate([wkv[:, :, :MLA_NOPE].reshape(MLA_KV_LORA, -1),
                           wkv[:, :, MLA_NOPE:].reshape(MLA_KV_LORA, -1)], axis=1).astype(MXU_DTYPE)
    qf = _matmul(cqn, wq, F32)
    kvf = _matmul(ckvn, wkv, MXU_DTYPE)
    qm, km = _mla_prep(qf, kvf, kr, (cos_b, sin_lo, sin_hi), seq)
    out_b = _mla_attention(qm, km, kvf, batch, seq)
    return out_a, out_b


def kernel(x, a_w_in, b_q_norm, b_w_q_b, b_kv_norm, b_w_kv_b, ab_w_out, c_w_in, c_w_out,
           peer_w_query, peer_sub_keys, peer_u, peer_v, ln_gain, ln_bias):
    batch, seq, d = x.shape
    depth = peer_u.shape[0]
    alpha = (2 * depth) ** 0.25
    tables = _rope_tables(seq)
    u_all, v_all = peer_u.astype(MXU_DTYPE), peer_v.astype(MXU_DTYPE)
    xf = x.reshape(batch * seq, d)
    xb = None
    for layer in range(depth):
        i = layer // 2
        if layer % 2 == 0:
            src = xf if xb is None else xb
            out_a, out_b = _even_mixer(src, batch, seq, a_w_in[i], b_q_norm[i], b_w_q_b[i],
                                       b_kv_norm[i], b_w_kv_b[i], tables)
            w_out = ab_w_out[i].astype(MXU_DTYPE)
            half = A_HEADS * HEAD_DIM
            xf, xb = _matmul_residual_ln([out_a, out_b], [w_out[:half], w_out[half:]], xf,
                                         ln_gain[layer, 0], ln_bias[layer, 0], alpha)
        else:
            src = xf if xb is None else xb
            n_q = C_HEADS * HEAD_DIM
            w_qkv = jnp.concatenate([c_w_in[i][:, :n_q] * (HEAD_DIM ** -0.5 * LOG2_E),
                                     c_w_in[i][:, n_q:]], axis=1).astype(MXU_DTYPE)
            qkv = _matmul(src, w_qkv, MXU_DTYPE)
            o = _stick_attention(qkv, batch, seq)
            xf, xb = _matmul_residual_ln([o], [c_w_out[i].astype(MXU_DTYPE)], xf,
                                         ln_gain[layer, 0], ln_bias[layer, 0], alpha)
        xf, xb = _peer_layer(xf, xb, peer_w_query[layer].astype(MXU_DTYPE),
                         peer_sub_keys[layer].astype(MXU_DTYPE),
                         u_all[layer], v_all[layer],
                         ln_gain[layer, 1], ln_bias[layer, 1], alpha)
    return xf.reshape(batch, seq, d)
```

```python
import functools

import jax
import jax.numpy as jnp
from jax import lax
from jax.experimental import pallas as pl
from jax.experimental.pallas import tpu as pltpu

F32 = jnp.float32
MXU_DTYPE = jnp.bfloat16

HEAD_DIM = 128
BLOCK = 128
DIL_STEPS = 128
DILATIONS = (1, 4, 16)
DIL_CHUNK = BLOCK * DILATIONS[-1]
A_HEADS = 8
B_HEADS = 8
MLA_Q_LORA = 512
MLA_KV_LORA = 256
MLA_NOPE = 128
MLA_ROPE = 64
MLA_V = 128
MLA_QK_PAD = 256
C_HEADS = 16
PEER_HEADS = 8
PEER_N_KEYS = 128
PEER_TOPK = 16
PEER_HALF = 128
KEY_GROUPS = PEER_N_KEYS // 16
ROPE_THETA = 10000.0
LN_EPS = 1e-5
RMS_EPS = 1e-6
NEG_INF = -1e30
LOG2_E = 1.4426950408889634
MLA_Q_SCALE = (MLA_NOPE + MLA_ROPE) ** -0.5 * LOG2_E

V7X_VMEM_LIMIT_BYTES = 56 * 1024 * 1024
NT_DIMS = (((1,), (1,)), ((), ()))
TN_DIMS = (((0,), (0,)), ((), ()))


def _params(n_axes):
    return pltpu.CompilerParams(dimension_semantics=("arbitrary",) * n_axes,
                                vmem_limit_bytes=V7X_VMEM_LIMIT_BYTES)


def _mm_kernel(a_ref, b_ref, o_ref):
    a = a_ref[...].astype(MXU_DTYPE)
    o_ref[...] = jnp.dot(a, b_ref[...], preferred_element_type=F32).astype(o_ref.dtype)


def _matmul(a, b, out_dtype, bm=1024, bn=1024):
    m, k = a.shape
    n = b.shape[1]
    bm, bn = min(bm, m), min(bn, n)
    assert m % bm == 0 and n % bn == 0
    return pl.pallas_call(
        _mm_kernel,
        grid=(m // bm, n // bn),
        in_specs=[pl.BlockSpec((bm, k), lambda i, j: (i, 0)),
                  pl.BlockSpec((k, bn), lambda i, j: (0, j))],
        out_specs=pl.BlockSpec((bm, bn), lambda i, j: (i, j)),
        out_shape=jax.ShapeDtypeStruct((m, n), out_dtype),
        compiler_params=_params(2),
        name="matmul",
    )(a, b)


def _layer_norm_rows(y, g, b):
    mu = jnp.mean(y, axis=-1, keepdims=True)
    d = y - mu
    var = jnp.mean(d * d, axis=-1, keepdims=True)
    return d * lax.rsqrt(var + LN_EPS) * g + b


def _mm_ln_kernel(*refs, n_pairs, alpha):
    a_refs = refs[:n_pairs]
    w_refs = refs[n_pairs:2 * n_pairs]
    res_ref, g_ref, b_ref, o_ref, ob_ref = refs[2 * n_pairs:]
    acc = jnp.dot(a_refs[0][...].astype(MXU_DTYPE), w_refs[0][...], preferred_element_type=F32)
    for a_ref, w_ref in zip(a_refs[1:], w_refs[1:]):
        acc = acc + jnp.dot(a_ref[...].astype(MXU_DTYPE), w_ref[...], preferred_element_type=F32)
    y = _layer_norm_rows(alpha * res_ref[...] + acc, g_ref[...], b_ref[...])
    o_ref[...] = y
    ob_ref[...] = y.astype(ob_ref.dtype)


def _matmul_residual_ln(a_list, w_list, res, gain, bias, alpha, bm=256):
    m, n = res.shape
    n_pairs = len(a_list)
    in_specs = [pl.BlockSpec((bm, a.shape[1]), lambda i: (i, 0)) for a in a_list]
    in_specs += [pl.BlockSpec(w.shape, lambda i: (0, 0)) for w in w_list]
    in_specs += [pl.BlockSpec((bm, n), lambda i: (i, 0)),
                 pl.BlockSpec((1, n), lambda i: (0, 0)),
                 pl.BlockSpec((1, n), lambda i: (0, 0))]
    return pl.pallas_call(
        functools.partial(_mm_ln_kernel, n_pairs=n_pairs, alpha=alpha),
        grid=(m // bm,),
        in_specs=in_specs,
        out_specs=[pl.BlockSpec((bm, n), lambda i: (i, 0)), pl.BlockSpec((bm, n), lambda i: (i, 0))],
        out_shape=[jax.ShapeDtypeStruct((m, n), F32), jax.ShapeDtypeStruct((m, n), MXU_DTYPE)],
        compiler_params=_params(1),
        name="matmul_residual_ln",
    )(*a_list, *w_list, res, gain.reshape(1, n), bias.reshape(1, n))


def _rope_tables(seq):
    pos = jnp.arange(seq, dtype=F32)[:, None]

    def cos_sin(half):
        inv = ROPE_THETA ** (-jnp.arange(half, dtype=F32) / half)
        ang = pos * inv[None, :]
        return jnp.cos(ang), jnp.sin(ang)

    c, s = cos_sin(HEAD_DIM // 2)
    cos_a = jnp.concatenate([c, c], axis=1)
    sin_a = jnp.concatenate([-s, s], axis=1)
    c, s = cos_sin(MLA_ROPE // 2)
    z32 = jnp.zeros_like(c)
    cos_b = jnp.concatenate([c, c, z32, z32], axis=1)
    sin_lo = jnp.concatenate([-s, z32, z32, z32], axis=1)
    sin_hi = jnp.concatenate([z32, s, z32, z32], axis=1)
    return cos_a, sin_a, cos_b, sin_lo, sin_hi


def _rope128(x, cos_a, sin_a):
    return x * cos_a + pltpu.roll(x, HEAD_DIM // 2, 1) * sin_a


def _rope64(x, cos_b, sin_lo, sin_hi):
    return x * cos_b + pltpu.roll(x, 96, 1) * sin_lo + pltpu.roll(x, 32, 1) * sin_hi


def _rms_norm_rows(x, g):
    return x * lax.rsqrt(jnp.mean(x * x, axis=-1, keepdims=True) + RMS_EPS) * g


def _even_prep_kernel(qk_ref, tail_ref, cos_a_ref, sin_a_ref, cos_b_ref, sin_lo_ref, sin_hi_ref,
                      qn_ref, kvn_ref, qa_ref, ka_ref, cq_ref, ckv_ref, kr_ref):
    cos_a, sin_a = cos_a_ref[...], sin_a_ref[...]
    n_qk = A_HEADS * HEAD_DIM
    for hd in range(A_HEADS):
        lo = hd * HEAD_DIM
        qa_ref[:, lo:lo + HEAD_DIM] = _rope128(qk_ref[:, lo:lo + HEAD_DIM], cos_a, sin_a)
        ka_ref[:, lo:lo + HEAD_DIM] = _rope128(qk_ref[:, n_qk + lo:n_qk + lo + HEAD_DIM], cos_a, sin_a)
    cq_ref[...] = _rms_norm_rows(tail_ref[:, :MLA_Q_LORA], qn_ref[...]).astype(cq_ref.dtype)
    o2 = MLA_Q_LORA + MLA_KV_LORA
    ckv_ref[...] = _rms_norm_rows(tail_ref[:, MLA_Q_LORA:o2], kvn_ref[...]).astype(ckv_ref.dtype)
    kr = _rope64(tail_ref[:, o2:o2 + 128], cos_b_ref[...], sin_lo_ref[...], sin_hi_ref[...])
    kr_ref[...] = kr.astype(kr_ref.dtype)


def _even_prep(h, tables, q_norm, kv_norm, seq, rows=256):
    t = h.shape[0]
    n_qk = A_HEADS * HEAD_DIM
    sb = seq // rows
    tab_spec = pl.BlockSpec((rows, 128), lambda i: (i % sb, 0))
    return pl.pallas_call(
        _even_prep_kernel,
        grid=(t // rows,),
        in_specs=[pl.BlockSpec((rows, 2 * n_qk), lambda i: (i, 0)),
                  pl.BlockSpec((rows, 1024), lambda i: (i, 3)),
                  tab_spec, tab_spec, tab_spec, tab_spec, tab_spec,
                  pl.BlockSpec((1, MLA_Q_LORA), lambda i: (0, 0)),
                  pl.BlockSpec((1, MLA_KV_LORA), lambda i: (0, 0))],
        out_specs=[pl.BlockSpec((rows, n_qk), lambda i: (i, 0)),
                   pl.BlockSpec((rows, n_qk), lambda i: (i, 0)),
                   pl.BlockSpec((rows, MLA_Q_LORA), lambda i: (i, 0)),
                   pl.BlockSpec((rows, MLA_KV_LORA), lambda i: (i, 0)),
                   pl.BlockSpec((rows, 128), lambda i: (i, 0))],
        out_shape=[jax.ShapeDtypeStruct((t, n_qk), F32),
                   jax.ShapeDtypeStruct((t, n_qk), F32),
                   jax.ShapeDtypeStruct((t, MLA_Q_LORA), MXU_DTYPE),
                   jax.ShapeDtypeStruct((t, MLA_KV_LORA), MXU_DTYPE),
                   jax.ShapeDtypeStruct((t, 128), MXU_DTYPE)],
        compiler_params=_params(1),
        name="even_prep",
    )(h, h, *tables, q_norm.reshape(1, -1), kv_norm.reshape(1, -1))


def _mla_prep_kernel(q_ref, kn_ref, kr_ref, cos_b_ref, sin_lo_ref, sin_hi_ref, qm_ref, km_ref):
    cos_b, sin_lo, sin_hi = cos_b_ref[...], sin_lo_ref[...], sin_hi_ref[...]
    kr = kr_ref[...]
    for hd in range(B_HEADS):
        lo = hd * MLA_QK_PAD
        qm_ref[:, lo:lo + MLA_NOPE] = (q_ref[:, lo:lo + MLA_NOPE] * MLA_Q_SCALE).astype(qm_ref.dtype)
        q_rope = _rope64(q_ref[:, lo + MLA_NOPE:lo + MLA_QK_PAD], cos_b, sin_lo, sin_hi)
        qm_ref[:, lo + MLA_NOPE:lo + MLA_QK_PAD] = (q_rope * MLA_Q_SCALE).astype(qm_ref.dtype)
        km_ref[:, lo:lo + MLA_NOPE] = kn_ref[:, hd * MLA_NOPE:(hd + 1) * MLA_NOPE]
        km_ref[:, lo + MLA_NOPE:lo + MLA_QK_PAD] = kr


def _mla_prep(qf, kvf, kr, tables, seq, rows=256):
    t = qf.shape[0]
    w = B_HEADS * MLA_QK_PAD
    sb = seq // rows
    tab_spec = pl.BlockSpec((rows, 128), lambda i: (i % sb, 0))
    return pl.pallas_call(
        _mla_prep_kernel,
        grid=(t // rows,),
        in_specs=[pl.BlockSpec((rows, w), lambda i: (i, 0)),
                  pl.BlockSpec((rows, B_HEADS * MLA_NOPE), lambda i: (i, 0)),
                  pl.BlockSpec((rows, 128), lambda i: (i, 0)),
                  tab_spec, tab_spec, tab_spec],
        out_specs=[pl.BlockSpec((rows, w), lambda i: (i, 0)), pl.BlockSpec((rows, w), lambda i: (i, 0))],
        out_shape=[jax.ShapeDtypeStruct((t, w), MXU_DTYPE), jax.ShapeDtypeStruct((t, w), MXU_DTYPE)],
        compiler_params=_params(1),
        name="mla_prep",
    )(qf, kvf, kr, *tables)


def _dilated_kernel(q_ref, kc_ref, kp_ref, vc_ref, vp_ref, o_ref,
                    kbuf, vbuf, o0, o1, o2, l0, l1, l2):
    c = pl.program_id(2)
    kbuf[:DIL_CHUNK, :] = kp_ref[...]
    kbuf[DIL_CHUNK:, :] = kc_ref[...]
    vbuf[:DIL_CHUNK, :] = vp_ref[...]
    vbuf[DIL_CHUNK:, :] = vc_ref[...]
    scale = HEAD_DIM ** -0.5
    row = lax.broadcasted_iota(jnp.int32, (BLOCK, 2 * BLOCK), 0)
    col = lax.broadcasted_iota(jnp.int32, (BLOCK, 2 * BLOCK), 1)
    band = (col >= row) & (col <= row + DIL_STEPS)

    def blocks(specs):
        staged = []
        for q_start, k_start, dil, first, o_g, l_g in specs:
            if dil == 1:
                q_idx, k_idx = pl.ds(q_start, BLOCK), pl.ds(k_start, 2 * BLOCK)
            else:
                q_idx = pl.ds(q_start, BLOCK, stride=dil)
                k_idx = pl.ds(k_start, 2 * BLOCK, stride=dil)
            q = q_ref[q_idx, :].astype(MXU_DTYPE)
            k = kbuf[k_idx, :].astype(MXU_DTYPE)
            v = vbuf[k_idx, :].astype(MXU_DTYPE)
            logits = lax.dot_general(q, k, NT_DIMS, preferred_element_type=F32) * scale
            staged.append((logits, v, q_idx, first, o_g, l_g))
        weighted = []
        for logits, v, q_idx, first, o_g, l_g in staged:
            valid_from = jnp.where(first, BLOCK, 0)
            logits = jnp.where(band & (col >= valid_from), logits, NEG_INF)
            m = jnp.max(logits, axis=-1, keepdims=True)
            p = jnp.exp(logits - m)
            denom = jnp.sum(p, axis=-1, keepdims=True)
            weighted.append(((p / denom).astype(MXU_DTYPE), m + jnp.log(denom)))
        for (logits, v, q_idx, first, o_g, l_g), (pn, lse) in zip(staged, weighted):
            o_g[q_idx, :] = jnp.dot(pn, v, preferred_element_type=F32)
            l_g[q_idx, :] = jnp.broadcast_to(lse, (BLOCK, HEAD_DIM))

    n_blocks = DIL_CHUNK // BLOCK
    per_trip = 2

    def body(j, carry):
        specs = []
        for u in range(per_trip):
            i = j * per_trip + u
            specs.append((pl.multiple_of(i * BLOCK, BLOCK),
                          pl.multiple_of(DIL_CHUNK + (i - 1) * BLOCK, BLOCK),
                          DILATIONS[0], (c == 0) & (i == 0), o0, l0))
            d1 = DILATIONS[1]
            n, r = i // d1, i % d1
            span = BLOCK * d1
            specs.append((n * span + r, DIL_CHUNK + (n - 1) * span + r, d1, (c == 0) & (n == 0), o1, l1))
            specs.append((i, DIL_CHUNK - BLOCK * DILATIONS[2] + i, DILATIONS[2], c == 0, o2, l2))
        blocks(specs)
        return carry

    lax.fori_loop(0, n_blocks // per_trip, body, 0)

    la, lb, lc = l0[...], l1[...], l2[...]
    mx = jnp.maximum(jnp.maximum(la, lb), lc)
    ea, eb, ec = jnp.exp(la - mx), jnp.exp(lb - mx), jnp.exp(lc - mx)
    merged = (ea * o0[...] + eb * o1[...] + ec * o2[...]) / (ea + eb + ec)
    o_ref[...] = merged.astype(o_ref.dtype)


def _dilated_attention(qa, ka, h, batch, seq):
    t = qa.shape[0]
    nc = seq // DIL_CHUNK
    v_col0 = 2 * A_HEADS
    cur = lambda b, hd, c: (b * nc + c, hd)
    prev = lambda b, hd, c: (b * nc + jnp.maximum(c - 1, 0), hd)
    blk = (DIL_CHUNK, HEAD_DIM)
    return pl.pallas_call(
        _dilated_kernel,
        grid=(batch, A_HEADS, nc),
        in_specs=[pl.BlockSpec(blk, cur),
                  pl.BlockSpec(blk, cur), pl.BlockSpec(blk, prev),
                  pl.BlockSpec(blk, lambda b, hd, c: (b * nc + c, v_col0 + hd)),
                  pl.BlockSpec(blk, lambda b, hd, c: (b * nc + jnp.maximum(c - 1, 0), v_col0 + hd))],
        out_specs=pl.BlockSpec(blk, cur),
        out_shape=jax.ShapeDtypeStruct((t, A_HEADS * HEAD_DIM), MXU_DTYPE),
        scratch_shapes=[pltpu.VMEM((2 * DIL_CHUNK, HEAD_DIM), F32)] * 2
                       + [pltpu.VMEM(blk, F32)] * 6,
        compiler_params=_params(3),
        name="dilated_attention",
    )(qa, ka, ka, h, h)


def _mla_kernel(q_ref, k_ref, v_ref, o_ref, m_ref, l_ref, acc_ref, *, bq, bk, unroll):
    qi = pl.program_id(2)
    n_sub = bq // bk
    m_ref[...] = jnp.full_like(m_ref, NEG_INF)
    l_ref[...] = jnp.zeros_like(l_ref)
    acc_ref[...] = jnp.zeros_like(acc_ref)

    def group(k_starts, r0, diagonal):
        cols = bq - r0
        q = q_ref[r0:, :]
        ss = [lax.dot_general(k_ref[pl.ds(ks, bk), :], q, NT_DIMS, preferred_element_type=F32)
              for ks in k_starts]
        if diagonal:
            key = lax.broadcasted_iota(jnp.int32, (bk, cols), 0)
            qry = lax.broadcasted_iota(jnp.int32, (bk, cols), 1)
            ss = [jnp.where(key <= qry, s, NEG_INF) for s in ss]
        m_prev = m_ref[:, r0:]
        m_new = m_prev
        for s in ss:
            m_new = jnp.maximum(m_new, jnp.max(s, axis=0, keepdims=True))
        alpha = jnp.exp2(m_prev - m_new)
        l = alpha * l_ref[:, r0:]
        acc = alpha * acc_ref[:, r0:]
        for ks, s in zip(k_starts, ss):
            p = jnp.exp2(s - m_new)
            l = l + jnp.sum(p, axis=0, keepdims=True)
            acc = acc + lax.dot_general(v_ref[pl.ds(ks, bk), :], p.astype(MXU_DTYPE), TN_DIMS,
                                        preferred_element_type=F32)
        m_ref[:, r0:] = m_new
        l_ref[:, r0:] = l
        acc_ref[:, r0:] = acc

    def body(j, carry):
        group([pl.multiple_of((j * unroll + u) * bk, bk) for u in range(unroll)], 0, False)
        return carry

    assert n_sub % unroll == 0
    lax.fori_loop(0, qi * (n_sub // unroll), body, 0)
    for jj in range(n_sub):
        group([pl.multiple_of(qi * bq + jj * bk, bk)], jj * bk, True)
    o_ref[...] = (acc_ref[...] / l_ref[...]).T.astype(o_ref.dtype)


def _mla_attention(qm, km, kvf, batch, seq, bq=2048, bk=512, unroll=4):
    t = qm.shape[0]
    bq, bk = min(bq, seq), min(bk, seq)
    nq = seq // bq
    return pl.pallas_call(
        functools.partial(_mla_kernel, bq=bq, bk=bk, unroll=min(unroll, bq // bk)),
        grid=(batch, B_HEADS, nq),
        in_specs=[pl.BlockSpec((bq, MLA_QK_PAD), lambda b, hd, i: (b * nq + i, hd)),
                  pl.BlockSpec((seq, MLA_QK_PAD), lambda b, hd, i: (b, hd)),
                  pl.BlockSpec((seq, MLA_V), lambda b, hd, i: (b, B_HEADS + hd))],
        out_specs=pl.BlockSpec((bq, MLA_V), lambda b, hd, i: (b * nq + i, hd)),
        out_shape=jax.ShapeDtypeStruct((t, B_HEADS * MLA_V), MXU_DTYPE),
        scratch_shapes=[pltpu.VMEM((1, bq), F32), pltpu.VMEM((1, bq), F32),
                        pltpu.VMEM((MLA_V, bq), F32)],
        compiler_params=_params(3),
        name="mla_attention",
    )(qm, km, kvf)


def _stick_kernel(q_ref, k_ref, v_ref, o_ref, acc_ref, run_ref, *, bq, bk, unroll):
    qi = pl.program_id(2)
    n_sub = bq // bk
    tri_r = lax.broadcasted_iota(jnp.int32, (bk, bk), 0)
    tri_c = lax.broadcasted_iota(jnp.int32, (bk, bk), 1)
    at_or_after = jnp.where(tri_r >= tri_c, 1.0, 0.0).astype(MXU_DTYPE)
    acc_ref[...] = jnp.zeros_like(acc_ref)
    run_ref[...] = jnp.zeros_like(run_ref)

    def group(tiles):
        staged = []
        for ks, r0, diagonal in tiles:
            rows = bq - r0
            z = lax.dot_general(q_ref[r0:, :], k_ref[pl.ds(ks, bk), :], NT_DIMS,
                                preferred_element_type=F32)
            neg_abs = lax.bitcast_convert_type(
                lax.bitcast_convert_type(z, jnp.uint32) | jnp.uint32(0x80000000), F32)
            sp = jnp.maximum(z, 0.0) + jnp.log(1.0 + jnp.exp2(neg_abs)) * LOG2_E
            mask = None
            if diagonal:
                row = lax.broadcasted_iota(jnp.int32, (rows, bk), 0)
                col = lax.broadcasted_iota(jnp.int32, (rows, bk), 1)
                mask = col < row
                sp = jnp.where(mask, sp, 0.0)
            staged.append((z, sp, mask))
        sufs = []
        for _, sp, _ in staged:
            hi = sp.astype(MXU_DTYPE)
            lo = (sp - hi.astype(F32)).astype(MXU_DTYPE)
            sufs.append(jnp.dot(hi, at_or_after, preferred_element_type=F32)
                        + jnp.dot(lo, at_or_after, preferred_element_type=F32))
        run = run_ref[...]
        acc = acc_ref[...]
        for (ks, r0, _), (z, _, mask), suf in zip(tiles, staged, sufs):
            att = jnp.exp2(z - suf - run[r0:])
            if mask is not None:
                att = jnp.where(mask, att, 0.0)
            part = jnp.dot(att.astype(MXU_DTYPE), v_ref[pl.ds(ks, bk), :], preferred_element_type=F32)
            total = suf[:, :1]
            if r0 == 0:
                acc, run = acc + part, run + total
            else:
                acc = jnp.concatenate([acc[:r0], acc[r0:] + part], axis=0)
                run = jnp.concatenate([run[:r0], run[r0:] + total], axis=0)
        acc_ref[...] = acc
        run_ref[...] = run

    group([(pl.multiple_of(qi * bq + jj * bk, bk), jj * bk, True) for jj in reversed(range(n_sub))])

    def body(j, carry):
        group([(pl.multiple_of((qi * n_sub - 1 - j * unroll - u) * bk, bk), 0, False)
               for u in range(unroll)])
        return carry

    assert n_sub % unroll == 0
    lax.fori_loop(0, qi * (n_sub // unroll), body, 0)
    o_ref[...] = acc_ref[...].astype(o_ref.dtype)


def _stick_attention(qkv, batch, seq, bq=1024, bk=256, unroll=4):
    t = qkv.shape[0]
    bq, bk = min(bq, seq), min(bk, seq)
    nq = seq // bq
    return pl.pallas_call(
        functools.partial(_stick_kernel, bq=bq, bk=bk, unroll=min(unroll, bq // bk)),
        grid=(batch, C_HEADS, nq),
        in_specs=[pl.BlockSpec((bq, HEAD_DIM), lambda b, hd, i: (b * nq + i, hd)),
                  pl.BlockSpec((seq, HEAD_DIM), lambda b, hd, i: (b, C_HEADS + hd)),
                  pl.BlockSpec((seq, HEAD_DIM), lambda b, hd, i: (b, 2 * C_HEADS + hd))],
        out_specs=pl.BlockSpec((bq, HEAD_DIM), lambda b, hd, i: (b * nq + i, hd)),
        out_shape=jax.ShapeDtypeStruct((t, C_HEADS * HEAD_DIM), MXU_DTYPE),
        scratch_shapes=[pltpu.VMEM((bq, HEAD_DIM), F32), pltpu.VMEM((bq, 1), F32)],
        compiler_params=_params(3),
        name="stick_breaking_attention",
    )(qkv, qkv, qkv)


def _top_values(s, count, with_rank=False):
    vals = []
    cur = s
    rank = jnp.full(s.shape, float(count), F32) if with_rank else None
    for k in range(count):
        m = jnp.max(cur, axis=0, keepdims=True)
        vals.append(m)
        hit = cur == m
        if with_rank:
            rank = jnp.where(hit, float(k), rank)
        cur = jnp.where(hit, NEG_INF, cur)
    return (vals, rank) if with_rank else vals


def _ranked(s, count):
    n = s.shape[0]
    rows = lax.broadcasted_iota(jnp.int32, s.shape, 0)
    vals = []
    cur = s
    rank = jnp.full(s.shape, float(count), F32)
    for k in range(count):
        m = jnp.max(cur, axis=0, keepdims=True)
        first = jnp.min(jnp.where(cur == m, rows, n), axis=0, keepdims=True)
        pick = rows == first
        vals.append(m)
        rank = jnp.where(pick, float(k), rank)
        cur = jnp.where(pick, NEG_INF, cur)
    return vals, rank


def _pair_sums(v1, v2):
    v1_all = jnp.concatenate(v1, axis=0)
    v2_all = jnp.concatenate(v2, axis=0)
    return jnp.concatenate([v1[0] + v2_all]
                           + [v1[p] + v2_all[:8] for p in range(1, 8)]
                           + [v2[0] + v1_all[8:]], axis=0)


def _count_at_least(s, threshold):
    return jnp.sum(jnp.where(s >= threshold, 1.0, 0.0), axis=0, keepdims=True)


def _route_distinct(s1, s2):
    v1 = _top_values(s1, PEER_TOPK)
    v2, rank2 = _top_values(s2, PEER_TOPK, with_rank=True)
    cand = _pair_sums(v1, v2)
    tops = _top_values(cand, PEER_TOPK + 1)
    cut = 0.5 * (tops[PEER_TOPK - 1] + tops[PEER_TOPK])
    top = tops[0]
    z = jnp.sum(jnp.where(cand >= cut, jnp.exp(cand - top), 0.0), axis=0, keepdims=True)
    need = cut - s1
    count = jnp.zeros_like(s1)
    for q in range(PEER_TOPK):
        count = count + jnp.where(v2[q] >= need, 1.0, 0.0)
    a = jnp.where(s1 >= v1[PEER_TOPK - 1], jnp.exp(s1 - v1[0]) / (2.0 * z), 0.0)
    b = jnp.where(s2 >= v2[PEER_TOPK - 1], jnp.exp(s2 - v2[0]), 0.0)
    distinct = ((_count_at_least(s1, v1[PEER_TOPK - 1]) == PEER_TOPK)
                & (_count_at_least(s2, v2[PEER_TOPK - 1]) == PEER_TOPK)
                & (_count_at_least(cand, tops[PEER_TOPK]) == PEER_TOPK + 1))
    return rank2, b, count, a, distinct


def _route_with_ties(s1, s2):
    v1, rank1 = _ranked(s1, PEER_TOPK)
    v2, rank2 = _ranked(s2, PEER_TOPK)
    cand = _pair_sums(v1, v2)
    _, cand_rank = _ranked(cand, PEER_TOPK)
    chosen = cand_rank < PEER_TOPK
    top = v1[0] + v2[0]
    z = jnp.sum(jnp.where(chosen, jnp.exp(cand - top), 0.0), axis=0, keepdims=True)
    picked = jnp.where(chosen, 1.0, 0.0)
    per_p = [jnp.sum(picked[:16], axis=0, keepdims=True)]
    per_p += [jnp.sum(picked[8 + 8 * p:16 + 8 * p], axis=0, keepdims=True) for p in range(1, 8)]
    per_p += [picked[64 + p:65 + p] for p in range(8, 16)]
    count = jnp.zeros_like(s1)
    for p in range(PEER_TOPK):
        count = count + jnp.where(rank1 == p, per_p[p], 0.0)
    a = jnp.where(rank1 < PEER_TOPK, jnp.exp(s1 - v1[0]) / (2.0 * z), 0.0)
    b = jnp.where(rank2 < PEER_TOPK, jnp.exp(s2 - v2[0]), 0.0)
    return rank2, b, count, a


def _peer_route_kernel(q_ref, keys_ref, r2_ref, b_ref, c_ref, a_ref):
    def head(hd, carry):
        col = pl.multiple_of(hd * 2 * PEER_HALF, 2 * PEER_HALF)
        q1 = q_ref[:, pl.ds(col, PEER_HALF)].astype(MXU_DTYPE)
        q2 = q_ref[:, pl.ds(col + PEER_HALF, PEER_HALF)].astype(MXU_DTYPE)
        s1 = lax.dot_general(keys_ref[hd, 0], q1, NT_DIMS, preferred_element_type=F32)
        s2 = lax.dot_general(keys_ref[hd, 1], q2, NT_DIMS, preferred_element_type=F32)

        def store(rank2, b, count, a):
            groups = pl.ds(pl.multiple_of(hd * KEY_GROUPS, KEY_GROUPS), KEY_GROUPS)
            tokens = rank2.shape[1]
            r2_ref[groups] = rank2.astype(r2_ref.dtype).reshape(KEY_GROUPS, 16, tokens)
            b_ref[groups] = b.astype(b_ref.dtype).reshape(KEY_GROUPS, 16, tokens)
            c_ref[hd] = count
            a_ref[hd] = a

        rank2, b, count, a, distinct = _route_distinct(s1, s2)
        store(rank2, b, count, a)

        @pl.when(jnp.max(jnp.where(distinct, 0.0, 1.0)) > 0.0)
        def _():
            store(*_route_with_ties(s1, s2))

        return carry

    lax.fori_loop(0, PEER_HEADS, head, 0)


def _peer_route(qp, keys, tr=256):
    t = qp.shape[0]
    shape = (PEER_HEADS, PEER_N_KEYS, t)
    spec = pl.BlockSpec((PEER_HEADS, PEER_N_KEYS, tr), lambda i: (0, 0, i))
    flat = (PEER_HEADS * KEY_GROUPS, 16, t)
    flat_spec = pl.BlockSpec((PEER_HEADS * KEY_GROUPS, 16, tr), lambda i: (0, 0, i))
    return pl.pallas_call(
        _peer_route_kernel,
        grid=(t // tr,),
        in_specs=[pl.BlockSpec((tr, qp.shape[1]), lambda i: (i, 0)),
                  pl.BlockSpec(keys.shape, lambda i: (0, 0, 0, 0))],
        out_specs=[flat_spec, flat_spec, spec, spec],
        out_shape=[jax.ShapeDtypeStruct(flat, MXU_DTYPE), jax.ShapeDtypeStruct(flat, MXU_DTYPE),
                   jax.ShapeDtypeStruct(shape, F32), jax.ShapeDtypeStruct(shape, F32)],
        compiler_params=_params(1),
        name="peer_route",
    )(qp, keys)


def _twice_gelu(x):
    return x * (1.0 + lax.erf(x * (2.0 ** -0.5)))


def _row_tile(row, rows):
    packed = jnp.broadcast_to(row, (16, 128)).astype(MXU_DTYPE)
    return jnp.concatenate([packed] * (rows // 16), axis=0)


def _peer_kernel(xt_ref, res_ref, u_ref, v_ref, r2_ref, b_ref, c_ref, a_ref, g_ref, bias_ref,
                 o_ref, ob_ref, st_ref, wt_ref, *, n_i, tb, alpha):
    e = pl.program_id(1)

    @pl.when(e == 0)
    def _():
        o_ref[...] = jnp.zeros_like(o_ref)

    st_ref[...] = jnp.dot(u_ref[...], xt_ref[...], preferred_element_type=F32)
    zero = jnp.zeros((PEER_N_KEYS, 128), MXU_DTYPE)
    assert 8 % n_i == 0
    per_group = 8 // n_i
    within = e % per_group

    def key_rows(ref, hd, lanes):
        grp = ref[hd, 0, :, lanes]
        rows = grp[:n_i]
        for k in range(1, per_group):
            rows = jnp.where(within == k, grp[k * n_i:(k + 1) * n_i], rows)
        return rows

    live = min(n_i, 4)
    for l in range(tb // 128):
        lanes = slice(l * 128, (l + 1) * 128)
        for i0 in range(0, n_i, live):
            gates = [zero for _ in range(live)]
            for hd in range(PEER_HEADS):
                r2 = jnp.concatenate([r2_ref[hd * KEY_GROUPS + g, :, lanes] for g in range(KEY_GROUPS)],
                                     axis=0)
                b = jnp.concatenate([b_ref[hd * KEY_GROUPS + g, :, lanes] for g in range(KEY_GROUPS)],
                                    axis=0)
                counts = key_rows(c_ref, hd, lanes)
                a_rows = key_rows(a_ref, hd, lanes)
                for k in range(live):
                    count = _row_tile(counts[i0 + k:i0 + k + 1], PEER_N_KEYS)
                    a = _row_tile(a_rows[i0 + k:i0 + k + 1], PEER_N_KEYS)
                    gates[k] = gates[k] + jnp.where(r2 < count, b, zero) * a
            for k in range(live):
                rows = slice((i0 + k) * PEER_N_KEYS, (i0 + k + 1) * PEER_N_KEYS)
                act = _twice_gelu(st_ref[rows, lanes]).astype(MXU_DTYPE)
                wt_ref[rows, lanes] = act * gates[k]
    o_ref[...] += lax.dot_general(wt_ref[...], v_ref[...], TN_DIMS, preferred_element_type=F32)

    @pl.when(e == pl.num_programs(1) - 1)
    def _():
        y = _layer_norm_rows(alpha * res_ref[...] + o_ref[...], g_ref[...], bias_ref[...])
        o_ref[...] = y
        ob_ref[...] = y.astype(ob_ref.dtype)


def _peer_experts(xt, res, u, v, r2, b, c, a, gain, bias, alpha, tb=512, n_i=8):
    t, d = res.shape
    n_exp = u.shape[0]
    eb = n_i * PEER_N_KEYS
    tb = min(tb, t)
    tok = lambda i, e: (i, 0)
    per_group = 8 // n_i
    c = c.reshape(PEER_HEADS, PEER_N_KEYS // 8, 8, t)
    a = a.reshape(PEER_HEADS, PEER_N_KEYS // 8, 8, t)
    full = pl.BlockSpec((PEER_HEADS, 1, 8, tb), lambda i, e: (0, e // per_group, 0, i))
    flat = pl.BlockSpec((PEER_HEADS * KEY_GROUPS, 16, tb), lambda i, e: (0, 0, i))
    return pl.pallas_call(
        functools.partial(_peer_kernel, n_i=n_i, tb=tb, alpha=alpha),
        grid=(t // tb, n_exp // eb),
        in_specs=[pl.BlockSpec((d, tb), lambda i, e: (0, i)), pl.BlockSpec((tb, d), tok),
                  pl.BlockSpec((eb, d), lambda i, e: (e, 0)),
                  pl.BlockSpec((eb, d), lambda i, e: (e, 0)),
                  flat, flat, full, full,
                  pl.BlockSpec((1, d), lambda i, e: (0, 0)), pl.BlockSpec((1, d), lambda i, e: (0, 0))],
        out_specs=[pl.BlockSpec((tb, d), tok), pl.BlockSpec((tb, d), tok)],
        out_shape=[jax.ShapeDtypeStruct((t, d), F32), jax.ShapeDtypeStruct((t, d), MXU_DTYPE)],
        scratch_shapes=[pltpu.VMEM((eb, tb), F32), pltpu.VMEM((eb, tb), MXU_DTYPE)],
        compiler_params=_params(2),
        name="peer_experts",
    )(xt, res, u, v, r2, b, c, a, gain.reshape(1, d), bias.reshape(1, d))


def _peer_layer(x, xb, w_query, keys, u, v, gain, bias, alpha):
    qp = _matmul(xb, w_query, F32, bn=w_query.shape[1])
    r2, b, c, a = _peer_route(qp, keys)
    return _peer_experts(xb.T, x, u, v, r2, b, c, a, gain, bias, alpha)


def _even_mixer(x2d, batch, seq, w_in, q_norm, w_q_b, kv_norm, w_kv_b, tables):
    d = x2d.shape[1]
    w_in_p = jnp.concatenate(
        [w_in, jnp.zeros((d, 4096 - w_in.shape[1]), w_in.dtype)], axis=1).astype(MXU_DTYPE)
    h = _matmul(x2d, w_in_p, F32)
    cos_a, sin_a, cos_b, sin_lo, sin_hi = tables
    qa, ka, cqn, ckvn, kr = _even_prep(h, tables, q_norm, kv_norm, seq)
    out_a = _dilated_attention(qa, ka, h, batch, seq)
    wq = w_q_b.reshape(MLA_Q_LORA, B_HEADS, MLA_NOPE + MLA_ROPE)
    wq = jnp.pad(wq, ((0, 0), (0, 0), (0, MLA_QK_PAD - MLA_NOPE - MLA_ROPE)))
    wq = wq.reshape(MLA_Q_LORA, B_HEADS * MLA_QK_PAD).astype(MXU_DTYPE)
    wkv = w_kv_b.reshape(MLA_KV_LORA, B_HEADS, MLA_NOPE + MLA_V)
    wkv = jnp.concatenate([wkv[:, :, :MLA_NOPE].reshape(MLA_KV_LORA, -1),
                           wkv[:, :, MLA_NOPE:].reshape(MLA_KV_LORA, -1)], axis=1).astype(MXU_DTYPE)
    qf = _matmul(cqn, wq, F32)
    kvf = _matmul(ckvn, wkv, MXU_DTYPE)
    qm, km = _mla_prep(qf, kvf, kr, (cos_b, sin_lo, sin_hi), seq)
    out_b = _mla_attention(qm, km, kvf, batch, seq)
    return out_a, out_b


def kernel(x, a_w_in, b_q_norm, b_w_q_b, b_kv_norm, b_w_kv_b, ab_w_out, c_w_in, c_w_out,
           peer_w_query, peer_sub_keys, peer_u, peer_v, ln_gain, ln_bias):
    batch, seq, d = x.shape
    depth = peer_u.shape[0]
    alpha = (2 * depth) ** 0.25
    tables = _rope_tables(seq)
    u_all, v_all = peer_u.astype(MXU_DTYPE), peer_v.astype(MXU_DTYPE)
    xf = x.reshape(batch * seq, d)
    xb = None
    for layer in range(depth):
        i = layer // 2
        if layer % 2 == 0:
            src = xf if xb is None else xb
            out_a, out_b = _even_mixer(src, batch, seq, a_w_in[i], b_q_norm[i], b_w_q_b[i],
                                       b_kv_norm[i], b_w_kv_b[i], tables)
            w_out = ab_w_out[i].astype(MXU_DTYPE)
            half = A_HEADS * HEAD_DIM
            xf, xb = _matmul_residual_ln([out_a, out_b], [w_out[:half], w_out[half:]], xf,
                                         ln_gain[layer, 0], ln_bias[layer, 0], alpha)
        else:
            src = xf if xb is None else xb
            n_q = C_HEADS * HEAD_DIM
            w_qkv = jnp.concatenate([c_w_in[i][:, :n_q] * (HEAD_DIM ** -0.5 * LOG2_E),
                                     c_w_in[i][:, n_q:]], axis=1).astype(MXU_DTYPE)
            qkv = _matmul(src, w_qkv, MXU_DTYPE)
            o = _stick_attention(qkv, batch, seq)
            xf, xb = _matmul_residual_ln([o], [c_w_out[i].astype(MXU_DTYPE)], xf,
                                         ln_gain[layer, 0], ln_bias[layer, 0], alpha)
        xf, xb = _peer_layer(xf, xb, peer_w_query[layer].astype(MXU_DTYPE),
                         peer_sub_keys[layer].astype(MXU_DTYPE),
                         u_all[layer], v_all[layer],
                         ln_gain[layer, 1], ln_bias[layer, 1], alpha)
    return xf.reshape(batch, seq, d)
```

```python
import functools

import jax
import jax.numpy as jnp
from jax import lax
from jax.experimental import pallas as pl
from jax.experimental.pallas import tpu as pltpu

F32 = jnp.float32
MXU_DTYPE = jnp.bfloat16

HEAD_DIM = 128
BLOCK = 128
DIL_STEPS = 128
DILATIONS = (1, 4, 16)
DIL_CHUNK = BLOCK * DILATIONS[-1]
A_HEADS = 8
B_HEADS = 8
MLA_Q_LORA = 512
MLA_KV_LORA = 256
MLA_NOPE = 128
MLA_ROPE = 64
MLA_V = 128
MLA_QK_PAD = 256
C_HEADS = 16
PEER_HEADS = 8
PEER_N_KEYS = 128
PEER_TOPK = 16
PEER_HALF = 128
KEY_GROUPS = PEER_N_KEYS // 16
ROPE_THETA = 10000.0
LN_EPS = 1e-5
RMS_EPS = 1e-6
NEG_INF = -1e30
LOG2_E = 1.4426950408889634
MLA_Q_SCALE = (MLA_NOPE + MLA_ROPE) ** -0.5 * LOG2_E

V7X_VMEM_LIMIT_BYTES = 56 * 1024 * 1024
NT_DIMS = (((1,), (1,)), ((), ()))
TN_DIMS = (((0,), (0,)), ((), ()))


def _params(n_axes):
    return pltpu.CompilerParams(dimension_semantics=("arbitrary",) * n_axes,
                                vmem_limit_bytes=V7X_VMEM_LIMIT_BYTES)


def _mm_kernel(a_ref, b_ref, o_ref):
    a = a_ref[...].astype(MXU_DTYPE)
    o_ref[...] = jnp.dot(a, b_ref[...], preferred_element_type=F32).astype(o_ref.dtype)


def _matmul(a, b, out_dtype, bm=1024, bn=1024):
    m, k = a.shape
    n = b.shape[1]
    bm, bn = min(bm, m), min(bn, n)
    assert m % bm == 0 and n % bn == 0
    return pl.pallas_call(
        _mm_kernel,
        grid=(m // bm, n // bn),
        in_specs=[pl.BlockSpec((bm, k), lambda i, j: (i, 0)),
                  pl.BlockSpec((k, bn), lambda i, j: (0, j))],
        out_specs=pl.BlockSpec((bm, bn), lambda i, j: (i, j)),
        out_shape=jax.ShapeDtypeStruct((m, n), out_dtype),
        compiler_params=_params(2),
        name="matmul",
    )(a, b)


def _layer_norm_rows(y, g, b):
    mu = jnp.mean(y, axis=-1, keepdims=True)
    d = y - mu
    var = jnp.mean(d * d, axis=-1, keepdims=True)
    return d * lax.rsqrt(var + LN_EPS) * g + b


def _mm_ln_kernel(*refs, n_pairs, alpha):
    a_refs = refs[:n_pairs]
    w_refs = refs[n_pairs:2 * n_pairs]
    res_ref, g_ref, b_ref, o_ref, ob_ref = refs[2 * n_pairs:]
    acc = jnp.dot(a_refs[0][...].astype(MXU_DTYPE), w_refs[0][...], preferred_element_type=F32)
    for a_ref, w_ref in zip(a_refs[1:], w_refs[1:]):
        acc = acc + jnp.dot(a_ref[...].astype(MXU_DTYPE), w_ref[...], preferred_element_type=F32)
    y = _layer_norm_rows(alpha * res_ref[...] + acc, g_ref[...], b_ref[...])
    o_ref[...] = y
    ob_ref[...] = y.astype(ob_ref.dtype)


def _matmul_residual_ln(a_list, w_list, res, gain, bias, alpha, bm=256):
    m, n = res.shape
    n_pairs = len(a_list)
    in_specs = [pl.BlockSpec((bm, a.shape[1]), lambda i: (i, 0)) for a in a_list]
    in_specs += [pl.BlockSpec(w.shape, lambda i: (0, 0)) for w in w_list]
    in_specs += [pl.BlockSpec((bm, n), lambda i: (i, 0)),
                 pl.BlockSpec((1, n), lambda i: (0, 0)),
                 pl.BlockSpec((1, n), lambda i: (0, 0))]
    return pl.pallas_call(
        functools.partial(_mm_ln_kernel, n_pairs=n_pairs, alpha=alpha),
        grid=(m // bm,),
        in_specs=in_specs,
        out_specs=[pl.BlockSpec((bm, n), lambda i: (i, 0)), pl.BlockSpec((bm, n), lambda i: (i, 0))],
        out_shape=[jax.ShapeDtypeStruct((m, n), F32), jax.ShapeDtypeStruct((m, n), MXU_DTYPE)],
        compiler_params=_params(1),
        name="matmul_residual_ln",
    )(*a_list, *w_list, res, gain.reshape(1, n), bias.reshape(1, n))


def _rope_tables(seq):
    pos = jnp.arange(seq, dtype=F32)[:, None]

    def cos_sin(half):
        inv = ROPE_THETA ** (-jnp.arange(half, dtype=F32) / half)
        ang = pos * inv[None, :]
        return jnp.cos(ang), jnp.sin(ang)

    c, s = cos_sin(HEAD_DIM // 2)
    cos_a = jnp.concatenate([c, c], axis=1)
    sin_a = jnp.concatenate([-s, s], axis=1)
    c, s = cos_sin(MLA_ROPE // 2)
    z32 = jnp.zeros_like(c)
    cos_b = jnp.concatenate([c, c, z32, z32], axis=1)
    sin_lo = jnp.concatenate([-s, z32, z32, z32], axis=1)
    sin_hi = jnp.concatenate([z32, s, z32, z32], axis=1)
    return cos_a, sin_a, cos_b, sin_lo, sin_hi


def _rope128(x, cos_a, sin_a):
    return x * cos_a + pltpu.roll(x, HEAD_DIM // 2, 1) * sin_a


def _rope64(x, cos_b, sin_lo, sin_hi):
    return x * cos_b + pltpu.roll(x, 96, 1) * sin_lo + pltpu.roll(x, 32, 1) * sin_hi


def _rms_norm_rows(x, g):
    return x * lax.rsqrt(jnp.mean(x * x, axis=-1, keepdims=True) + RMS_EPS) * g


def _even_prep_kernel(qk_ref, tail_ref, cos_a_ref, sin_a_ref, cos_b_ref, sin_lo_ref, sin_hi_ref,
                      qn_ref, kvn_ref, qa_ref, ka_ref, cq_ref, ckv_ref, kr_ref):
    cos_a, sin_a = cos_a_ref[...], sin_a_ref[...]
    n_qk = A_HEADS * HEAD_DIM
    for hd in range(A_HEADS):
        lo = hd * HEAD_DIM
        qa_ref[:, lo:lo + HEAD_DIM] = _rope128(qk_ref[:, lo:lo + HEAD_DIM], cos_a, sin_a)
        ka_ref[:, lo:lo + HEAD_DIM] = _rope128(qk_ref[:, n_qk + lo:n_qk + lo + HEAD_DIM], cos_a, sin_a)
    cq_ref[...] = _rms_norm_rows(tail_ref[:, :MLA_Q_LORA], qn_ref[...]).astype(cq_ref.dtype)
    o2 = MLA_Q_LORA + MLA_KV_LORA
    ckv_ref[...] = _rms_norm_rows(tail_ref[:, MLA_Q_LORA:o2], kvn_ref[...]).astype(ckv_ref.dtype)
    kr = _rope64(tail_ref[:, o2:o2 + 128], cos_b_ref[...], sin_lo_ref[...], sin_hi_ref[...])
    kr_ref[...] = kr.astype(kr_ref.dtype)


def _even_prep(h, tables, q_norm, kv_norm, seq, rows=256):
    t = h.shape[0]
    n_qk = A_HEADS * HEAD_DIM
    sb = seq // rows
    tab_spec = pl.BlockSpec((rows, 128), lambda i: (i % sb, 0))
    return pl.pallas_call(
        _even_prep_kernel,
        grid=(t // rows,),
        in_specs=[pl.BlockSpec((rows, 2 * n_qk), lambda i: (i, 0)),
                  pl.BlockSpec((rows, 1024), lambda i: (i, 3)),
                  tab_spec, tab_spec, tab_spec, tab_spec, tab_spec,
                  pl.BlockSpec((1, MLA_Q_LORA), lambda i: (0, 0)),
                  pl.BlockSpec((1, MLA_KV_LORA), lambda i: (0, 0))],
        out_specs=[pl.BlockSpec((rows, n_qk), lambda i: (i, 0)),
                   pl.BlockSpec((rows, n_qk), lambda i: (i, 0)),
                   pl.BlockSpec((rows, MLA_Q_LORA), lambda i: (i, 0)),
                   pl.BlockSpec((rows, MLA_KV_LORA), lambda i: (i, 0)),
                   pl.BlockSpec((rows, 128), lambda i: (i, 0))],
        out_shape=[jax.ShapeDtypeStruct((t, n_qk), F32),
                   jax.ShapeDtypeStruct((t, n_qk), F32),
                   jax.ShapeDtypeStruct((t, MLA_Q_LORA), MXU_DTYPE),
                   jax.ShapeDtypeStruct((t, MLA_KV_LORA), MXU_DTYPE),
                   jax.ShapeDtypeStruct((t, 128), MXU_DTYPE)],
        compiler_params=_params(1),
        name="even_prep",
    )(h, h, *tables, q_norm.reshape(1, -1), kv_norm.reshape(1, -1))


def _mla_prep_kernel(q_ref, kn_ref, kr_ref, cos_b_ref, sin_lo_ref, sin_hi_ref, qm_ref, km_ref):
    cos_b, sin_lo, sin_hi = cos_b_ref[...], sin_lo_ref[...], sin_hi_ref[...]
    kr = kr_ref[...]
    for hd in range(B_HEADS):
        lo = hd * MLA_QK_PAD
        qm_ref[:, lo:lo + MLA_NOPE] = (q_ref[:, lo:lo + MLA_NOPE] * MLA_Q_SCALE).astype(qm_ref.dtype)
        q_rope = _rope64(q_ref[:, lo + MLA_NOPE:lo + MLA_QK_PAD], cos_b, sin_lo, sin_hi)
        qm_ref[:, lo + MLA_NOPE:lo + MLA_QK_PAD] = (q_rope * MLA_Q_SCALE).astype(qm_ref.dtype)
        km_ref[:, lo:lo + MLA_NOPE] = kn_ref[:, hd * MLA_NOPE:(hd + 1) * MLA_NOPE]
        km_ref[:, lo + MLA_NOPE:lo + MLA_QK_PAD] = kr


def _mla_prep(qf, kvf, kr, tables, seq, rows=256):
    t = qf.shape[0]
    w = B_HEADS * MLA_QK_PAD
    sb = seq // rows
    tab_spec = pl.BlockSpec((rows, 128), lambda i: (i % sb, 0))
    return pl.pallas_call(
        _mla_prep_kernel,
        grid=(t // rows,),
        in_specs=[pl.BlockSpec((rows, w), lambda i: (i, 0)),
                  pl.BlockSpec((rows, B_HEADS * MLA_NOPE), lambda i: (i, 0)),
                  pl.BlockSpec((rows, 128), lambda i: (i, 0)),
                  tab_spec, tab_spec, tab_spec],
        out_specs=[pl.BlockSpec((rows, w), lambda i: (i, 0)), pl.BlockSpec((rows, w), lambda i: (i, 0))],
        out_shape=[jax.ShapeDtypeStruct((t, w), MXU_DTYPE), jax.ShapeDtypeStruct((t, w), MXU_DTYPE)],
        compiler_params=_params(1),
        name="mla_prep",
    )(qf, kvf, kr, *tables)


def _dilated_kernel(q_ref, kc_ref, kp_ref, vc_ref, vp_ref, o_ref,
                    kbuf, vbuf, o0, o1, o2, l0, l1, l2):
    c = pl.program_id(2)
    kbuf[:DIL_CHUNK, :] = kp_ref[...]
    kbuf[DIL_CHUNK:, :] = kc_ref[...]
    vbuf[:DIL_CHUNK, :] = vp_ref[...]
    vbuf[DIL_CHUNK:, :] = vc_ref[...]
    scale = HEAD_DIM ** -0.5
    row = lax.broadcasted_iota(jnp.int32, (BLOCK, 2 * BLOCK), 0)
    col = lax.broadcasted_iota(jnp.int32, (BLOCK, 2 * BLOCK), 1)
    band = (col >= row) & (col <= row + DIL_STEPS)

    def blocks(specs):
        staged = []
        for q_start, k_start, dil, first, o_g, l_g in specs:
            if dil == 1:
                q_idx, k_idx = pl.ds(q_start, BLOCK), pl.ds(k_start, 2 * BLOCK)
            else:
                q_idx = pl.ds(q_start, BLOCK, stride=dil)
                k_idx = pl.ds(k_start, 2 * BLOCK, stride=dil)
            q = q_ref[q_idx, :].astype(MXU_DTYPE)
            k = kbuf[k_idx, :].astype(MXU_DTYPE)
            v = vbuf[k_idx, :].astype(MXU_DTYPE)
            logits = lax.dot_general(q, k, NT_DIMS, preferred_element_type=F32) * scale
            staged.append((logits, v, q_idx, first, o_g, l_g))
        weighted = []
        for logits, v, q_idx, first, o_g, l_g in staged:
            valid_from = jnp.where(first, BLOCK, 0)
            logits = jnp.where(band & (col >= valid_from), logits, NEG_INF)
            m = jnp.max(logits, axis=-1, keepdims=True)
            p = jnp.exp(logits - m)
            denom = jnp.sum(p, axis=-1, keepdims=True)
            weighted.append(((p / denom).astype(MXU_DTYPE), m + jnp.log(denom)))
        for (logits, v, q_idx, first, o_g, l_g), (pn, lse) in zip(staged, weighted):
            o_g[q_idx, :] = jnp.dot(pn, v, preferred_element_type=F32)
            l_g[q_idx, :] = jnp.broadcast_to(lse, (BLOCK, HEAD_DIM))

    n_blocks = DIL_CHUNK // BLOCK
    per_trip = 2

    def body(j, carry):
        specs = []
        for u in range(per_trip):
            i = j * per_trip + u
            specs.append((pl.multiple_of(i * BLOCK, BLOCK),
                          pl.multiple_of(DIL_CHUNK + (i - 1) * BLOCK, BLOCK),
                          DILATIONS[0], (c == 0) & (i == 0), o0, l0))
            d1 = DILATIONS[1]
            n, r = i // d1, i % d1
            span = BLOCK * d1
            specs.append((n * span + r, DIL_CHUNK + (n - 1) * span + r, d1, (c == 0) & (n == 0), o1, l1))
            specs.append((i, DIL_CHUNK - BLOCK * DILATIONS[2] + i, DILATIONS[2], c == 0, o2, l2))
        blocks(specs)
        return carry

    lax.fori_loop(0, n_blocks // per_trip, body, 0)

    la, lb, lc = l0[...], l1[...], l2[...]
    mx = jnp.maximum(jnp.maximum(la, lb), lc)
    ea, eb, ec = jnp.exp(la - mx), jnp.exp(lb - mx), jnp.exp(lc - mx)
    merged = (ea * o0[...] + eb * o1[...] + ec * o2[...]) / (ea + eb + ec)
    o_ref[...] = merged.astype(o_ref.dtype)


def _dilated_attention(qa, ka, h, batch, seq):
    t = qa.shape[0]
    nc = seq // DIL_CHUNK
    v_col0 = 2 * A_HEADS
    cur = lambda b, hd, c: (b * nc + c, hd)
    prev = lambda b, hd, c: (b * nc + jnp.maximum(c - 1, 0), hd)
    blk = (DIL_CHUNK, HEAD_DIM)
    return pl.pallas_call(
        _dilated_kernel,
        grid=(batch, A_HEADS, nc),
        in_specs=[pl.BlockSpec(blk, cur),
                  pl.BlockSpec(blk, cur), pl.BlockSpec(blk, prev),
                  pl.BlockSpec(blk, lambda b, hd, c: (b * nc + c, v_col0 + hd)),
                  pl.BlockSpec(blk, lambda b, hd, c: (b * nc + jnp.maximum(c - 1, 0), v_col0 + hd))],
        out_specs=pl.BlockSpec(blk, cur),
        out_shape=jax.ShapeDtypeStruct((t, A_HEADS * HEAD_DIM), MXU_DTYPE),
        scratch_shapes=[pltpu.VMEM((2 * DIL_CHUNK, HEAD_DIM), F32)] * 2
                       + [pltpu.VMEM(blk, F32)] * 6,
        compiler_params=_params(3),
        name="dilated_attention",
    )(qa, ka, ka, h, h)


def _mla_kernel(q_ref, k_ref, v_ref, o_ref, m_ref, l_ref, acc_ref, *, bq, bk, unroll):
    qi = pl.program_id(2)
    n_sub = bq // bk
    m_ref[...] = jnp.full_like(m_ref, NEG_INF)
    l_ref[...] = jnp.zeros_like(l_ref)
    acc_ref[...] = jnp.zeros_like(acc_ref)

    def group(k_starts, r0, diagonal):
        cols = bq - r0
        q = q_ref[r0:, :]
        ss = [lax.dot_general(k_ref[pl.ds(ks, bk), :], q, NT_DIMS, preferred_element_type=F32)
              for ks in k_starts]
        if diagonal:
            key = lax.broadcasted_iota(jnp.int32, (bk, cols), 0)
            qry = lax.broadcasted_iota(jnp.int32, (bk, cols), 1)
            ss = [jnp.where(key <= qry, s, NEG_INF) for s in ss]
        m_prev = m_ref[:, r0:]
        m_new = m_prev
        for s in ss:
            m_new = jnp.maximum(m_new, jnp.max(s, axis=0, keepdims=True))
        alpha = jnp.exp2(m_prev - m_new)
        l = alpha * l_ref[:, r0:]
        acc = alpha * acc_ref[:, r0:]
        for ks, s in zip(k_starts, ss):
            p = jnp.exp2(s - m_new)
            l = l + jnp.sum(p, axis=0, keepdims=True)
            acc = acc + lax.dot_general(v_ref[pl.ds(ks, bk), :], p.astype(MXU_DTYPE), TN_DIMS,
                                        preferred_element_type=F32)
        m_ref[:, r0:] = m_new
        l_ref[:, r0:] = l
        acc_ref[:, r0:] = acc

    def body(j, carry):
        group([pl.multiple_of((j * unroll + u) * bk, bk) for u in range(unroll)], 0, False)
        return carry

    assert n_sub % unroll == 0
    lax.fori_loop(0, qi * (n_sub // unroll), body, 0)
    for jj in range(n_sub):
        group([pl.multiple_of(qi * bq + jj * bk, bk)], jj * bk, True)
    o_ref[...] = (acc_ref[...] / l_ref[...]).T.astype(o_ref.dtype)


def _mla_attention(qm, km, kvf, batch, seq, bq=2048, bk=512, unroll=4):
    t = qm.shape[0]
    bq, bk = min(bq, seq), min(bk, seq)
    nq = seq // bq
    return pl.pallas_call(
        functools.partial(_mla_kernel, bq=bq, bk=bk, unroll=min(unroll, bq // bk)),
        grid=(batch, B_HEADS, nq),
        in_specs=[pl.BlockSpec((bq, MLA_QK_PAD), lambda b, hd, i: (b * nq + i, hd)),
                  pl.BlockSpec((seq, MLA_QK_PAD), lambda b, hd, i: (b, hd)),
                  pl.BlockSpec((seq, MLA_V), lambda b, hd, i: (b, B_HEADS + hd))],
        out_specs=pl.BlockSpec((bq, MLA_V), lambda b, hd, i: (b * nq + i, hd)),
        out_shape=jax.ShapeDtypeStruct((t, B_HEADS * MLA_V), MXU_DTYPE),
        scratch_shapes=[pltpu.VMEM((1, bq), F32), pltpu.VMEM((1, bq), F32),
                        pltpu.VMEM((MLA_V, bq), F32)],
        compiler_params=_params(3),
        name="mla_attention",
    )(qm, km, kvf)


def _stick_kernel(q_ref, k_ref, v_ref, o_ref, acc_ref, run_ref, *, bq, bk, unroll):
    qi = pl.program_id(2)
    n_sub = bq // bk
    tri_r = lax.broadcasted_iota(jnp.int32, (bk, bk), 0)
    tri_c = lax.broadcasted_iota(jnp.int32, (bk, bk), 1)
    at_or_after = jnp.where(tri_r >= tri_c, 1.0, 0.0).astype(MXU_DTYPE)
    acc_ref[...] = jnp.zeros_like(acc_ref)
    run_ref[...] = jnp.zeros_like(run_ref)

    def group(tiles):
        staged = []
        for ks, r0, diagonal in tiles:
            rows = bq - r0
            z = lax.dot_general(q_ref[r0:, :], k_ref[pl.ds(ks, bk), :], NT_DIMS,
                                preferred_element_type=F32)
            neg_abs = lax.bitcast_convert_type(
                lax.bitcast_convert_type(z, jnp.uint32) | jnp.uint32(0x80000000), F32)
            sp = jnp.maximum(z, 0.0) + jnp.log(1.0 + jnp.exp2(neg_abs)) * LOG2_E
            mask = None
            if diagonal:
                row = lax.broadcasted_iota(jnp.int32, (rows, bk), 0)
                col = lax.broadcasted_iota(jnp.int32, (rows, bk), 1)
                mask = col < row
                sp = jnp.where(mask, sp, 0.0)
            staged.append((z, sp, mask))
        sufs = []
        for _, sp, _ in staged:
            hi = sp.astype(MXU_DTYPE)
            lo = (sp - hi.astype(F32)).astype(MXU_DTYPE)
            sufs.append(jnp.dot(hi, at_or_after, preferred_element_type=F32)
                        + jnp.dot(lo, at_or_after, preferred_element_type=F32))
        run = run_ref[...]
        acc = acc_ref[...]
        for (ks, r0, _), (z, _, mask), suf in zip(tiles, staged, sufs):
            att = jnp.exp2(z - suf - run[r0:])
            if mask is not None:
                att = jnp.where(mask, att, 0.0)
            part = jnp.dot(att.astype(MXU_DTYPE), v_ref[pl.ds(ks, bk), :], preferred_element_type=F32)
            total = suf[:, :1]
            if r0 == 0:
                acc, run = acc + part, run + total
            else:
                acc = jnp.concatenate([acc[:r0], acc[r0:] + part], axis=0)
                run = jnp.concatenate([run[:r0], run[r0:] + total], axis=0)
        acc_ref[...] = acc
        run_ref[...] = run

    group([(pl.multiple_of(qi * bq + jj * bk, bk), jj * bk, True) for jj in reversed(range(n_sub))])

    def body(j, carry):
        group([(pl.multiple_of((qi * n_sub - 1 - j * unroll - u) * bk, bk), 0, False)
               for u in range(unroll)])
        return carry

    assert n_sub % unroll == 0
    lax.fori_loop(0, qi * (n_sub // unroll), body, 0)
    o_ref[...] = acc_ref[...].astype(o_ref.dtype)


def _stick_attention(qkv, batch, seq, bq=1024, bk=256, unroll=4):
    t = qkv.shape[0]
    bq, bk = min(bq, seq), min(bk, seq)
    nq = seq // bq
    return pl.pallas_call(
        functools.partial(_stick_kernel, bq=bq, bk=bk, unroll=min(unroll, bq // bk)),
        grid=(batch, C_HEADS, nq),
        in_specs=[pl.BlockSpec((bq, HEAD_DIM), lambda b, hd, i: (b * nq + i, hd)),
                  pl.BlockSpec((seq, HEAD_DIM), lambda b, hd, i: (b, C_HEADS + hd)),
                  pl.BlockSpec((seq, HEAD_DIM), lambda b, hd, i: (b, 2 * C_HEADS + hd))],
        out_specs=pl.BlockSpec((bq, HEAD_DIM), lambda b, hd, i: (b * nq + i, hd)),
        out_shape=jax.ShapeDtypeStruct((t, C_HEADS * HEAD_DIM), MXU_DTYPE),
        scratch_shapes=[pltpu.VMEM((bq, HEAD_DIM), F32), pltpu.VMEM((bq, 1), F32)],
        compiler_params=_params(3),
        name="stick_breaking_attention",
    )(qkv, qkv, qkv)


def _top_values(s, count, with_rank=False):
    vals = []
    cur = s
    rank = jnp.full(s.shape, float(count), F32) if with_rank else None
    for k in range(count):
        m = jnp.max(cur, axis=0, keepdims=True)
        vals.append(m)
        hit = cur == m
        if with_rank:
            rank = jnp.where(hit, float(k), rank)
        cur = jnp.where(hit, NEG_INF, cur)
    return (vals, rank) if with_rank else vals


def _ranked(s, count):
    n = s.shape[0]
    rows = lax.broadcasted_iota(jnp.int32, s.shape, 0)
    vals = []
    cur = s
    rank = jnp.full(s.shape, float(count), F32)
    for k in range(count):
        m = jnp.max(cur, axis=0, keepdims=True)
        first = jnp.min(jnp.where(cur == m, rows, n), axis=0, keepdims=True)
        pick = rows == first
        vals.append(m)
        rank = jnp.where(pick, float(k), rank)
        cur = jnp.where(pick, NEG_INF, cur)
    return vals, rank


def _pair_sums(v1, v2):
    v1_all = jnp.concatenate(v1, axis=0)
    v2_all = jnp.concatenate(v2, axis=0)
    return jnp.concatenate([v1[0] + v2_all]
                           + [v1[p] + v2_all[:8] for p in range(1, 8)]
                           + [v2[0] + v1_all[8:]], axis=0)


def _count_at_least(s, threshold):
    return jnp.sum(jnp.where(s >= threshold, 1.0, 0.0), axis=0, keepdims=True)


def _route_distinct(s1, s2):
    v1 = _top_values(s1, PEER_TOPK)
    v2, rank2 = _top_values(s2, PEER_TOPK, with_rank=True)
    cand = _pair_sums(v1, v2)
    tops = _top_values(cand, PEER_TOPK + 1)
    cut = 0.5 * (tops[PEER_TOPK - 1] + tops[PEER_TOPK])
    top = tops[0]
    z = jnp.sum(jnp.where(cand >= cut, jnp.exp(cand - top), 0.0), axis=0, keepdims=True)
    need = cut - s1
    count = jnp.zeros_like(s1)
    for q in range(PEER_TOPK):
        count = count + jnp.where(v2[q] >= need, 1.0, 0.0)
    a = jnp.where(s1 >= v1[PEER_TOPK - 1], jnp.exp(s1 - v1[0]) / (2.0 * z), 0.0)
    b = jnp.where(s2 >= v2[PEER_TOPK - 1], jnp.exp(s2 - v2[0]), 0.0)
    distinct = ((_count_at_least(s1, v1[PEER_TOPK - 1]) == PEER_TOPK)
                & (_count_at_least(s2, v2[PEER_TOPK - 1]) == PEER_TOPK)
                & (_count_at_least(cand, tops[PEER_TOPK]) == PEER_TOPK + 1))
    return rank2, b, count, a, distinct


def _route_with_ties(s1, s2):
    v1, rank1 = _ranked(s1, PEER_TOPK)
    v2, rank2 = _ranked(s2, PEER_TOPK)
    cand = _pair_sums(v1, v2)
    _, cand_rank = _ranked(cand, PEER_TOPK)
    chosen = cand_rank < PEER_TOPK
    top = v1[0] + v2[0]
    z = jnp.sum(jnp.where(chosen, jnp.exp(cand - top), 0.0), axis=0, keepdims=True)
    picked = jnp.where(chosen, 1.0, 0.0)
    per_p = [jnp.sum(picked[:16], axis=0, keepdims=True)]
    per_p += [jnp.sum(picked[8 + 8 * p:16 + 8 * p], axis=0, keepdims=True) for p in range(1, 8)]
    per_p += [picked[64 + p:65 + p] for p in range(8, 16)]
    count = jnp.zeros_like(s1)
    for p in range(PEER_TOPK):
        count = count + jnp.where(rank1 == p, per_p[p], 0.0)
    a = jnp.where(rank1 < PEER_TOPK, jnp.exp(s1 - v1[0]) / (2.0 * z), 0.0)
    b = jnp.where(rank2 < PEER_TOPK, jnp.exp(s2 - v2[0]), 0.0)
    return rank2, b, count, a


def _peer_route_kernel(q_ref, keys_ref, r2_ref, b_ref, c_ref, a_ref):
    def head(hd, carry):
        col = pl.multiple_of(hd * 2 * PEER_HALF, 2 * PEER_HALF)
        q1 = q_ref[:, pl.ds(col, PEER_HALF)].astype(MXU_DTYPE)
        q2 = q_ref[:, pl.ds(col + PEER_HALF, PEER_HALF)].astype(MXU_DTYPE)
        s1 = lax.dot_general(keys_ref[hd, 0], q1, NT_DIMS, preferred_element_type=F32)
        s2 = lax.dot_general(keys_ref[hd, 1], q2, NT_DIMS, preferred_element_type=F32)

        def store(rank2, b, count, a):
            groups = pl.ds(pl.multiple_of(hd * KEY_GROUPS, KEY_GROUPS), KEY_GROUPS)
            tokens = rank2.shape[1]
            r2_ref[groups] = rank2.astype(r2_ref.dtype).reshape(KEY_GROUPS, 16, tokens)
            b_ref[groups] = b.astype(b_ref.dtype).reshape(KEY_GROUPS, 16, tokens)
            c_ref[hd] = count
            a_ref[hd] = a

        rank2, b, count, a, distinct = _route_distinct(s1, s2)
        store(rank2, b, count, a)

        @pl.when(jnp.max(jnp.where(distinct, 0.0, 1.0)) > 0.0)
        def _():
            store(*_route_with_ties(s1, s2))

        return carry

    lax.fori_loop(0, PEER_HEADS, head, 0)


def _peer_route(qp, keys, tr=256):
    t = qp.shape[0]
    shape = (PEER_HEADS, PEER_N_KEYS, t)
    spec = pl.BlockSpec((PEER_HEADS, PEER_N_KEYS, tr), lambda i: (0, 0, i))
    flat = (PEER_HEADS * KEY_GROUPS, 16, t)
    flat_spec = pl.BlockSpec((PEER_HEADS * KEY_GROUPS, 16, tr), lambda i: (0, 0, i))
    return pl.pallas_call(
        _peer_route_kernel,
        grid=(t // tr,),
        in_specs=[pl.BlockSpec((tr, qp.shape[1]), lambda i: (i, 0)),
                  pl.BlockSpec(keys.shape, lambda i: (0, 0, 0, 0))],
        out_specs=[flat_spec, flat_spec, spec, spec],
        out_shape=[jax.ShapeDtypeStruct(flat, MXU_DTYPE), jax.ShapeDtypeStruct(flat, MXU_DTYPE),
                   jax.ShapeDtypeStruct(shape, F32), jax.ShapeDtypeStruct(shape, F32)],
        compiler_params=_params(1),
        name="peer_route",
    )(qp, keys)


def _twice_gelu(x):
    return x * (1.0 + lax.erf(x * (2.0 ** -0.5)))


def _row_tile(row, rows):
    packed = jnp.broadcast_to(row, (16, 128)).astype(MXU_DTYPE)
    return jnp.concatenate([packed] * (rows // 16), axis=0)


def _peer_kernel(xt_ref, res_ref, u_ref, v_ref, r2_ref, b_ref, c_ref, a_ref, g_ref, bias_ref,
                 o_ref, ob_ref, st_ref, wt_ref, *, n_i, tb, alpha):
    e = pl.program_id(1)

    @pl.when(e == 0)
    def _():
        o_ref[...] = jnp.zeros_like(o_ref)

    st_ref[...] = jnp.dot(u_ref[...], xt_ref[...], preferred_element_type=F32)
    zero = jnp.zeros((PEER_N_KEYS, 128), MXU_DTYPE)
    assert 8 % n_i == 0
    per_group = 8 // n_i
    within = e % per_group

    def key_rows(ref, hd, lanes):
        grp = ref[hd, 0, :, lanes]
        rows = grp[:n_i]
        for k in range(1, per_group):
            rows = jnp.where(within == k, grp[k * n_i:(k + 1) * n_i], rows)
        return rows

    live = min(n_i, 4)
    for l in range(tb // 128):
        lanes = slice(l * 128, (l + 1) * 128)
        for i0 in range(0, n_i, live):
            gates = [zero for _ in range(live)]
            for hd in range(PEER_HEADS):
                r2 = jnp.concatenate([r2_ref[hd * KEY_GROUPS + g, :, lanes] for g in range(KEY_GROUPS)],
                                     axis=0)
                b = jnp.concatenate([b_ref[hd * KEY_GROUPS + g, :, lanes] for g in range(KEY_GROUPS)],
                                    axis=0)
                counts = key_rows(c_ref, hd, lanes)
                a_rows = key_rows(a_ref, hd, lanes)
                for k in range(live):
                    count = _row_tile(counts[i0 + k:i0 + k + 1], PEER_N_KEYS)
                    a = _row_tile(a_rows[i0 + k:i0 + k + 1], PEER_N_KEYS)
                    gates[k] = gates[k] + jnp.where(r2 < count, b, zero) * a
            for k in range(live):
                rows = slice((i0 + k) * PEER_N_KEYS, (i0 + k + 1) * PEER_N_KEYS)
                act = _twice_gelu(st_ref[rows, lanes]).astype(MXU_DTYPE)
                wt_ref[rows, lanes] = act * gates[k]
    o_ref[...] += lax.dot_general(wt_ref[...], v_ref[...], TN_DIMS, preferred_element_type=F32)

    @pl.when(e == pl.num_programs(1) - 1)
    def _():
        y = _layer_norm_rows(alpha * res_ref[...] + o_ref[...], g_ref[...], bias_ref[...])
        o_ref[...] = y
        ob_ref[...] = y.astype(ob_ref.dtype)


def _peer_experts(xt, res, u, v, layer, r2, b, c, a, gain, bias, alpha, tb=512, n_i=8):
    t, d = res.shape
    n_exp = u.shape[1]
    eb = n_i * PEER_N_KEYS
    tb = min(tb, t)
    tok = lambda i, e: (i, 0)
    per_group = 8 // n_i
    c = c.reshape(PEER_HEADS, PEER_N_KEYS // 8, 8, t)
    a = a.reshape(PEER_HEADS, PEER_N_KEYS // 8, 8, t)
    full = pl.BlockSpec((PEER_HEADS, 1, 8, tb), lambda i, e: (0, e // per_group, 0, i))
    flat = pl.BlockSpec((PEER_HEADS * KEY_GROUPS, 16, tb), lambda i, e: (0, 0, i))
    return pl.pallas_call(
        functools.partial(_peer_kernel, n_i=n_i, tb=tb, alpha=alpha),
        grid=(t // tb, n_exp // eb),
        in_specs=[pl.BlockSpec((d, tb), lambda i, e: (0, i)), pl.BlockSpec((tb, d), tok),
                  pl.BlockSpec((None, eb, d), lambda i, e: (layer, e, 0)),
                  pl.BlockSpec((None, eb, d), lambda i, e: (layer, e, 0)),
                  flat, flat, full, full,
                  pl.BlockSpec((1, d), lambda i, e: (0, 0)), pl.BlockSpec((1, d), lambda i, e: (0, 0))],
        out_specs=[pl.BlockSpec((tb, d), tok), pl.BlockSpec((tb, d), tok)],
        out_shape=[jax.ShapeDtypeStruct((t, d), F32), jax.ShapeDtypeStruct((t, d), MXU_DTYPE)],
        scratch_shapes=[pltpu.VMEM((eb, tb), F32), pltpu.VMEM((eb, tb), MXU_DTYPE)],
        compiler_params=_params(2),
        name="peer_experts",
    )(xt, res, u, v, r2, b, c, a, gain.reshape(1, d), bias.reshape(1, d))


def _peer_layer(x, xb, w_query, keys, u, v, layer, gain, bias, alpha):
    qp = _matmul(xb, w_query, F32, bn=w_query.shape[1])
    r2, b, c, a = _peer_route(qp, keys)
    return _peer_experts(xb.T, x, u, v, layer, r2, b, c, a, gain, bias, alpha)


def _even_mixer(x2d, batch, seq, w_in, q_norm, w_q_b, kv_norm, w_kv_b, tables):
    d = x2d.shape[1]
    w_in_p = jnp.concatenate(
        [w_in, jnp.zeros((d, 4096 - w_in.shape[1]), w_in.dtype)], axis=1).astype(MXU_DTYPE)
    h = _matmul(x2d, w_in_p, F32)
    cos_a, sin_a, cos_b, sin_lo, sin_hi = tables
    qa, ka, cqn, ckvn, kr = _even_prep(h, tables, q_norm, kv_norm, seq)
    out_a = _dilated_attention(qa, ka, h, batch, seq)
    wq = w_q_b.reshape(MLA_Q_LORA, B_HEADS, MLA_NOPE + MLA_ROPE)
    wq = jnp.pad(wq, ((0, 0), (0, 0), (0, MLA_QK_PAD - MLA_NOPE - MLA_ROPE)))
    wq = wq.reshape(MLA_Q_LORA, B_HEADS * MLA_QK_PAD).astype(MXU_DTYPE)
    wkv = w_kv_b.reshape(MLA_KV_LORA, B_HEADS, MLA_NOPE + MLA_V)
    wkv = jnp.concatenate([wkv[:, :, :MLA_NOPE].reshape(MLA_KV_LORA, -1),
                           wkv[:, :, MLA_NOPE:].reshape(MLA_KV_LORA, -1)], axis=1).astype(MXU_DTYPE)
    qf = _matmul(cqn, wq, F32)
    kvf = _matmul(ckvn, wkv, MXU_DTYPE)
    qm, km = _mla_prep(qf, kvf, kr, (cos_b, sin_lo, sin_hi), seq)
    out_b = _mla_attention(qm, km, kvf, batch, seq)
    return out_a, out_b


def kernel(x, a_w_in, b_q_norm, b_w_q_b, b_kv_norm, b_w_kv_b, ab_w_out, c_w_in, c_w_out,
           peer_w_query, peer_sub_keys, peer_u, peer_v, ln_gain, ln_bias):
    batch, seq, d = x.shape
    depth = peer_u.shape[0]
    alpha = (2 * depth) ** 0.25
    tables = _rope_tables(seq)
    u_all, v_all = peer_u.astype(MXU_DTYPE), peer_v.astype(MXU_DTYPE)
    xf = x.reshape(batch * seq, d)
    xb = None
    for layer in range(depth):
        i = layer // 2
        if layer % 2 == 0:
            src = xf if xb is None else xb
            out_a, out_b = _even_mixer(src, batch, seq, a_w_in[i], b_q_norm[i], b_w_q_b[i],
                                       b_kv_norm[i], b_w_kv_b[i], tables)
            w_out = ab_w_out[i].astype(MXU_DTYPE)
            half = A_HEADS * HEAD_DIM
            xf, xb = _matmul_residual_ln([out_a, out_b], [w_out[:half], w_out[half:]], xf,
                                         ln_gain[layer, 0], ln_bias[layer, 0], alpha)
        else:
            src = xf if xb is None else xb
            n_q = C_HEADS * HEAD_DIM
            w_qkv = jnp.concatenate([c_w_in[i][:, :n_q] * (HEAD_DIM ** -0.5 * LOG2_E),
                                     c_w_in[i][:, n_q:]], axis=1).astype(MXU_DTYPE)
            qkv = _matmul(src, w_qkv, MXU_DTYPE)
            o = _stick_attention(qkv, batch, seq)
            xf, xb = _matmul_residual_ln([o], [c_w_out[i].astype(MXU_DTYPE)], xf,
                                         ln_gain[layer, 0], ln_bias[layer, 0], alpha)
        xf, xb = _peer_layer(xf, xb, peer_w_query[layer].astype(MXU_DTYPE),
                         peer_sub_keys[layer].astype(MXU_DTYPE),
                         u_all, v_all, layer,
                         ln_gain[layer, 1], ln_bias[layer, 1], alpha)
    return xf.reshape(batch, seq, d)
```

```python
import functools

import jax
import jax.numpy as jnp
from jax import lax
from jax.experimental import pallas as pl
from jax.experimental.pallas import tpu as pltpu

F32 = jnp.float32
MXU_DTYPE = jnp.bfloat16

HEAD_DIM = 128
BLOCK = 128
DIL_STEPS = 128
DILATIONS = (1, 4, 16)
DIL_CHUNK = BLOCK * DILATIONS[-1]
A_HEADS = 8
B_HEADS = 8
MLA_Q_LORA = 512
MLA_KV_LORA = 256
MLA_NOPE = 128
MLA_ROPE = 64
MLA_V = 128
MLA_QK_PAD = 256
C_HEADS = 16
PEER_HEADS = 8
PEER_N_KEYS = 128
PEER_TOPK = 16
PEER_HALF = 128
KEY_GROUPS = PEER_N_KEYS // 16
ROPE_THETA = 10000.0
LN_EPS = 1e-5
RMS_EPS = 1e-6
NEG_INF = -1e30
LOG2_E = 1.4426950408889634
MLA_Q_SCALE = (MLA_NOPE + MLA_ROPE) ** -0.5 * LOG2_E

V7X_VMEM_LIMIT_BYTES = 56 * 1024 * 1024
NT_DIMS = (((1,), (1,)), ((), ()))
TN_DIMS = (((0,), (0,)), ((), ()))


def _params(n_axes):
    return pltpu.CompilerParams(dimension_semantics=("arbitrary",) * n_axes,
                                vmem_limit_bytes=V7X_VMEM_LIMIT_BYTES)


def _mm_kernel(a_ref, b_ref, o_ref):
    a = a_ref[...].astype(MXU_DTYPE)
    o_ref[...] = jnp.dot(a, b_ref[...], preferred_element_type=F32).astype(o_ref.dtype)


def _matmul(a, b, out_dtype, bm=1024, bn=1024):
    m, k = a.shape
    n = b.shape[1]
    bm, bn = min(bm, m), min(bn, n)
    assert m % bm == 0 and n % bn == 0
    return pl.pallas_call(
        _mm_kernel,
        grid=(m // bm, n // bn),
        in_specs=[pl.BlockSpec((bm, k), lambda i, j: (i, 0)),
                  pl.BlockSpec((k, bn), lambda i, j: (0, j))],
        out_specs=pl.BlockSpec((bm, bn), lambda i, j: (i, j)),
        out_shape=jax.ShapeDtypeStruct((m, n), out_dtype),
        compiler_params=_params(2),
        name="matmul",
    )(a, b)


def _layer_norm_rows(y, g, b):
    mu = jnp.mean(y, axis=-1, keepdims=True)
    d = y - mu
    var = jnp.mean(d * d, axis=-1, keepdims=True)
    return d * lax.rsqrt(var + LN_EPS) * g + b


def _mm_ln_kernel(*refs, n_pairs, alpha):
    a_refs = refs[:n_pairs]
    w_refs = refs[n_pairs:2 * n_pairs]
    res_ref, g_ref, b_ref, o_ref, ob_ref = refs[2 * n_pairs:]
    acc = jnp.dot(a_refs[0][...].astype(MXU_DTYPE), w_refs[0][...], preferred_element_type=F32)
    for a_ref, w_ref in zip(a_refs[1:], w_refs[1:]):
        acc = acc + jnp.dot(a_ref[...].astype(MXU_DTYPE), w_ref[...], preferred_element_type=F32)
    y = _layer_norm_rows(alpha * res_ref[...] + acc, g_ref[...], b_ref[...])
    o_ref[...] = y
    ob_ref[...] = y.astype(ob_ref.dtype)


def _matmul_residual_ln(a_list, w_list, res, gain, bias, alpha, bm=256):
    m, n = res.shape
    n_pairs = len(a_list)
    in_specs = [pl.BlockSpec((bm, a.shape[1]), lambda i: (i, 0)) for a in a_list]
    in_specs += [pl.BlockSpec(w.shape, lambda i: (0, 0)) for w in w_list]
    in_specs += [pl.BlockSpec((bm, n), lambda i: (i, 0)),
                 pl.BlockSpec((1, n), lambda i: (0, 0)),
                 pl.BlockSpec((1, n), lambda i: (0, 0))]
    return pl.pallas_call(
        functools.partial(_mm_ln_kernel, n_pairs=n_pairs, alpha=alpha),
        grid=(m // bm,),
        in_specs=in_specs,
        out_specs=[pl.BlockSpec((bm, n), lambda i: (i, 0)), pl.BlockSpec((bm, n), lambda i: (i, 0))],
        out_shape=[jax.ShapeDtypeStruct((m, n), F32), jax.ShapeDtypeStruct((m, n), MXU_DTYPE)],
        compiler_params=_params(1),
        name="matmul_residual_ln",
    )(*a_list, *w_list, res, gain.reshape(1, n), bias.reshape(1, n))


def _rope_tables(seq):
    pos = jnp.arange(seq, dtype=F32)[:, None]

    def cos_sin(half):
        inv = ROPE_THETA ** (-jnp.arange(half, dtype=F32) / half)
        ang = pos * inv[None, :]
        return jnp.cos(ang), jnp.sin(ang)

    c, s = cos_sin(HEAD_DIM // 2)
    cos_a = jnp.concatenate([c, c], axis=1)
    sin_a = jnp.concatenate([-s, s], axis=1)
    c, s = cos_sin(MLA_ROPE // 2)
    z32 = jnp.zeros_like(c)
    cos_b = jnp.concatenate([c, c, z32, z32], axis=1)
    sin_lo = jnp.concatenate([-s, z32, z32, z32], axis=1)
    sin_hi = jnp.concatenate([z32, s, z32, z32], axis=1)
    return cos_a, sin_a, cos_b, sin_lo, sin_hi


def _rope128(x, cos_a, sin_a):
    return x * cos_a + pltpu.roll(x, HEAD_DIM // 2, 1) * sin_a


def _rope64(x, cos_b, sin_lo, sin_hi):
    return x * cos_b + pltpu.roll(x, 96, 1) * sin_lo + pltpu.roll(x, 32, 1) * sin_hi


def _rms_norm_rows(x, g):
    return x * lax.rsqrt(jnp.mean(x * x, axis=-1, keepdims=True) + RMS_EPS) * g


def _even_prep_kernel(qk_ref, tail_ref, cos_a_ref, sin_a_ref, cos_b_ref, sin_lo_ref, sin_hi_ref,
                      qn_ref, kvn_ref, qa_ref, ka_ref, cq_ref, ckv_ref, kr_ref):
    cos_a, sin_a = cos_a_ref[...], sin_a_ref[...]
    n_qk = A_HEADS * HEAD_DIM
    for hd in range(A_HEADS):
        lo = hd * HEAD_DIM
        qa_ref[:, lo:lo + HEAD_DIM] = _rope128(qk_ref[:, lo:lo + HEAD_DIM], cos_a, sin_a)
        ka_ref[:, lo:lo + HEAD_DIM] = _rope128(qk_ref[:, n_qk + lo:n_qk + lo + HEAD_DIM], cos_a, sin_a)
    cq_ref[...] = _rms_norm_rows(tail_ref[:, :MLA_Q_LORA], qn_ref[...]).astype(cq_ref.dtype)
    o2 = MLA_Q_LORA + MLA_KV_LORA
    ckv_ref[...] = _rms_norm_rows(tail_ref[:, MLA_Q_LORA:o2], kvn_ref[...]).astype(ckv_ref.dtype)
    kr = _rope64(tail_ref[:, o2:o2 + 128], cos_b_ref[...], sin_lo_ref[...], sin_hi_ref[...])
    kr_ref[...] = kr.astype(kr_ref.dtype)


def _even_prep(h, tables, q_norm, kv_norm, seq, rows=256):
    t = h.shape[0]
    n_qk = A_HEADS * HEAD_DIM
    sb = seq // rows
    tab_spec = pl.BlockSpec((rows, 128), lambda i: (i % sb, 0))
    return pl.pallas_call(
        _even_prep_kernel,
        grid=(t // rows,),
        in_specs=[pl.BlockSpec((rows, 2 * n_qk), lambda i: (i, 0)),
                  pl.BlockSpec((rows, 1024), lambda i: (i, 3)),
                  tab_spec, tab_spec, tab_spec, tab_spec, tab_spec,
                  pl.BlockSpec((1, MLA_Q_LORA), lambda i: (0, 0)),
                  pl.BlockSpec((1, MLA_KV_LORA), lambda i: (0, 0))],
        out_specs=[pl.BlockSpec((rows, n_qk), lambda i: (i, 0)),
                   pl.BlockSpec((rows, n_qk), lambda i: (i, 0)),
                   pl.BlockSpec((rows, MLA_Q_LORA), lambda i: (i, 0)),
                   pl.BlockSpec((rows, MLA_KV_LORA), lambda i: (i, 0)),
                   pl.BlockSpec((rows, 128), lambda i: (i, 0))],
        out_shape=[jax.ShapeDtypeStruct((t, n_qk), F32),
                   jax.ShapeDtypeStruct((t, n_qk), F32),
                   jax.ShapeDtypeStruct((t, MLA_Q_LORA), MXU_DTYPE),
                   jax.ShapeDtypeStruct((t, MLA_KV_LORA), MXU_DTYPE),
                   jax.ShapeDtypeStruct((t, 128), MXU_DTYPE)],
        compiler_params=_params(1),
        name="even_prep",
    )(h, h, *tables, q_norm.reshape(1, -1), kv_norm.reshape(1, -1))


def _mla_prep_kernel(q_ref, kn_ref, kr_ref, cos_b_ref, sin_lo_ref, sin_hi_ref, qm_ref, km_ref):
    cos_b, sin_lo, sin_hi = cos_b_ref[...], sin_lo_ref[...], sin_hi_ref[...]
    kr = kr_ref[...]
    for hd in range(B_HEADS):
        lo = hd * MLA_QK_PAD
        qm_ref[:, lo:lo + MLA_NOPE] = (q_ref[:, lo:lo + MLA_NOPE] * MLA_Q_SCALE).astype(qm_ref.dtype)
        q_rope = _rope64(q_ref[:, lo + MLA_NOPE:lo + MLA_QK_PAD], cos_b, sin_lo, sin_hi)
        qm_ref[:, lo + MLA_NOPE:lo + MLA_QK_PAD] = (q_rope * MLA_Q_SCALE).astype(qm_ref.dtype)
        km_ref[:, lo:lo + MLA_NOPE] = kn_ref[:, hd * MLA_NOPE:(hd + 1) * MLA_NOPE]
        km_ref[:, lo + MLA_NOPE:lo + MLA_QK_PAD] = kr


def _mla_prep(qf, kvf, kr, tables, seq, rows=256):
    t = qf.shape[0]
    w = B_HEADS * MLA_QK_PAD
    sb = seq // rows
    tab_spec = pl.BlockSpec((rows, 128), lambda i: (i % sb, 0))
    return pl.pallas_call(
        _mla_prep_kernel,
        grid=(t // rows,),
        in_specs=[pl.BlockSpec((rows, w), lambda i: (i, 0)),
                  pl.BlockSpec((rows, B_HEADS * MLA_NOPE), lambda i: (i, 0)),
                  pl.BlockSpec((rows, 128), lambda i: (i, 0)),
                  tab_spec, tab_spec, tab_spec],
        out_specs=[pl.BlockSpec((rows, w), lambda i: (i, 0)), pl.BlockSpec((rows, w), lambda i: (i, 0))],
        out_shape=[jax.ShapeDtypeStruct((t, w), MXU_DTYPE), jax.ShapeDtypeStruct((t, w), MXU_DTYPE)],
        compiler_params=_params(1),
        name="mla_prep",
    )(qf, kvf, kr, *tables)


def _dilated_kernel(q_ref, kc_ref, kp_ref, vc_ref, vp_ref, o_ref,
                    kbuf, vbuf, o0, o1, o2, l0, l1, l2):
    c = pl.program_id(2)
    kbuf[:DIL_CHUNK, :] = kp_ref[...]
    kbuf[DIL_CHUNK:, :] = kc_ref[...]
    vbuf[:DIL_CHUNK, :] = vp_ref[...]
    vbuf[DIL_CHUNK:, :] = vc_ref[...]
    scale = HEAD_DIM ** -0.5
    row = lax.broadcasted_iota(jnp.int32, (BLOCK, 2 * BLOCK), 0)
    col = lax.broadcasted_iota(jnp.int32, (BLOCK, 2 * BLOCK), 1)
    band = (col >= row) & (col <= row + DIL_STEPS)

    def blocks(specs):
        staged = []
        for q_start, k_start, dil, first, o_g, l_g in specs:
            if dil == 1:
                q_idx, k_idx = pl.ds(q_start, BLOCK), pl.ds(k_start, 2 * BLOCK)
            else:
                q_idx = pl.ds(q_start, BLOCK, stride=dil)
                k_idx = pl.ds(k_start, 2 * BLOCK, stride=dil)
            q = q_ref[q_idx, :].astype(MXU_DTYPE)
            k = kbuf[k_idx, :].astype(MXU_DTYPE)
            v = vbuf[k_idx, :].astype(MXU_DTYPE)
            logits = lax.dot_general(q, k, NT_DIMS, preferred_element_type=F32) * scale
            staged.append((logits, v, q_idx, first, o_g, l_g))
        weighted = []
        for logits, v, q_idx, first, o_g, l_g in staged:
            valid_from = jnp.where(first, BLOCK, 0)
            logits = jnp.where(band & (col >= valid_from), logits, NEG_INF)
            m = jnp.max(logits, axis=-1, keepdims=True)
            p = jnp.exp(logits - m)
            denom = jnp.sum(p, axis=-1, keepdims=True)
            weighted.append(((p / denom).astype(MXU_DTYPE), m + jnp.log(denom)))
        for (logits, v, q_idx, first, o_g, l_g), (pn, lse) in zip(staged, weighted):
            o_g[q_idx, :] = jnp.dot(pn, v, preferred_element_type=F32)
            l_g[q_idx, :] = jnp.broadcast_to(lse, (BLOCK, HEAD_DIM))

    n_blocks = DIL_CHUNK // BLOCK
    per_trip = 2

    def body(j, carry):
        specs = []
        for u in range(per_trip):
            i = j * per_trip + u
            specs.append((pl.multiple_of(i * BLOCK, BLOCK),
                          pl.multiple_of(DIL_CHUNK + (i - 1) * BLOCK, BLOCK),
                          DILATIONS[0], (c == 0) & (i == 0), o0, l0))
            d1 = DILATIONS[1]
            n, r = i // d1, i % d1
            span = BLOCK * d1
            specs.append((n * span + r, DIL_CHUNK + (n - 1) * span + r, d1, (c == 0) & (n == 0), o1, l1))
            specs.append((i, DIL_CHUNK - BLOCK * DILATIONS[2] + i, DILATIONS[2], c == 0, o2, l2))
        blocks(specs)
        return carry

    lax.fori_loop(0, n_blocks // per_trip, body, 0)

    la, lb, lc = l0[...], l1[...], l2[...]
    mx = jnp.maximum(jnp.maximum(la, lb), lc)
    ea, eb, ec = jnp.exp(la - mx), jnp.exp(lb - mx), jnp.exp(lc - mx)
    merged = (ea * o0[...] + eb * o1[...] + ec * o2[...]) / (ea + eb + ec)
    o_ref[...] = merged.astype(o_ref.dtype)


def _dilated_attention(qa, ka, h, batch, seq):
    t = qa.shape[0]
    nc = seq // DIL_CHUNK
    v_col0 = 2 * A_HEADS
    cur = lambda b, hd, c: (b * nc + c, hd)
    prev = lambda b, hd, c: (b * nc + jnp.maximum(c - 1, 0), hd)
    blk = (DIL_CHUNK, HEAD_DIM)
    return pl.pallas_call(
        _dilated_kernel,
        grid=(batch, A_HEADS, nc),
        in_specs=[pl.BlockSpec(blk, cur),
                  pl.BlockSpec(blk, cur), pl.BlockSpec(blk, prev),
                  pl.BlockSpec(blk, lambda b, hd, c: (b * nc + c, v_col0 + hd)),
                  pl.BlockSpec(blk, lambda b, hd, c: (b * nc + jnp.maximum(c - 1, 0), v_col0 + hd))],
        out_specs=pl.BlockSpec(blk, cur),
        out_shape=jax.ShapeDtypeStruct((t, A_HEADS * HEAD_DIM), MXU_DTYPE),
        scratch_shapes=[pltpu.VMEM((2 * DIL_CHUNK, HEAD_DIM), F32)] * 2
                       + [pltpu.VMEM(blk, F32)] * 6,
        compiler_params=_params(3),
        name="dilated_attention",
    )(qa, ka, ka, h, h)


def _mla_kernel(q_ref, k_ref, v_ref, o_ref, m_ref, l_ref, acc_ref, *, bq, bk, unroll):
    qi = pl.program_id(2)
    n_sub = bq // bk
    m_ref[...] = jnp.full_like(m_ref, NEG_INF)
    l_ref[...] = jnp.zeros_like(l_ref)
    acc_ref[...] = jnp.zeros_like(acc_ref)

    def group(k_starts, r0, diagonal):
        cols = bq - r0
        q = q_ref[r0:, :]
        ss = [lax.dot_general(k_ref[pl.ds(ks, bk), :], q, NT_DIMS, preferred_element_type=F32)
              for ks in k_starts]
        if diagonal:
            key = lax.broadcasted_iota(jnp.int32, (bk, cols), 0)
            qry = lax.broadcasted_iota(jnp.int32, (bk, cols), 1)
            ss = [jnp.where(key <= qry, s, NEG_INF) for s in ss]
        m_prev = m_ref[:, r0:]
        m_new = m_prev
        for s in ss:
            m_new = jnp.maximum(m_new, jnp.max(s, axis=0, keepdims=True))
        alpha = jnp.exp2(m_prev - m_new)
        l = alpha * l_ref[:, r0:]
        acc = alpha * acc_ref[:, r0:]
        for ks, s in zip(k_starts, ss):
            p = jnp.exp2(s - m_new)
            l = l + jnp.sum(p, axis=0, keepdims=True)
            acc = acc + lax.dot_general(v_ref[pl.ds(ks, bk), :], p.astype(MXU_DTYPE), TN_DIMS,
                                        preferred_element_type=F32)
        m_ref[:, r0:] = m_new
        l_ref[:, r0:] = l
        acc_ref[:, r0:] = acc

    def body(j, carry):
        group([pl.multiple_of((j * unroll + u) * bk, bk) for u in range(unroll)], 0, False)
        return carry

    assert n_sub % unroll == 0
    lax.fori_loop(0, qi * (n_sub // unroll), body, 0)
    for jj in range(n_sub):
        group([pl.multiple_of(qi * bq + jj * bk, bk)], jj * bk, True)
    o_ref[...] = (acc_ref[...] / l_ref[...]).T.astype(o_ref.dtype)


def _mla_attention(qm, km, kvf, batch, seq, bq=2048, bk=512, unroll=4):
    t = qm.shape[0]
    bq, bk = min(bq, seq), min(bk, seq)
    nq = seq // bq
    return pl.pallas_call(
        functools.partial(_mla_kernel, bq=bq, bk=bk, unroll=min(unroll, bq // bk)),
        grid=(batch, B_HEADS, nq),
        in_specs=[pl.BlockSpec((bq, MLA_QK_PAD), lambda b, hd, i: (b * nq + i, hd)),
                  pl.BlockSpec((seq, MLA_QK_PAD), lambda b, hd, i: (b, hd)),
                  pl.BlockSpec((seq, MLA_V), lambda b, hd, i: (b, B_HEADS + hd))],
        out_specs=pl.BlockSpec((bq, MLA_V), lambda b, hd, i: (b * nq + i, hd)),
        out_shape=jax.ShapeDtypeStruct((t, B_HEADS * MLA_V), MXU_DTYPE),
        scratch_shapes=[pltpu.VMEM((1, bq), F32), pltpu.VMEM((1, bq), F32),
                        pltpu.VMEM((MLA_V, bq), F32)],
        compiler_params=_params(3),
        name="mla_attention",
    )(qm, km, kvf)


def _stick_kernel(q_ref, k_ref, v_ref, o_ref, acc_ref, run_ref, *, bq, bk, unroll):
    qi = pl.program_id(2)
    n_sub = bq // bk
    tri_r = lax.broadcasted_iota(jnp.int32, (bk, bk), 0)
    tri_c = lax.broadcasted_iota(jnp.int32, (bk, bk), 1)
    at_or_after = jnp.where(tri_r >= tri_c, 1.0, 0.0).astype(MXU_DTYPE)
    acc_ref[...] = jnp.zeros_like(acc_ref)
    run_ref[...] = jnp.zeros_like(run_ref)

    def group(tiles):
        staged = []
        for ks, r0, diagonal in tiles:
            rows = bq - r0
            z = lax.dot_general(q_ref[r0:, :], k_ref[pl.ds(ks, bk), :], NT_DIMS,
                                preferred_element_type=F32)
            neg_abs = lax.bitcast_convert_type(
                lax.bitcast_convert_type(z, jnp.uint32) | jnp.uint32(0x80000000), F32)
            sp = jnp.maximum(z, 0.0) + jnp.log(1.0 + jnp.exp2(neg_abs)) * LOG2_E
            mask = None
            if diagonal:
                row = lax.broadcasted_iota(jnp.int32, (rows, bk), 0)
                col = lax.broadcasted_iota(jnp.int32, (rows, bk), 1)
                mask = col < row
                sp = jnp.where(mask, sp, 0.0)
            staged.append((z, sp, mask))
        sufs = []
        for _, sp, _ in staged:
            hi = sp.astype(MXU_DTYPE)
            lo = (sp - hi.astype(F32)).astype(MXU_DTYPE)
            sufs.append(jnp.dot(hi, at_or_after, preferred_element_type=F32)
                        + jnp.dot(lo, at_or_after, preferred_element_type=F32))
        run = run_ref[...]
        acc = acc_ref[...]
        for (ks, r0, _), (z, _, mask), suf in zip(tiles, staged, sufs):
            att = jnp.exp2(z - suf - run[r0:])
            if mask is not None:
                att = jnp.where(mask, att, 0.0)
            part = jnp.dot(att.astype(MXU_DTYPE), v_ref[pl.ds(ks, bk), :], preferred_element_type=F32)
            total = suf[:, :1]
            if r0 == 0:
                acc, run = acc + part, run + total
            else:
                acc = jnp.concatenate([acc[:r0], acc[r0:] + part], axis=0)
                run = jnp.concatenate([run[:r0], run[r0:] + total], axis=0)
        acc_ref[...] = acc
        run_ref[...] = run

    group([(pl.multiple_of(qi * bq + jj * bk, bk), jj * bk, True) for jj in reversed(range(n_sub))])

    def body(j, carry):
        group([(pl.multiple_of((qi * n_sub - 1 - j * unroll - u) * bk, bk), 0, False)
               for u in range(unroll)])
        return carry

    assert n_sub % unroll == 0
    lax.fori_loop(0, qi * (n_sub // unroll), body, 0)
    o_ref[...] = acc_ref[...].astype(o_ref.dtype)


def _stick_attention(qkv, batch, seq, bq=2048, bk=256, unroll=4):
    t = qkv.shape[0]
    bq, bk = min(bq, seq), min(bk, seq)
    nq = seq // bq
    return pl.pallas_call(
        functools.partial(_stick_kernel, bq=bq, bk=bk, unroll=min(unroll, bq // bk)),
        grid=(batch, C_HEADS, nq),
        in_specs=[pl.BlockSpec((bq, HEAD_DIM), lambda b, hd, i: (b * nq + i, hd)),
                  pl.BlockSpec((seq, HEAD_DIM), lambda b, hd, i: (b, C_HEADS + hd)),
                  pl.BlockSpec((seq, HEAD_DIM), lambda b, hd, i: (b, 2 * C_HEADS + hd))],
        out_specs=pl.BlockSpec((bq, HEAD_DIM), lambda b, hd, i: (b * nq + i, hd)),
        out_shape=jax.ShapeDtypeStruct((t, C_HEADS * HEAD_DIM), MXU_DTYPE),
        scratch_shapes=[pltpu.VMEM((bq, HEAD_DIM), F32), pltpu.VMEM((bq, 1), F32)],
        compiler_params=_params(3),
        name="stick_breaking_attention",
    )(qkv, qkv, qkv)


def _top_values(s, count, with_rank=False):
    vals = []
    cur = s
    rank = jnp.full(s.shape, float(count), F32) if with_rank else None
    for k in range(count):
        m = jnp.max(cur, axis=0, keepdims=True)
        vals.append(m)
        hit = cur == m
        if with_rank:
            rank = jnp.where(hit, float(k), rank)
        cur = jnp.where(hit, NEG_INF, cur)
    return (vals, rank) if with_rank else vals


def _ranked(s, count):
    n = s.shape[0]
    rows = lax.broadcasted_iota(jnp.int32, s.shape, 0)
    vals = []
    cur = s
    rank = jnp.full(s.shape, float(count), F32)
    for k in range(count):
        m = jnp.max(cur, axis=0, keepdims=True)
        first = jnp.min(jnp.where(cur == m, rows, n), axis=0, keepdims=True)
        pick = rows == first
        vals.append(m)
        rank = jnp.where(pick, float(k), rank)
        cur = jnp.where(pick, NEG_INF, cur)
    return vals, rank


def _pair_sums(v1, v2):
    v1_all = jnp.concatenate(v1, axis=0)
    v2_all = jnp.concatenate(v2, axis=0)
    return jnp.concatenate([v1[0] + v2_all]
                           + [v1[p] + v2_all[:8] for p in range(1, 8)]
                           + [v2[0] + v1_all[8:]], axis=0)


def _count_at_least(s, threshold):
    return jnp.sum(jnp.where(s >= threshold, 1.0, 0.0), axis=0, keepdims=True)


def _route_distinct(s1, s2):
    v1 = _top_values(s1, PEER_TOPK)
    v2, rank2 = _top_values(s2, PEER_TOPK, with_rank=True)
    cand = _pair_sums(v1, v2)
    tops = _top_values(cand, PEER_TOPK + 1)
    cut = 0.5 * (tops[PEER_TOPK - 1] + tops[PEER_TOPK])
    top = tops[0]
    z = jnp.sum(jnp.where(cand >= cut, jnp.exp(cand - top), 0.0), axis=0, keepdims=True)
    need = cut - s1
    count = jnp.zeros_like(s1)
    for q in range(PEER_TOPK):
        count = count + jnp.where(v2[q] >= need, 1.0, 0.0)
    a = jnp.where(s1 >= v1[PEER_TOPK - 1], jnp.exp(s1 - v1[0]) / (2.0 * z), 0.0)
    b = jnp.where(s2 >= v2[PEER_TOPK - 1], jnp.exp(s2 - v2[0]), 0.0)
    distinct = ((_count_at_least(s1, v1[PEER_TOPK - 1]) == PEER_TOPK)
                & (_count_at_least(s2, v2[PEER_TOPK - 1]) == PEER_TOPK)
                & (_count_at_least(cand, tops[PEER_TOPK]) == PEER_TOPK + 1))
    return rank2, b, count, a, distinct


def _route_with_ties(s1, s2):
    v1, rank1 = _ranked(s1, PEER_TOPK)
    v2, rank2 = _ranked(s2, PEER_TOPK)
    cand = _pair_sums(v1, v2)
    _, cand_rank = _ranked(cand, PEER_TOPK)
    chosen = cand_rank < PEER_TOPK
    top = v1[0] + v2[0]
    z = jnp.sum(jnp.where(chosen, jnp.exp(cand - top), 0.0), axis=0, keepdims=True)
    picked = jnp.where(chosen, 1.0, 0.0)
    per_p = [jnp.sum(picked[:16], axis=0, keepdims=True)]
    per_p += [jnp.sum(picked[8 + 8 * p:16 + 8 * p], axis=0, keepdims=True) for p in range(1, 8)]
    per_p += [picked[64 + p:65 + p] for p in range(8, 16)]
    count = jnp.zeros_like(s1)
    for p in range(PEER_TOPK):
        count = count + jnp.where(rank1 == p, per_p[p], 0.0)
    a = jnp.where(rank1 < PEER_TOPK, jnp.exp(s1 - v1[0]) / (2.0 * z), 0.0)
    b = jnp.where(rank2 < PEER_TOPK, jnp.exp(s2 - v2[0]), 0.0)
    return rank2, b, count, a


def _peer_route_kernel(q_ref, keys_ref, r2_ref, b_ref, c_ref, a_ref):
    def head(hd, carry):
        col = pl.multiple_of(hd * 2 * PEER_HALF, 2 * PEER_HALF)
        q1 = q_ref[:, pl.ds(col, PEER_HALF)].astype(MXU_DTYPE)
        q2 = q_ref[:, pl.ds(col + PEER_HALF, PEER_HALF)].astype(MXU_DTYPE)
        s1 = lax.dot_general(keys_ref[hd, 0], q1, NT_DIMS, preferred_element_type=F32)
        s2 = lax.dot_general(keys_ref[hd, 1], q2, NT_DIMS, preferred_element_type=F32)

        def store(rank2, b, count, a):
            groups = pl.ds(pl.multiple_of(hd * KEY_GROUPS, KEY_GROUPS), KEY_GROUPS)
            tokens = rank2.shape[1]
            r2_ref[groups] = rank2.astype(r2_ref.dtype).reshape(KEY_GROUPS, 16, tokens)
            b_ref[groups] = b.astype(b_ref.dtype).reshape(KEY_GROUPS, 16, tokens)
            c_ref[hd] = count
            a_ref[hd] = a

        rank2, b, count, a, distinct = _route_distinct(s1, s2)
        store(rank2, b, count, a)

        @pl.when(jnp.max(jnp.where(distinct, 0.0, 1.0)) > 0.0)
        def _():
            store(*_route_with_ties(s1, s2))

        return carry

    lax.fori_loop(0, PEER_HEADS, head, 0)


def _peer_route(qp, keys, tr=256):
    t = qp.shape[0]
    shape = (PEER_HEADS, PEER_N_KEYS, t)
    spec = pl.BlockSpec((PEER_HEADS, PEER_N_KEYS, tr), lambda i: (0, 0, i))
    flat = (PEER_HEADS * KEY_GROUPS, 16, t)
    flat_spec = pl.BlockSpec((PEER_HEADS * KEY_GROUPS, 16, tr), lambda i: (0, 0, i))
    return pl.pallas_call(
        _peer_route_kernel,
        grid=(t // tr,),
        in_specs=[pl.BlockSpec((tr, qp.shape[1]), lambda i: (i, 0)),
                  pl.BlockSpec(keys.shape, lambda i: (0, 0, 0, 0))],
        out_specs=[flat_spec, flat_spec, spec, spec],
        out_shape=[jax.ShapeDtypeStruct(flat, MXU_DTYPE), jax.ShapeDtypeStruct(flat, MXU_DTYPE),
                   jax.ShapeDtypeStruct(shape, F32), jax.ShapeDtypeStruct(shape, F32)],
        compiler_params=_params(1),
        name="peer_route",
    )(qp, keys)


def _twice_gelu(x):
    return x * (1.0 + lax.erf(x * (2.0 ** -0.5)))


def _row_tile(row, rows):
    packed = jnp.broadcast_to(row, (16, 128)).astype(MXU_DTYPE)
    return jnp.concatenate([packed] * (rows // 16), axis=0)


def _peer_kernel(xt_ref, res_ref, u_ref, v_ref, r2_ref, b_ref, c_ref, a_ref, g_ref, bias_ref,
                 o_ref, ob_ref, st_ref, wt_ref, *, n_i, tb, alpha):
    e = pl.program_id(1)

    @pl.when(e == 0)
    def _():
        o_ref[...] = jnp.zeros_like(o_ref)

    st_ref[...] = jnp.dot(u_ref[...], xt_ref[...], preferred_element_type=F32)
    zero = jnp.zeros((PEER_N_KEYS, 128), MXU_DTYPE)
    assert 8 % n_i == 0
    per_group = 8 // n_i
    within = e % per_group

    def key_rows(ref, hd, lanes):
        grp = ref[hd, 0, :, lanes]
        rows = grp[:n_i]
        for k in range(1, per_group):
            rows = jnp.where(within == k, grp[k * n_i:(k + 1) * n_i], rows)
        return rows

    live = min(n_i, 4)
    for l in range(tb // 128):
        lanes = slice(l * 128, (l + 1) * 128)
        for i0 in range(0, n_i, live):
            gates = [zero for _ in range(live)]
            for hd in range(PEER_HEADS):
                r2 = jnp.concatenate([r2_ref[hd * KEY_GROUPS + g, :, lanes] for g in range(KEY_GROUPS)],
                                     axis=0)
                b = jnp.concatenate([b_ref[hd * KEY_GROUPS + g, :, lanes] for g in range(KEY_GROUPS)],
                                    axis=0)
                counts = key_rows(c_ref, hd, lanes)
                a_rows = key_rows(a_ref, hd, lanes)
                for k in range(live):
                    count = _row_tile(counts[i0 + k:i0 + k + 1], PEER_N_KEYS)
                    a = _row_tile(a_rows[i0 + k:i0 + k + 1], PEER_N_KEYS)
                    gates[k] = gates[k] + jnp.where(r2 < count, b, zero) * a
            for k in range(live):
                rows = slice((i0 + k) * PEER_N_KEYS, (i0 + k + 1) * PEER_N_KEYS)
                act = _twice_gelu(st_ref[rows, lanes]).astype(MXU_DTYPE)
                wt_ref[rows, lanes] = act * gates[k]
    o_ref[...] += lax.dot_general(wt_ref[...], v_ref[...], TN_DIMS, preferred_element_type=F32)

    @pl.when(e == pl.num_programs(1) - 1)
    def _():
        y = _layer_norm_rows(alpha * res_ref[...] + o_ref[...], g_ref[...], bias_ref[...])
        o_ref[...] = y
        ob_ref[...] = y.astype(ob_ref.dtype)


def _peer_experts(xt, res, u, v, layer, r2, b, c, a, gain, bias, alpha, tb=512, n_i=8):
    t, d = res.shape
    n_exp = u.shape[1]
    eb = n_i * PEER_N_KEYS
    tb = min(tb, t)
    tok = lambda i, e: (i, 0)
    per_group = 8 // n_i
    c = c.reshape(PEER_HEADS, PEER_N_KEYS // 8, 8, t)
    a = a.reshape(PEER_HEADS, PEER_N_KEYS // 8, 8, t)
    full = pl.BlockSpec((PEER_HEADS, 1, 8, tb), lambda i, e: (0, e // per_group, 0, i))
    flat = pl.BlockSpec((PEER_HEADS * KEY_GROUPS, 16, tb), lambda i, e: (0, 0, i))
    return pl.pallas_call(
        functools.partial(_peer_kernel, n_i=n_i, tb=tb, alpha=alpha),
        grid=(t // tb, n_exp // eb),
        in_specs=[pl.BlockSpec((d, tb), lambda i, e: (0, i)), pl.BlockSpec((tb, d), tok),
                  pl.BlockSpec((None, eb, d), lambda i, e: (layer, e, 0)),
                  pl.BlockSpec((None, eb, d), lambda i, e: (layer, e, 0)),
                  flat, flat, full, full,
                  pl.BlockSpec((1, d), lambda i, e: (0, 0)), pl.BlockSpec((1, d), lambda i, e: (0, 0))],
        out_specs=[pl.BlockSpec((tb, d), tok), pl.BlockSpec((tb, d), tok)],
        out_shape=[jax.ShapeDtypeStruct((t, d), F32), jax.ShapeDtypeStruct((t, d), MXU_DTYPE)],
        scratch_shapes=[pltpu.VMEM((eb, tb), F32), pltpu.VMEM((eb, tb), MXU_DTYPE)],
        compiler_params=_params(2),
        name="peer_experts",
    )(xt, res, u, v, r2, b, c, a, gain.reshape(1, d), bias.reshape(1, d))


def _peer_layer(x, xb, w_query, keys, u, v, layer, gain, bias, alpha):
    qp = _matmul(xb, w_query, F32, bn=w_query.shape[1])
    r2, b, c, a = _peer_route(qp, keys)
    return _peer_experts(xb.T, x, u, v, layer, r2, b, c, a, gain, bias, alpha)


def _even_mixer(x2d, batch, seq, w_in, q_norm, w_q_b, kv_norm, w_kv_b, tables):
    d = x2d.shape[1]
    w_in_p = jnp.concatenate(
        [w_in, jnp.zeros((d, 4096 - w_in.shape[1]), w_in.dtype)], axis=1).astype(MXU_DTYPE)
    h = _matmul(x2d, w_in_p, F32)
    cos_a, sin_a, cos_b, sin_lo, sin_hi = tables
    qa, ka, cqn, ckvn, kr = _even_prep(h, tables, q_norm, kv_norm, seq)
    out_a = _dilated_attention(qa, ka, h, batch, seq)
    wq = w_q_b.reshape(MLA_Q_LORA, B_HEADS, MLA_NOPE + MLA_ROPE)
    wq = jnp.pad(wq, ((0, 0), (0, 0), (0, MLA_QK_PAD - MLA_NOPE - MLA_ROPE)))
    wq = wq.reshape(MLA_Q_LORA, B_HEADS * MLA_QK_PAD).astype(MXU_DTYPE)
    wkv = w_kv_b.reshape(MLA_KV_LORA, B_HEADS, MLA_NOPE + MLA_V)
    wkv = jnp.concatenate([wkv[:, :, :MLA_NOPE].reshape(MLA_KV_LORA, -1),
                           wkv[:, :, MLA_NOPE:].reshape(MLA_KV_LORA, -1)], axis=1).astype(MXU_DTYPE)
    qf = _matmul(cqn, wq, F32)
    kvf = _matmul(ckvn, wkv, MXU_DTYPE)
    qm, km = _mla_prep(qf, kvf, kr, (cos_b, sin_lo, sin_hi), seq)
    out_b = _mla_attention(qm, km, kvf, batch, seq)
    return out_a, out_b


def kernel(x, a_w_in, b_q_norm, b_w_q_b, b_kv_norm, b_w_kv_b, ab_w_out, c_w_in, c_w_out,
           peer_w_query, peer_sub_keys, peer_u, peer_v, ln_gain, ln_bias):
    batch, seq, d = x.shape
    depth = peer_u.shape[0]
    alpha = (2 * depth) ** 0.25
    tables = _rope_tables(seq)
    u_all, v_all = peer_u.astype(MXU_DTYPE), peer_v.astype(MXU_DTYPE)
    xf = x.reshape(batch * seq, d)
    xb = None
    for layer in range(depth):
        i = layer // 2
        if layer % 2 == 0:
            src = xf if xb is None else xb
            out_a, out_b = _even_mixer(src, batch, seq, a_w_in[i], b_q_norm[i], b_w_q_b[i],
                                       b_kv_norm[i], b_w_kv_b[i], tables)
            w_out = ab_w_out[i].astype(MXU_DTYPE)
            half = A_HEADS * HEAD_DIM
            xf, xb = _matmul_residual_ln([out_a, out_b], [w_out[:half], w_out[half:]], xf,
                                         ln_gain[layer, 0], ln_bias[layer, 0], alpha)
        else:
            src = xf if xb is None else xb
            n_q = C_HEADS * HEAD_DIM
            w_qkv = jnp.concatenate([c_w_in[i][:, :n_q] * (HEAD_DIM ** -0.5 * LOG2_E),
                                     c_w_in[i][:, n_q:]], axis=1).astype(MXU_DTYPE)
            qkv = _matmul(src, w_qkv, MXU_DTYPE)
            o = _stick_attention(qkv, batch, seq)
            xf, xb = _matmul_residual_ln([o], [c_w_out[i].astype(MXU_DTYPE)], xf,
                                         ln_gain[layer, 0], ln_bias[layer, 0], alpha)
        xf, xb = _peer_layer(xf, xb, peer_w_query[layer].astype(MXU_DTYPE),
                         peer_sub_keys[layer].astype(MXU_DTYPE),
                         u_all, v_all, layer,
                         ln_gain[layer, 1], ln_bias[layer, 1], alpha)
    return xf.reshape(batch, seq, d)
```
